```python
import jax, jax.numpy as jnp
from jax import lax
import numpy as np

D_MODEL = 1024
BATCH = 4
SEQ = 8192
DEPTH = 1

CHUNK = 64
Q_BLOCK = 2 * CHUNK
HEAD_DIM = 64
N_FOX_HEADS = 8
N_SB_HEADS = 8
FOX_WIDTH = N_FOX_HEADS * HEAD_DIM
SB_WIDTH = N_SB_HEADS * HEAD_DIM
MIX_WIDTH = FOX_WIDTH + SB_WIDTH
IN_COLS = 3 * FOX_WIDTH + N_FOX_HEADS + 3 * SB_WIDTH
PEER_HEADS = 8
PEER_N_KEYS = 128
PEER_N_EXPERTS = PEER_N_KEYS * PEER_N_KEYS
PEER_TOPK = 16
PEER_D_KEY = 256
PEER_HALF = PEER_D_KEY // 2
PEER_TOKEN_BLOCK = 128
N_MOD = 6
EPS = 1e-6

kernel_name = "hybrid_fox_stickbreak_peer_adaln_block"


def rms_norm(x, g):
    xf = x.astype(jnp.float32)
    inv = lax.rsqrt(jnp.mean(xf * xf, axis=-1, keepdims=True) + EPS)
    return (xf * inv).astype(x.dtype) * g


def split_heads(t, n_heads):
    b, s, _ = t.shape
    return t.reshape(b, s, n_heads, HEAD_DIM).transpose(0, 2, 1, 3)


def merge_blocks(o):
    nb, b, h, qb, dh = o.shape
    return o.transpose(1, 0, 3, 2, 4).reshape(b, nb * qb, h, dh)


def forgetting_attention(q, k, v, log_fcum):
    seq = q.shape[2]
    n_blocks = seq // Q_BLOCK
    scale = HEAD_DIM ** -0.5
    k_pos = jnp.arange(seq)

    def block(i):
        start = i * Q_BLOCK
        qb = lax.dynamic_slice_in_dim(q, start, Q_BLOCK, axis=2)
        fq = lax.dynamic_slice_in_dim(log_fcum, start, Q_BLOCK, axis=2)
        q_pos = start + jnp.arange(Q_BLOCK)
        logits = jnp.einsum('bhqd,bhkd->bhqk', qb, k, preferred_element_type=jnp.float32) * scale
        logits = logits + fq[..., :, None] - log_fcum[..., None, :]
        mask = k_pos[None, :] <= q_pos[:, None]
        logits = jnp.where(mask, logits, -jnp.inf)
        p = jax.nn.softmax(logits, axis=-1)
        return jnp.einsum('bhqk,bhkd->bhqd', p.astype(v.dtype), v)

    return merge_blocks(lax.map(block, jnp.arange(n_blocks)))


def stick_breaking_attention(q, k, v):
    seq = q.shape[2]
    n_blocks = seq // Q_BLOCK
    scale = HEAD_DIM ** -0.5
    k_pos = jnp.arange(seq)

    def block(i):
        start = i * Q_BLOCK
        qb = lax.dynamic_slice_in_dim(q, start, Q_BLOCK, axis=2)
        q_pos = start + jnp.arange(Q_BLOCK)
        z = jnp.einsum('bhqd,bhkd->bhqk', qb, k, preferred_element_type=jnp.float32) * scale
        mask = k_pos[None, :] < q_pos[:, None]
        log_beta = jax.nn.log_sigmoid(z)
        log_one_minus = jnp.where(mask, jax.nn.log_sigmoid(-z), 0.0)
        rest = lax.cumsum(log_one_minus, axis=3, reverse=True) - log_one_minus
        a = jnp.where(mask, jnp.exp(log_beta + rest), 0.0)
        return jnp.einsum('bhqk,bhkd->bhqd', a.astype(v.dtype), v)

    return merge_blocks(lax.map(block, jnp.arange(n_blocks)))


def peer_layer(h, w_query, sub_keys, expert_down, expert_up):
    b, s, d = h.shape
    t = b * s
    ht = h.reshape(t, d)
    query = (ht @ w_query).reshape(t, PEER_HEADS, 2, PEER_HALF)
    sub_scores = jnp.einsum('thpd,hpkd->thpk', query, sub_keys,
                            preferred_element_type=jnp.float32)
    half_vals, half_idx = lax.top_k(sub_scores, PEER_TOPK)
    cand = (half_vals[:, :, 0, :, None] + half_vals[:, :, 1, None, :]
            ).reshape(t, PEER_HEADS, PEER_TOPK * PEER_TOPK)
    top_vals, top_pos = lax.top_k(cand, PEER_TOPK)
    i1 = jnp.take_along_axis(half_idx[:, :, 0, :], top_pos // PEER_TOPK, axis=-1)
    i2 = jnp.take_along_axis(half_idx[:, :, 1, :], top_pos % PEER_TOPK, axis=-1)
    experts = i1 * PEER_N_KEYS + i2
    gates = jax.nn.softmax(top_vals, axis=-1).astype(h.dtype)

    n_blk = t // PEER_TOKEN_BLOCK
    xs = (ht.reshape(n_blk, PEER_TOKEN_BLOCK, d),
          experts.reshape(n_blk, PEER_TOKEN_BLOCK, PEER_HEADS, PEER_TOPK),
          gates.reshape(n_blk, PEER_TOKEN_BLOCK, PEER_HEADS, PEER_TOPK))

    def block(args):
        xb, eb, gb = args
        u = expert_down[eb]
        act = jax.nn.gelu(jnp.einsum('td,thkd->thk', xb, u), approximate=False) * gb
        vv = expert_up[eb]
        return jnp.einsum('thk,thkd->td', act, vv)

    return lax.map(block, xs).reshape(b, s, d)


def setup_inputs(seed: int = 0) -> dict:
    key = jax.random.key(seed)
    ks = jax.random.split(key, 17)
    nrm = jax.random.normal
    L, D = DEPTH, D_MODEL
    return {
        "x": nrm(ks[0], (BATCH, SEQ, D), jnp.float32),
        "c": nrm(ks[1], (BATCH, D), jnp.float32),
        "w_ada": nrm(ks[2], (L, D, N_MOD * D), jnp.float32) * D ** -0.5,
        "b_ada": 0.02 * nrm(ks[3], (L, N_MOD * D), jnp.float32),
        "g_norm_mix": 1.0 + 0.05 * nrm(ks[4], (L, D), jnp.float32),
        "w_in": nrm(ks[5], (L, D, IN_COLS), jnp.float32) * D ** -0.5,
        "b_forget": 1.5 + 0.5 * nrm(ks[6], (L, N_FOX_HEADS), jnp.float32),
        "g_out_fox": 1.0 + 0.05 * nrm(ks[7], (L, HEAD_DIM), jnp.float32),
        "g_out_sb": 1.0 + 0.05 * nrm(ks[8], (L, HEAD_DIM), jnp.float32),
        "w_out": nrm(ks[9], (L, MIX_WIDTH, D), jnp.float32) * MIX_WIDTH ** -0.5,
        "g_norm_ffn": 1.0 + 0.05 * nrm(ks[10], (L, D), jnp.float32),
        "w_query": nrm(ks[11], (L, D, PEER_HEADS * PEER_D_KEY), jnp.float32) * D ** -0.5,
        "sub_keys": nrm(ks[12], (L, PEER_HEADS, 2, PEER_N_KEYS, PEER_HALF), jnp.float32) * PEER_HALF ** -0.5,
        "expert_down": nrm(ks[13], (L, PEER_N_EXPERTS, D), jnp.float32) * D ** -0.5,
        "expert_up": nrm(ks[14], (L, PEER_N_EXPERTS, D), jnp.float32) * PEER_HEADS ** -0.5,
        "g_final": 1.0 + 0.05 * nrm(ks[15], (D,), jnp.float32),
    }


def reference(x, c, w_ada, b_ada, g_norm_mix, w_in, b_forget, g_out_fox, g_out_sb,
              w_out, g_norm_ffn, w_query, sub_keys, expert_down, expert_up, g_final):
    b, s, d = x.shape
    c_act = jax.nn.silu(c)
    for l in range(DEPTH):
        mod = c_act @ w_ada[l] + b_ada[l]
        sh1, sc1, gt1, sh2, sc2, gt2 = [m[:, None, :] for m in jnp.split(mod, N_MOD, axis=-1)]

        h = rms_norm(x, g_norm_mix[l]) * (1.0 + sc1) + sh1
        proj = h @ w_in[l]
        o1 = 3 * FOX_WIDTH
        o2 = o1 + N_FOX_HEADS
        fq, fk, fv = jnp.split(proj[..., :o1], 3, axis=-1)
        f_logit = proj[..., o1:o2]
        sq, sk, sv = jnp.split(proj[..., o2:], 3, axis=-1)

        log_f = jax.nn.log_sigmoid((f_logit + b_forget[l]).astype(jnp.float32))
        log_fcum = lax.cumsum(log_f, axis=1).transpose(0, 2, 1)
        fox_out = forgetting_attention(split_heads(fq, N_FOX_HEADS), split_heads(fk, N_FOX_HEADS),
                                       split_heads(fv, N_FOX_HEADS), log_fcum)
        sb_out = stick_breaking_attention(split_heads(sq, N_SB_HEADS), split_heads(sk, N_SB_HEADS),
                                          split_heads(sv, N_SB_HEADS))
        mixed = jnp.concatenate([rms_norm(fox_out, g_out_fox[l]).reshape(b, s, FOX_WIDTH),
                                 rms_norm(sb_out, g_out_sb[l]).reshape(b, s, SB_WIDTH)], axis=-1)
        x = x + gt1 * (mixed @ w_out[l])

        h2 = rms_norm(x, g_norm_ffn[l]) * (1.0 + sc2) + sh2
        x = x + gt2 * peer_layer(h2, w_query[l], sub_keys[l], expert_down[l], expert_up[l])
    return rms_norm(x, g_final)
```

```python
import functools

import numpy as np
import jax
import jax.numpy as jnp
from jax import lax
from jax.experimental import pallas as pl
from jax.experimental.pallas import tpu as pltpu

F32 = jnp.float32
BF16 = jnp.bfloat16

HEAD_DIM = 64
N_HEADS = 8
PEER_HEADS = 8
PEER_N_KEYS = 128
PEER_TOPK = 16
TOPK_SHIFT = PEER_TOPK.bit_length() - 1
N_MOD = 6
EPS = 1e-6

LANES = 128
VMEM_LIMIT = 56 * 1024 * 1024

HP = N_HEADS * LANES
LANE_ONE_Q = HEAD_DIM
LANE_F_Q = HEAD_DIM + 3
LANE_ONE_V = HEAD_DIM
NEG_INF = float("-inf")


def _cparams(*sem):
    return pltpu.CompilerParams(dimension_semantics=sem, vmem_limit_bytes=VMEM_LIMIT)


def _dot(a, b):
    return jnp.dot(a, b, preferred_element_type=F32)


def _dot_nt(a, b):
    return lax.dot_general(a, b, (((1,), (1,)), ((), ())), preferred_element_type=F32)


def _split2(x):
    hi = x.astype(BF16)
    lo = (x - hi.astype(F32)).astype(BF16)
    return hi, lo


def _split3(x):
    hi = x.astype(BF16)
    r = x - hi.astype(F32)
    mid = r.astype(BF16)
    lo = (r - mid.astype(F32)).astype(BF16)
    return hi, mid, lo


def _softplus(z):
    return jnp.maximum(z, 0.0) + jnp.log(1.0 + jnp.exp(-jnp.abs(z)))


def _ada_kernel(c_ref, w_ref, b_ref, o_ref):
    c = c_ref[...]
    a = c / (1.0 + jnp.exp(-c))
    ah, al = _split2(a)
    wh, wl = _split2(w_ref[...])
    o_ref[...] = _dot(ah, wh) + _dot(ah, wl) + _dot(al, wh) + b_ref[...]


def _ada(c_pad, w_ada, b_ada):
    rows, d = c_pad.shape
    n = w_ada.shape[1]
    bn = 1024
    return pl.pallas_call(
        _ada_kernel,
        grid=(n // bn,),
        in_specs=[pl.BlockSpec((rows, d), lambda j: (0, 0)),
                  pl.BlockSpec((d, bn), lambda j: (0, j)),
                  pl.BlockSpec((1, bn), lambda j: (0, j))],
        out_specs=pl.BlockSpec((rows, bn), lambda j: (0, j)),
        out_shape=jax.ShapeDtypeStruct((rows, n), F32),
        compiler_params=_cparams("arbitrary"),
        name="ada",
    )(c_pad, w_ada, b_ada)


def _wc_kernel(sk_ref, wq_ref, o_ref):
    sh, sl = _split2(sk_ref[0])
    wh, wl = _split2(wq_ref[...])
    o_ref[...] = _dot_nt(sh, wh) + _dot_nt(sh, wl) + _dot_nt(sl, wh)


def _wc(sub_keys, w_query):
    d = w_query.shape[0]
    nhp = sub_keys.shape[0]
    return pl.pallas_call(
        _wc_kernel,
        grid=(nhp,),
        in_specs=[pl.BlockSpec((1, PEER_N_KEYS, LANES), lambda i: (i, 0, 0)),
                  pl.BlockSpec((d, LANES), lambda i: (0, i))],
        out_specs=pl.BlockSpec((PEER_N_KEYS, d), lambda i: (i, 0)),
        out_shape=jax.ShapeDtypeStruct((nhp * PEER_N_KEYS, d), F32),
        compiler_params=_cparams("arbitrary"),
        name="wc",
    )(sub_keys, w_query)


def _inproj_kernel(x_ref, mod_ref, g_ref, w_ref, wfh_ref, wfl_ref, bf_ref, pq_ref, pk_ref,
                   cq_ref, ck_ref, cv_ref,
                   fq_ref, fk_ref, fv_ref, sq_ref, sk_ref, sv_ref, carry_ref):
    tm = x_ref.shape[1]

    @pl.when(pl.program_id(1) == 0)
    def _():
        carry_ref[...] = jnp.zeros_like(carry_ref)

    x = x_ref[0]
    inv = lax.rsqrt(jnp.mean(x * x, axis=-1, keepdims=True) + EPS)
    sh1 = mod_ref[0, 0:1, :]
    sc1 = mod_ref[0, 1:2, :]
    h = (x * inv) * g_ref[...] * (1.0 + sc1) + sh1
    hh, hl = _split2(h)

    fl = _dot(hh, wfh_ref[...]) + _dot(hh, wfl_ref[...]) + _dot(hl, wfh_ref[...]) + bf_ref[...]
    lf = jnp.minimum(fl, 0.0) - jnp.log(1.0 + jnp.exp(-jnp.abs(fl)))
    row = lax.broadcasted_iota(jnp.int32, (tm, tm), 0)
    col = lax.broadcasted_iota(jnp.int32, (tm, tm), 1)
    tri = jnp.where(col <= row, 1.0, 0.0).astype(BF16)
    l0, l1, l2 = _split3(lf)
    fcum = _dot(tri, l0) + _dot(tri, l1) + _dot(tri, l2) + carry_ref[...]
    carry_ref[...] = fcum[tm - 1:tm, :]
    f0, f1, f2 = _split3(fcum)
    fcat = jnp.concatenate([f0, f1, f2], axis=1)

    fq_ref[0] = (_dot(hh, w_ref[:, 0 * HP:1 * HP]) + _dot(fcat, pq_ref[...]) + cq_ref[...]).astype(BF16)
    fk_ref[0] = (_dot(hh, w_ref[:, 1 * HP:2 * HP]) + _dot(fcat, pk_ref[...]) + ck_ref[...]).astype(BF16)
    fv_ref[0] = (_dot(hh, w_ref[:, 2 * HP:3 * HP]) + cv_ref[...]).astype(BF16)
    sq_ref[0] = _dot(hh, w_ref[:, 3 * HP:4 * HP]).astype(BF16)
    sk_ref[0] = _dot(hh, w_ref[:, 4 * HP:5 * HP]).astype(BF16)
    sv_ref[0] = _dot(hh, w_ref[:, 5 * HP:6 * HP]).astype(BF16)


def _inproj(x, mod3, g_mix, w_all, wf_hi, wf_lo, b_f, pq, pk, cq, ck, cv, tm):
    b, s, d = x.shape
    const = lambda shape: pl.BlockSpec(shape, lambda bi, si: (0,) * len(shape),
                                       pipeline_mode=pl.Buffered(1))
    out_spec = pl.BlockSpec((1, tm, HP), lambda bi, si: (bi, si, 0))
    out_sds = jax.ShapeDtypeStruct((b, s, HP), BF16)
    return pl.pallas_call(
        _inproj_kernel,
        grid=(b, s // tm),
        in_specs=[pl.BlockSpec((1, tm, d), lambda bi, si: (bi, si, 0)),
                  pl.BlockSpec((1, N_MOD, d), lambda bi, si: (bi, 0, 0)),
                  const((1, d)),
                  const((d, 6 * HP)),
                  const((d, LANES)), const((d, LANES)), const((1, LANES)),
                  const((3 * LANES, HP)), const((3 * LANES, HP)),
                  const((1, HP)), const((1, HP)), const((1, HP))],
        out_specs=[out_spec] * 6,
        out_shape=[out_sds] * 6,
        scratch_shapes=[pltpu.VMEM((1, LANES), F32)],
        compiler_params=_cparams("arbitrary", "arbitrary"),
        name="inproj",
    )(x, mod3, g_mix, w_all, wf_hi, wf_lo, b_f, pq, pk, cq, ck, cv)


def _head_norm(o, g):
    lane = lax.broadcasted_iota(jnp.int32, o.shape, 1)
    o = jnp.where(lane < HEAD_DIM, o, 0.0)
    ms = jnp.sum(o * o, axis=-1, keepdims=True) * (1.0 / HEAD_DIM)
    return o * lax.rsqrt(ms + EPS) * g


def _fox_kernel(q_ref, k_ref, v_ref, g_ref, o_ref, m_ref, acc_ref, *, tk):
    tq = q_ref.shape[1]
    qi = pl.program_id(2)
    q = q_ref[0]
    m_ref[...] = jnp.full_like(m_ref, NEG_INF)
    acc_ref[...] = jnp.zeros_like(acc_ref)

    def step(start, mask):
        k = k_ref[0, pl.ds(start, tk), :]
        v = v_ref[0, pl.ds(start, tk), :]
        s = _dot_nt(q, k)
        if mask is not None:
            s = jnp.where(mask, s, NEG_INF)
        m_prev = m_ref[...]
        m_new = jnp.maximum(m_prev, jnp.max(s, axis=-1, keepdims=True))
        p = jnp.exp(s - m_new)
        acc_ref[...] = jnp.exp(m_prev - m_new) * acc_ref[...] + _dot(p.astype(BF16), v)
        m_ref[...] = m_new

    n_diag = tq // tk
    n_full = qi * n_diag

    def body(kb, carry):
        step(pl.multiple_of(kb * tk, tk), None)
        return carry

    lax.fori_loop(0, n_full, body, 0)
    row = lax.broadcasted_iota(jnp.int32, (tq, tk), 0)
    col = lax.broadcasted_iota(jnp.int32, (tq, tk), 1)
    for d in range(n_diag):
        step(pl.multiple_of((n_full + d) * tk, tk), col + d * tk <= row)

    acc = acc_ref[...]
    lane = lax.broadcasted_iota(jnp.int32, acc.shape, 1)
    l = jnp.sum(jnp.where(lane == LANE_ONE_V, acc, 0.0), axis=-1, keepdims=True)
    o_ref[0] = _head_norm(acc / l, g_ref[...]).astype(o_ref.dtype)


def _fox(q, k, v, g, tq, tk):
    b, s, _ = q.shape
    return pl.pallas_call(
        functools.partial(_fox_kernel, tk=tk),
        grid=(b, N_HEADS, s // tq),
        in_specs=[pl.BlockSpec((1, tq, LANES), lambda bi, hi, qi: (bi, qi, hi)),
                  pl.BlockSpec((1, s, LANES), lambda bi, hi, qi: (bi, 0, hi)),
                  pl.BlockSpec((1, s, LANES), lambda bi, hi, qi: (bi, 0, hi)),
                  pl.BlockSpec((1, LANES), lambda bi, hi, qi: (0, 0))],
        out_specs=pl.BlockSpec((1, tq, LANES), lambda bi, hi, qi: (bi, qi, hi)),
        out_shape=jax.ShapeDtypeStruct((b, s, HP), BF16),
        scratch_shapes=[pltpu.VMEM((tq, 1), F32), pltpu.VMEM((tq, LANES), F32)],
        compiler_params=_cparams("arbitrary", "arbitrary", "arbitrary"),
        name="fox",
    )(q, k, v, g)


def _sb_kernel(q_ref, k_ref, v_ref, g_ref, o_ref, c_ref, acc_ref, *, tk):
    tq = q_ref.shape[1]
    qi = pl.program_id(2)
    q = q_ref[0]
    c_ref[...] = jnp.zeros_like(c_ref)
    acc_ref[...] = jnp.zeros_like(acc_ref)
    jrow = lax.broadcasted_iota(jnp.int32, (tk, tk), 0)
    scol = lax.broadcasted_iota(jnp.int32, (tk, tk), 1)
    later = jnp.where(jrow > scol, 1.0, 0.0).astype(BF16)

    def step(start, mask):
        k = k_ref[0, pl.ds(start, tk), :]
        v = v_ref[0, pl.ds(start, tk), :]
        z = _dot_nt(q, k)
        sp = _softplus(z)
        spm = sp if mask is None else jnp.where(mask, sp, 0.0)
        hi, lo = _split2(spm)
        rest = _dot(hi, later) + _dot(lo, later) + c_ref[...]
        a = jnp.exp(z - sp - rest)
        if mask is not None:
            a = jnp.where(mask, a, 0.0)
        acc_ref[...] += _dot(a.astype(BF16), v)
        c_ref[...] += jnp.sum(spm, axis=-1, keepdims=True)

    n_diag = tq // tk
    n_full = qi * n_diag
    row = lax.broadcasted_iota(jnp.int32, (tq, tk), 0)
    col = lax.broadcasted_iota(jnp.int32, (tq, tk), 1)
    for d in reversed(range(n_diag)):
        step(pl.multiple_of((n_full + d) * tk, tk), col + d * tk < row)

    def body(i, carry):
        step(pl.multiple_of((n_full - 1 - i) * tk, tk), None)
        return carry

    lax.fori_loop(0, n_full, body, 0)
    o_ref[0] = _head_norm(acc_ref[...], g_ref[...]).astype(o_ref.dtype)


def _sb(q, k, v, g, tq, tk):
    b, s, _ = q.shape
    return pl.pallas_call(
        functools.partial(_sb_kernel, tk=tk),
        grid=(b, N_HEADS, s // tq),
        in_specs=[pl.BlockSpec((1, tq, LANES), lambda bi, hi, qi: (bi, qi, hi)),
                  pl.BlockSpec((1, s, LANES), lambda bi, hi, qi: (bi, 0, hi)),
                  pl.BlockSpec((1, s, LANES), lambda bi, hi, qi: (bi, 0, hi)),
                  pl.BlockSpec((1, LANES), lambda bi, hi, qi: (0, 0))],
        out_specs=pl.BlockSpec((1, tq, LANES), lambda bi, hi, qi: (bi, qi, hi)),
        out_shape=jax.ShapeDtypeStruct((b, s, HP), BF16),
        scratch_shapes=[pltpu.VMEM((tq, 1), F32), pltpu.VMEM((tq, LANES), F32)],
        compiler_params=_cparams("arbitrary", "arbitrary", "arbitrary"),
        name="sb",
    )(q, k, v, g)


def _outproj_kernel(x_ref, fo_ref, so_ref, mod_ref, g_ref, wo_ref, wch_ref, wcl_ref,
                    x1_ref, h2t_ref, sc_ref):
    gt1 = mod_ref[0, 2:3, :]
    sh2 = mod_ref[0, 3:4, :]
    sc2 = mod_ref[0, 4:5, :]
    mixed = _dot(fo_ref[0], wo_ref[0:HP, :]) + _dot(so_ref[0], wo_ref[HP:2 * HP, :])
    x1 = x_ref[0] + gt1 * mixed
    x1_ref[0] = x1
    inv = lax.rsqrt(jnp.mean(x1 * x1, axis=-1, keepdims=True) + EPS)
    h2 = (x1 * inv) * g_ref[...] * (1.0 + sc2) + sh2
    h2t = h2.T
    th, tl = _split2(h2t)
    h2t_ref[...] = th
    sc_ref[...] = _dot(wch_ref[...], th) + _dot(wch_ref[...], tl) + _dot(wcl_ref[...], th)


def _outproj(x, fo, so, mod3, g_ffn, w_out_p, wc_hi, wc_lo, tm):
    b, s, d = x.shape
    t = b * s
    nsc = wc_hi.shape[0]
    const = lambda shape: pl.BlockSpec(shape, lambda bi, si: (0,) * len(shape),
                                       pipeline_mode=pl.Buffered(1))
    spb = s // tm
    return pl.pallas_call(
        _outproj_kernel,
        grid=(b, spb),
        in_specs=[pl.BlockSpec((1, tm, d), lambda bi, si: (bi, si, 0)),
                  pl.BlockSpec((1, tm, HP), lambda bi, si: (bi, si, 0)),
                  pl.BlockSpec((1, tm, HP), lambda bi, si: (bi, si, 0)),
                  pl.BlockSpec((1, N_MOD, d), lambda bi, si: (bi, 0, 0)),
                  const((1, d)),
                  const((2 * HP, d)),
                  const((nsc, d)), const((nsc, d))],
        out_specs=[pl.BlockSpec((1, tm, d), lambda bi, si: (bi, si, 0)),
                   pl.BlockSpec((d, tm), lambda bi, si: (0, bi * spb + si)),
                   pl.BlockSpec((nsc, tm), lambda bi, si: (0, bi * spb + si))],
        out_shape=[jax.ShapeDtypeStruct((b, s, d), F32),
                   jax.ShapeDtypeStruct((d, t), BF16),
                   jax.ShapeDtypeStruct((nsc, t), F32)],
        compiler_params=_cparams("arbitrary", "arbitrary"),
        name="outproj",
    )(x, fo, so, mod3, g_ffn, w_out_p, wc_hi, wc_lo)


def _top16(s):
    n, t = s.shape
    idx = lax.broadcasted_iota(jnp.int32, (n, t), 0)
    krow = lax.broadcasted_iota(jnp.int32, (PEER_TOPK, t), 0)
    rank = jnp.full((n, t), float(PEER_TOPK), F32)
    vals = jnp.zeros((PEER_TOPK, t), F32)
    for k in range(PEER_TOPK):
        m = jnp.max(s, axis=0, keepdims=True)
        first = jnp.min(jnp.where(s == m, idx, n), axis=0, keepdims=True)
        hit = idx == first
        rank = jnp.where(hit, float(k), rank)
        s = jnp.where(hit, NEG_INF, s)
        vals = jnp.where(krow == k, m, vals)
    return vals, rank


def _topk_kernel(sc_ref, r2_ref, e2_ref, l1_ref, e1_ref):
    n = PEER_N_KEYS
    tl = sc_ref.shape[1]
    ngrp = tl // LANES

    def body(it, carry):
        hd = lax.div(it, ngrp)
        gi = lax.rem(it, ngrp)
        lanes = pl.ds(pl.multiple_of(gi * LANES, LANES), LANES)
        s1 = sc_ref[pl.ds(pl.multiple_of(hd * 2 * n, n), n), lanes]
        s2 = sc_ref[pl.ds(pl.multiple_of(hd * 2 * n + n, n), n), lanes]
        v1, r1 = _top16(s1)
        v2, r2 = _top16(s2)
        cand = jnp.concatenate([v1[a:a + 1, :] + v2 for a in range(PEER_TOPK)], axis=0)
        pos = lax.broadcasted_iota(jnp.int32, cand.shape, 0)
        arow = lax.broadcasted_iota(jnp.int32, (PEER_TOPK, LANES), 0)
        cnt = jnp.zeros((PEER_TOPK, LANES), F32)
        zsum = jnp.zeros((1, LANES), F32)
        c0 = cand[0:1, :]
        for k in range(PEER_TOPK):
            m = jnp.max(cand, axis=0, keepdims=True)
            first = jnp.min(jnp.where(cand == m, pos, PEER_TOPK * PEER_TOPK), axis=0, keepdims=True)
            cand = jnp.where(pos == first, NEG_INF, cand)
            cnt = cnt + jnp.where(arow == jnp.right_shift(first, TOPK_SHIFT), 1.0, 0.0)
            zsum = zsum + jnp.exp(m - c0)
        lrow = jnp.zeros((n, LANES), F32)
        for a in range(PEER_TOPK):
            lrow = jnp.where(r1 == float(a), cnt[a:a + 1, :], lrow)
        r2_ref[hd, :, lanes] = r2
        e2_ref[hd, :, lanes] = jnp.exp(s2 - v2[0:1, :])
        l1_ref[hd, :, lanes] = lrow
        e1_ref[hd, :, lanes] = jnp.exp(s1 - v1[0:1, :]) / zsum
        return carry

    lax.fori_loop(0, PEER_HEADS * ngrp, body, 0)


def _topk(scores_t, tl):
    nsc, t = scores_t.shape
    spec = pl.BlockSpec((PEER_HEADS, PEER_N_KEYS, tl), lambda i: (0, 0, i))
    sds = jax.ShapeDtypeStruct((PEER_HEADS, PEER_N_KEYS, t), F32)
    return pl.pallas_call(
        _topk_kernel,
        grid=(t // tl,),
        in_specs=[pl.BlockSpec((nsc, tl), lambda i: (0, i))],
        out_specs=[spec] * 4,
        out_shape=[sds] * 4,
        compiler_params=_cparams("arbitrary"),
        name="topk",
    )(scores_t)


def _gelu(x):
    return 0.5 * x * (1.0 + lax.erf(x * (2.0 ** -0.5)))


def _peer_kernel(h2t_ref, dn_ref, up_ref, r2_ref, e2_ref, l1_ref, e1_ref, x1_ref, mod_ref, gf_ref,
                 o_ref, acc_ref, st_ref, pw_ref, *, rows_per_chunk):
    j = pl.program_id(2)
    n = PEER_N_KEYS

    @pl.when(j == 0)
    def _():
        acc_ref[...] = jnp.zeros_like(acc_ref)

    st_ref[...] = _dot(dn_ref[...], h2t_ref[...])
    for i1 in range(rows_per_chunk):
        rows = slice(i1 * n, (i1 + 1) * n)
        for g in range(h2t_ref.shape[1] // LANES):
            lanes = slice(g * LANES, (g + 1) * LANES)
            w = None
            for hd in range(PEER_HEADS):
                sel = r2_ref[hd, :, lanes] < l1_ref[hd, i1:i1 + 1, lanes]
                wh = jnp.where(sel, e2_ref[hd, :, lanes], 0.0) * e1_ref[hd, i1:i1 + 1, lanes]
                w = wh if w is None else w + wh
            pw_ref[rows, lanes] = (_gelu(st_ref[rows, lanes]) * w).astype(BF16)
    acc_ref[...] += _dot(up_ref[...], pw_ref[...])

    @pl.when(j == pl.num_programs(2) - 1)
    def _():
        gt2 = mod_ref[0, 5:6, :]
        x2 = x1_ref[0] + gt2 * acc_ref[...].T
        inv = lax.rsqrt(jnp.mean(x2 * x2, axis=-1, keepdims=True) + EPS)
        o_ref[0] = (x2 * inv) * gf_ref[...]


def _peer(h2t, down, up_t, r2, e2, l1, e1, x1, mod3, g_final, tt, ec):
    b, s, d = x1.shape
    ne = down.shape[0]
    rpc = ec // PEER_N_KEYS
    spb = s // tt
    tok = lambda bi, si, j: (0, 0, bi * spb + si)
    return pl.pallas_call(
        functools.partial(_peer_kernel, rows_per_chunk=rpc),
        grid=(b, spb, ne // ec),
        in_specs=[pl.BlockSpec((d, tt), lambda bi, si, j: (0, bi * spb + si)),
                  pl.BlockSpec((ec, d), lambda bi, si, j: (j, 0)),
                  pl.BlockSpec((d, ec), lambda bi, si, j: (0, j)),
                  pl.BlockSpec((PEER_HEADS, PEER_N_KEYS, tt), tok),
                  pl.BlockSpec((PEER_HEADS, PEER_N_KEYS, tt), tok),
                  pl.BlockSpec((PEER_HEADS, rpc, tt), lambda bi, si, j: (0, j, bi * spb + si)),
                  pl.BlockSpec((PEER_HEADS, rpc, tt), lambda bi, si, j: (0, j, bi * spb + si)),
                  pl.BlockSpec((1, tt, d), lambda bi, si, j: (bi, si, 0)),
                  pl.BlockSpec((1, N_MOD, d), lambda bi, si, j: (bi, 0, 0)),
                  pl.BlockSpec((1, d), lambda bi, si, j: (0, 0))],
        out_specs=pl.BlockSpec((1, tt, d), lambda bi, si, j: (bi, si, 0)),
        out_shape=jax.ShapeDtypeStruct((b, s, d), F32),
        scratch_shapes=[pltpu.VMEM((d, tt), F32), pltpu.VMEM((ec, tt), F32), pltpu.VMEM((ec, tt), BF16)],
        compiler_params=_cparams("arbitrary", "arbitrary", "arbitrary"),
        name="peer",
    )(h2t, down, up_t, r2, e2, l1, e1, x1, mod3, g_final)


def _pad_heads(w):
    d = w.shape[0]
    w = w.reshape(d, N_HEADS, HEAD_DIM)
    return jnp.pad(w, ((0, 0), (0, 0), (0, LANES - HEAD_DIM))).reshape(d, HP)


def _bias_constants():
    pq = np.zeros((3 * LANES, HP), np.float32)
    pk = np.zeros((3 * LANES, HP), np.float32)
    cq = np.zeros((1, HP), np.float32)
    ck = np.zeros((1, HP), np.float32)
    cv = np.zeros((1, HP), np.float32)
    for h in range(N_HEADS):
        for part in range(3):
            pq[part * LANES + h, h * LANES + LANE_F_Q + part] = 1.0
            pk[part * LANES + h, h * LANES + LANE_ONE_Q + part] = -1.0
            cq[0, h * LANES + LANE_ONE_Q + part] = 1.0
            ck[0, h * LANES + LANE_F_Q + part] = 1.0
        cv[0, h * LANES + LANE_ONE_V] = 1.0
    return (jnp.asarray(pq, BF16), jnp.asarray(pk, BF16), jnp.asarray(cq), jnp.asarray(ck), jnp.asarray(cv))


def _tile(n, pref):
    return pref if n % pref == 0 else n


def kernel(x, c, w_ada, b_ada, g_norm_mix, w_in, b_forget, g_out_fox, g_out_sb, w_out, g_norm_ffn,
           w_query, sub_keys, expert_down, expert_up, g_final):
    b, s, d = x.shape
    assert w_ada.shape[0] == 1, "single-layer block: the final RMSNorm is fused into the PEER kernel"
    fw = N_HEADS * HEAD_DIM
    scale = HEAD_DIM ** -0.5
    pq, pk, cq, ck, cv = _bias_constants()
    tm = _tile(s, 512)
    c_pad = jnp.pad(c, ((0, -b % 8), (0, 0)))

    for l in range(1):
        mod = _ada(c_pad, w_ada[l], b_ada[l][None, :])[:b]
        mod3 = mod.reshape(b, N_MOD, d)

        wi = w_in[l]
        o1 = 3 * fw
        o2 = o1 + N_HEADS
        w_all = jnp.concatenate([
            _pad_heads(wi[:, 0:fw] * scale), _pad_heads(wi[:, fw:2 * fw]), _pad_heads(wi[:, 2 * fw:o1]),
            _pad_heads(wi[:, o2:o2 + fw] * scale), _pad_heads(wi[:, o2 + fw:o2 + 2 * fw]),
            _pad_heads(wi[:, o2 + 2 * fw:o2 + 3 * fw])], axis=1).astype(BF16)
        wf = jnp.pad(wi[:, o1:o2], ((0, 0), (0, LANES - N_HEADS)))
        wf_hi = wf.astype(BF16)
        wf_lo = (wf - wf_hi.astype(F32)).astype(BF16)
        b_f = jnp.pad(b_forget[l], (0, LANES - N_HEADS))[None, :]

        fq, fk, fv, sq, sk, sv = _inproj(x, mod3, g_norm_mix[l][None, :], w_all, wf_hi, wf_lo, b_f,
                                         pq, pk, cq, ck, cv, tm)

        g_fox = jnp.pad(g_out_fox[l], (0, LANES - HEAD_DIM))[None, :]
        g_sb = jnp.pad(g_out_sb[l], (0, LANES - HEAD_DIM))[None, :]
        fo = _fox(fq, fk, fv, g_fox, _tile(s, 512), _tile(s, 512))
        so = _sb(sq, sk, sv, g_sb, _tile(s, 512), _tile(s, 256))

        wo = w_out[l]
        w_out_p = jnp.pad(wo.reshape(2 * N_HEADS, HEAD_DIM, d),
                          ((0, 0), (0, LANES - HEAD_DIM), (0, 0))).reshape(2 * HP, d).astype(BF16)
        wc = _wc(sub_keys[l].reshape(PEER_HEADS * 2, PEER_N_KEYS, LANES), w_query[l])
        wc_hi = wc.astype(BF16)
        wc_lo = (wc - wc_hi.astype(F32)).astype(BF16)
        x1, h2t, scores_t = _outproj(x, fo, so, mod3, g_norm_ffn[l][None, :], w_out_p, wc_hi, wc_lo, tm)

        r2, e2, l1, e1 = _topk(scores_t, _tile(b * s, 512))
        x = _peer(h2t, expert_down[l].astype(BF16), expert_up[l].T.astype(BF16), r2, e2, l1, e1,
                  x1, mod3, g_final[None, :], _tile(s, 512), 1024)
    return x
```

```python
import functools

import numpy as np
import jax
import jax.numpy as jnp
from jax import lax
from jax.experimental import pallas as pl
from jax.experimental.pallas import tpu as pltpu

F32 = jnp.float32
BF16 = jnp.bfloat16

HEAD_DIM = 64
N_HEADS = 8
PEER_HEADS = 8
PEER_N_KEYS = 128
PEER_TOPK = 16
TOPK_SHIFT = PEER_TOPK.bit_length() - 1
N_MOD = 6
EPS = 1e-6

LANES = 128
VMEM_LIMIT = 56 * 1024 * 1024

HP = N_HEADS * LANES
LANE_ONE_Q = HEAD_DIM
LANE_F_Q = HEAD_DIM + 3
LANE_ONE_V = HEAD_DIM
NEG_INF = float("-inf")
EXP_UNDERFLOW = -106.0


def _cparams(*sem):
    return pltpu.CompilerParams(dimension_semantics=sem, vmem_limit_bytes=VMEM_LIMIT)


def _dot(a, b):
    return jnp.dot(a, b, preferred_element_type=F32)


def _dot_nt(a, b):
    return lax.dot_general(a, b, (((1,), (1,)), ((), ())), preferred_element_type=F32)


def _split2(x):
    hi = x.astype(BF16)
    lo = (x - hi.astype(F32)).astype(BF16)
    return hi, lo


def _split3(x):
    hi = x.astype(BF16)
    r = x - hi.astype(F32)
    mid = r.astype(BF16)
    lo = (r - mid.astype(F32)).astype(BF16)
    return hi, mid, lo


def _softplus(z):
    return jnp.maximum(z, 0.0) + jnp.log(1.0 + jnp.exp(-jnp.abs(z)))


def _ada_kernel(c_ref, w_ref, b_ref, o_ref):
    c = c_ref[...]
    a = c / (1.0 + jnp.exp(-c))
    ah, al = _split2(a)
    wh, wl = _split2(w_ref[...])
    o_ref[...] = _dot(ah, wh) + _dot(ah, wl) + _dot(al, wh) + b_ref[...]


def _ada(c_pad, w_ada, b_ada):
    rows, d = c_pad.shape
    n = w_ada.shape[1]
    bn = 1024
    return pl.pallas_call(
        _ada_kernel,
        grid=(n // bn,),
        in_specs=[pl.BlockSpec((rows, d), lambda j: (0, 0)),
                  pl.BlockSpec((d, bn), lambda j: (0, j)),
                  pl.BlockSpec((1, bn), lambda j: (0, j))],
        out_specs=pl.BlockSpec((rows, bn), lambda j: (0, j)),
        out_shape=jax.ShapeDtypeStruct((rows, n), F32),
        compiler_params=_cparams("arbitrary"),
        name="ada",
    )(c_pad, w_ada, b_ada)


def _wc_kernel(sk_ref, wq_ref, o_ref):
    sh, sl = _split2(sk_ref[0])
    wh, wl = _split2(wq_ref[...])
    o_ref[...] = _dot_nt(sh, wh) + _dot_nt(sh, wl) + _dot_nt(sl, wh)


def _wc(sub_keys, w_query):
    d = w_query.shape[0]
    nhp = sub_keys.shape[0]
    return pl.pallas_call(
        _wc_kernel,
        grid=(nhp,),
        in_specs=[pl.BlockSpec((1, PEER_N_KEYS, LANES), lambda i: (i, 0, 0)),
                  pl.BlockSpec((d, LANES), lambda i: (0, i))],
        out_specs=pl.BlockSpec((PEER_N_KEYS, d), lambda i: (i, 0)),
        out_shape=jax.ShapeDtypeStruct((nhp * PEER_N_KEYS, d), F32),
        compiler_params=_cparams("arbitrary"),
        name="wc",
    )(sub_keys, w_query)


def _inproj_kernel(x_ref, mod_ref, g_ref, w_ref, wfh_ref, wfl_ref, bf_ref, pq_ref, pk_ref,
                   cq_ref, ck_ref, cv_ref,
                   fq_ref, fk_ref, fv_ref, sq_ref, sk_ref, sv_ref, carry_ref):
    tm = x_ref.shape[1]

    @pl.when(pl.program_id(1) == 0)
    def _():
        carry_ref[...] = jnp.zeros_like(carry_ref)

    x = x_ref[0]
    inv = lax.rsqrt(jnp.mean(x * x, axis=-1, keepdims=True) + EPS)
    sh1 = mod_ref[0, 0:1, :]
    sc1 = mod_ref[0, 1:2, :]
    h = (x * inv) * g_ref[...] * (1.0 + sc1) + sh1
    hh, hl = _split2(h)

    fl = _dot(hh, wfh_ref[...]) + _dot(hh, wfl_ref[...]) + _dot(hl, wfh_ref[...]) + bf_ref[...]
    lf = jnp.minimum(fl, 0.0) - jnp.log(1.0 + jnp.exp(-jnp.abs(fl)))
    row = lax.broadcasted_iota(jnp.int32, (tm, tm), 0)
    col = lax.broadcasted_iota(jnp.int32, (tm, tm), 1)
    tri = jnp.where(col <= row, 1.0, 0.0).astype(BF16)
    l0, l1, l2 = _split3(lf)
    fcum = _dot(tri, l0) + _dot(tri, l1) + _dot(tri, l2) + carry_ref[...]
    carry_ref[...] = fcum[tm - 1:tm, :]
    f0, f1, f2 = _split3(fcum)
    fcat = jnp.concatenate([f0, f1, f2], axis=1)

    fq_ref[0] = (_dot(hh, w_ref[:, 0 * HP:1 * HP]) + _dot(fcat, pq_ref[...]) + cq_ref[...]).astype(BF16)
    fk_ref[0] = (_dot(hh, w_ref[:, 1 * HP:2 * HP]) + _dot(fcat, pk_ref[...]) + ck_ref[...]).astype(BF16)
    fv_ref[0] = (_dot(hh, w_ref[:, 2 * HP:3 * HP]) + cv_ref[...]).astype(BF16)
    sq_ref[0] = _dot(hh, w_ref[:, 3 * HP:4 * HP]).astype(BF16)
    sk_ref[0] = _dot(hh, w_ref[:, 4 * HP:5 * HP]).astype(BF16)
    sv_ref[0] = _dot(hh, w_ref[:, 5 * HP:6 * HP]).astype(BF16)


def _inproj(x, mod3, g_mix, w_all, wf_hi, wf_lo, b_f, pq, pk, cq, ck, cv, tm):
    b, s, d = x.shape
    const = lambda shape: pl.BlockSpec(shape, lambda bi, si: (0,) * len(shape),
                                       pipeline_mode=pl.Buffered(1))
    out_spec = pl.BlockSpec((1, tm, HP), lambda bi, si: (bi, si, 0))
    out_sds = jax.ShapeDtypeStruct((b, s, HP), BF16)
    return pl.pallas_call(
        _inproj_kernel,
        grid=(b, s // tm),
        in_specs=[pl.BlockSpec((1, tm, d), lambda bi, si: (bi, si, 0)),
                  pl.BlockSpec((1, N_MOD, d), lambda bi, si: (bi, 0, 0)),
                  const((1, d)),
                  const((d, 6 * HP)),
                  const((d, LANES)), const((d, LANES)), const((1, LANES)),
                  const((3 * LANES, HP)), const((3 * LANES, HP)),
                  const((1, HP)), const((1, HP)), const((1, HP))],
        out_specs=[out_spec] * 6,
        out_shape=[out_sds] * 6,
        scratch_shapes=[pltpu.VMEM((1, LANES), F32)],
        compiler_params=_cparams("arbitrary", "arbitrary"),
        name="inproj",
    )(x, mod3, g_mix, w_all, wf_hi, wf_lo, b_f, pq, pk, cq, ck, cv)


def _head_norm(o, g):
    lane = lax.broadcasted_iota(jnp.int32, o.shape, 1)
    o = jnp.where(lane < HEAD_DIM, o, 0.0)
    ms = jnp.sum(o * o, axis=-1, keepdims=True) * (1.0 / HEAD_DIM)
    return o * lax.rsqrt(ms + EPS) * g


def _fox_kernel(q_ref, k_ref, v_ref, g_ref, o_ref, m_ref, acc_ref, *, tq, tk):
    s_len = q_ref.shape[1]
    n_diag = tq // tk
    lane1 = lax.broadcasted_iota(jnp.int32, (1, LANES), 1)
    head_lanes = lane1 < HEAD_DIM
    row = lax.broadcasted_iota(jnp.int32, (tq, tk), 0)
    col = lax.broadcasted_iota(jnp.int32, (tq, tk), 1)

    def knorm(i, mx):
        kb = k_ref[0, pl.ds(pl.multiple_of(i * tk, tk), tk), :].astype(F32)
        kb = jnp.where(head_lanes, kb, 0.0)
        n2 = jnp.sum(kb * kb, axis=-1, keepdims=True)
        return jnp.maximum(mx, jnp.max(n2, axis=0, keepdims=True))

    kmax = jnp.sqrt(lax.fori_loop(0, s_len // tk, knorm, jnp.zeros((1, 1), F32)))

    def q_tile(qi, carry):
        q0 = pl.multiple_of(qi * tq, tq)
        q = q_ref[0, pl.ds(q0, tq), :]
        qf = q.astype(F32)
        qn = jnp.sqrt(jnp.sum(jnp.where(head_lanes, qf * qf, 0.0), axis=-1, keepdims=True))
        f_t = jnp.sum(jnp.where((lane1 >= LANE_F_Q) & (lane1 < LANE_F_Q + 3), qf, 0.0),
                      axis=-1, keepdims=True)
        reach = qn * kmax + f_t
        m_ref[...] = jnp.full_like(m_ref, NEG_INF)
        acc_ref[...] = jnp.zeros_like(acc_ref)

        def step(start, mask):
            k = k_ref[0, pl.ds(start, tk), :]
            v = v_ref[0, pl.ds(start, tk), :]
            s = _dot_nt(q, k)
            if mask is not None:
                s = jnp.where(mask, s, NEG_INF)
            m_prev = m_ref[...]
            m_new = jnp.maximum(m_prev, jnp.max(s, axis=-1, keepdims=True))
            p = jnp.exp(s - m_new)
            acc_ref[...] = jnp.exp(m_prev - m_new) * acc_ref[...] + _dot(p.astype(BF16), v)
            m_ref[...] = m_new

        def block_matters(kb):
            last = k_ref[0, pl.ds(pl.multiple_of(kb * tk + tk - 16, 16), 16), :][15:16, :].astype(F32)
            neg_f_last = jnp.sum(jnp.where((lane1 >= LANE_ONE_Q) & (lane1 < LANE_ONE_Q + 3), last, 0.0),
                                 axis=-1, keepdims=True)
            return jnp.max(reach + neg_f_last - m_ref[...]) > EXP_UNDERFLOW

        for d in range(n_diag):
            step(pl.multiple_of(q0 + d * tk, tk), col + d * tk <= row)

        def cond(st):
            return st[1]

        def body(st):
            kb = st[0]
            step(pl.multiple_of(kb * tk, tk), None)
            nxt = kb - 1
            return nxt, (nxt >= 0) & block_matters(jnp.maximum(nxt, 0))

        kb0 = qi * n_diag - 1
        lax.while_loop(cond, body, (kb0, (kb0 >= 0) & block_matters(jnp.maximum(kb0, 0))))

        acc = acc_ref[...]
        lane = lax.broadcasted_iota(jnp.int32, acc.shape, 1)
        l = jnp.sum(jnp.where(lane == LANE_ONE_V, acc, 0.0), axis=-1, keepdims=True)
        o_ref[0, pl.ds(q0, tq), :] = _head_norm(acc / l, g_ref[...]).astype(o_ref.dtype)
        return carry

    lax.fori_loop(0, s_len // tq, q_tile, 0)


def _attn_call(body, name, q, k, v, g, tq):
    b, s, _ = q.shape
    spec = pl.BlockSpec((1, s, LANES), lambda bi, hi: (bi, 0, hi))
    return pl.pallas_call(
        body,
        grid=(b, N_HEADS),
        in_specs=[spec, spec, spec, pl.BlockSpec((1, LANES), lambda bi, hi: (0, 0))],
        out_specs=spec,
        out_shape=jax.ShapeDtypeStruct((b, s, HP), BF16),
        scratch_shapes=[pltpu.VMEM((tq, 1), F32), pltpu.VMEM((tq, LANES), F32)],
        compiler_params=_cparams("arbitrary", "arbitrary"),
        name=name,
    )(q, k, v, g)


def _fox(q, k, v, g, tq, tk):
    return _attn_call(functools.partial(_fox_kernel, tq=tq, tk=tk), "fox", q, k, v, g, tq)


def _sb_kernel(q_ref, k_ref, v_ref, g_ref, o_ref, c_ref, acc_ref, *, tq, tk):
    s_len = q_ref.shape[1]
    n_diag = tq // tk
    jrow = lax.broadcasted_iota(jnp.int32, (tk, tk), 0)
    scol = lax.broadcasted_iota(jnp.int32, (tk, tk), 1)
    later = jnp.where(jrow > scol, 1.0, 0.0).astype(BF16)
    row = lax.broadcasted_iota(jnp.int32, (tq, tk), 0)
    col = lax.broadcasted_iota(jnp.int32, (tq, tk), 1)

    def q_tile(qi, carry):
        q0 = pl.multiple_of(qi * tq, tq)
        q = q_ref[0, pl.ds(q0, tq), :]
        c_ref[...] = jnp.zeros_like(c_ref)
        acc_ref[...] = jnp.zeros_like(acc_ref)

        def step(start, mask):
            k = k_ref[0, pl.ds(start, tk), :]
            v = v_ref[0, pl.ds(start, tk), :]
            z = _dot_nt(q, k)
            sp = _softplus(z)
            spm = sp if mask is None else jnp.where(mask, sp, 0.0)
            hi, lo = _split2(spm)
            rest = _dot(hi, later) + _dot(lo, later) + c_ref[...]
            a = jnp.exp(z - sp - rest)
            if mask is not None:
                a = jnp.where(mask, a, 0.0)
            acc_ref[...] += _dot(a.astype(BF16), v)
            c_ref[...] += jnp.sum(spm, axis=-1, keepdims=True)

        def more_matters():
            return jnp.min(c_ref[...]) < -EXP_UNDERFLOW

        for d in reversed(range(n_diag)):
            step(pl.multiple_of(q0 + d * tk, tk), col + d * tk < row)

        def cond(st):
            return st[1]

        def body(st):
            kb = st[0]
            step(pl.multiple_of(kb * tk, tk), None)
            nxt = kb - 1
            return nxt, (nxt >= 0) & more_matters()

        kb0 = qi * n_diag - 1
        lax.while_loop(cond, body, (kb0, (kb0 >= 0) & more_matters()))
        o_ref[0, pl.ds(q0, tq), :] = _head_norm(acc_ref[...], g_ref[...]).astype(o_ref.dtype)
        return carry

    lax.fori_loop(0, s_len // tq, q_tile, 0)


def _sb(q, k, v, g, tq, tk):
    return _attn_call(functools.partial(_sb_kernel, tq=tq, tk=tk), "sb", q, k, v, g, tq)


def _outproj_kernel(x_ref, fo_ref, so_ref, mod_ref, g_ref, wo_ref, wch_ref, wcl_ref,
                    x1_ref, h2t_ref, sc_ref):
    gt1 = mod_ref[0, 2:3, :]
    sh2 = mod_ref[0, 3:4, :]
    sc2 = mod_ref[0, 4:5, :]
    mixed = _dot(fo_ref[0], wo_ref[0:HP, :]) + _dot(so_ref[0], wo_ref[HP:2 * HP, :])
    x1 = x_ref[0] + gt1 * mixed
    x1_ref[0] = x1
    inv = lax.rsqrt(jnp.mean(x1 * x1, axis=-1, keepdims=True) + EPS)
    h2 = (x1 * inv) * g_ref[...] * (1.0 + sc2) + sh2
    h2t = h2.T
    th, tl = _split2(h2t)
    h2t_ref[...] = th
    sc_ref[...] = _dot(wch_ref[...], th) + _dot(wch_ref[...], tl) + _dot(wcl_ref[...], th)


def _outproj(x, fo, so, mod3, g_ffn, w_out_p, wc_hi, wc_lo, tm):
    b, s, d = x.shape
    t = b * s
    nsc = wc_hi.shape[0]
    const = lambda shape: pl.BlockSpec(shape, lambda bi, si: (0,) * len(shape),
                                       pipeline_mode=pl.Buffered(1))
    spb = s // tm
    return pl.pallas_call(
        _outproj_kernel,
        grid=(b, spb),
        in_specs=[pl.BlockSpec((1, tm, d), lambda bi, si: (bi, si, 0)),
                  pl.BlockSpec((1, tm, HP), lambda bi, si: (bi, si, 0)),
                  pl.BlockSpec((1, tm, HP), lambda bi, si: (bi, si, 0)),
                  pl.BlockSpec((1, N_MOD, d), lambda bi, si: (bi, 0, 0)),
                  const((1, d)),
                  const((2 * HP, d)),
                  const((nsc, d)), const((nsc, d))],
        out_specs=[pl.BlockSpec((1, tm, d), lambda bi, si: (bi, si, 0)),
                   pl.BlockSpec((d, tm), lambda bi, si: (0, bi * spb + si)),
                   pl.BlockSpec((nsc, tm), lambda bi, si: (0, bi * spb + si))],
        out_shape=[jax.ShapeDtypeStruct((b, s, d), F32),
                   jax.ShapeDtypeStruct((d, t), BF16),
                   jax.ShapeDtypeStruct((nsc, t), F32)],
        compiler_params=_cparams("arbitrary", "arbitrary"),
        name="outproj",
    )(x, fo, so, mod3, g_ffn, w_out_p, wc_hi, wc_lo)


def _top16(s):
    n, t = s.shape
    idx = lax.broadcasted_iota(jnp.int32, (n, t), 0)
    krow = lax.broadcasted_iota(jnp.int32, (PEER_TOPK, t), 0)
    rank = jnp.full((n, t), float(PEER_TOPK), F32)
    vals = jnp.zeros((PEER_TOPK, t), F32)
    for k in range(PEER_TOPK):
        m = jnp.max(s, axis=0, keepdims=True)
        first = jnp.min(jnp.where(s == m, idx, n), axis=0, keepdims=True)
        hit = idx == first
        rank = jnp.where(hit, float(k), rank)
        s = jnp.where(hit, NEG_INF, s)
        vals = jnp.where(krow == k, m, vals)
    return vals, rank


def _topk_kernel(sc_ref, r2_ref, e2_ref, l1_ref, e1_ref):
    n = PEER_N_KEYS
    tl = sc_ref.shape[1]
    ngrp = tl // LANES

    def body(it, carry):
        hd = lax.div(it, ngrp)
        gi = lax.rem(it, ngrp)
        lanes = pl.ds(pl.multiple_of(gi * LANES, LANES), LANES)
        s1 = sc_ref[pl.ds(pl.multiple_of(hd * 2 * n, n), n), lanes]
        s2 = sc_ref[pl.ds(pl.multiple_of(hd * 2 * n + n, n), n), lanes]
        v1, r1 = _top16(s1)
        v2, r2 = _top16(s2)
        cand = jnp.concatenate([v1[a:a + 1, :] + v2 for a in range(PEER_TOPK)], axis=0)
        pos = lax.broadcasted_iota(jnp.int32, cand.shape, 0)
        arow = lax.broadcasted_iota(jnp.int32, (PEER_TOPK, LANES), 0)
        cnt = jnp.zeros((PEER_TOPK, LANES), F32)
        zsum = jnp.zeros((1, LANES), F32)
        c0 = cand[0:1, :]
        for k in range(PEER_TOPK):
            m = jnp.max(cand, axis=0, keepdims=True)
            first = jnp.min(jnp.where(cand == m, pos, PEER_TOPK * PEER_TOPK), axis=0, keepdims=True)
            cand = jnp.where(pos == first, NEG_INF, cand)
            cnt = cnt + jnp.where(arow == jnp.right_shift(first, TOPK_SHIFT), 1.0, 0.0)
            zsum = zsum + jnp.exp(m - c0)
        lrow = jnp.zeros((n, LANES), F32)
        for a in range(PEER_TOPK):
            lrow = jnp.where(r1 == float(a), cnt[a:a + 1, :], lrow)
        r2_ref[hd, :, lanes] = r2
        e2_ref[hd, :, lanes] = jnp.exp(s2 - v2[0:1, :])
        l1_ref[hd, :, lanes] = lrow
        e1_ref[hd, :, lanes] = jnp.exp(s1 - v1[0:1, :]) / zsum
        return carry

    lax.fori_loop(0, PEER_HEADS * ngrp, body, 0)


def _topk(scores_t, tl):
    nsc, t = scores_t.shape
    spec = pl.BlockSpec((PEER_HEADS, PEER_N_KEYS, tl), lambda i: (0, 0, i))
    sds = jax.ShapeDtypeStruct((PEER_HEADS, PEER_N_KEYS, t), F32)
    return pl.pallas_call(
        _topk_kernel,
        grid=(t // tl,),
        in_specs=[pl.BlockSpec((nsc, tl), lambda i: (0, i))],
        out_specs=[spec] * 4,
        out_shape=[sds] * 4,
        compiler_params=_cparams("arbitrary"),
        name="topk",
    )(scores_t)


def _gelu(x):
    return 0.5 * x * (1.0 + lax.erf(x * (2.0 ** -0.5)))


def _peer_kernel(h2t_ref, dn_ref, up_ref, r2_ref, e2_ref, l1_ref, e1_ref, x1_ref, mod_ref, gf_ref,
                 o_ref, acc_ref, st_ref, pw_ref, *, rows_per_chunk):
    j = pl.program_id(2)
    n = PEER_N_KEYS

    @pl.when(j == 0)
    def _():
        acc_ref[...] = jnp.zeros_like(acc_ref)

    st_ref[...] = _dot(dn_ref[...], h2t_ref[...])
    for i1 in range(rows_per_chunk):
        rows = slice(i1 * n, (i1 + 1) * n)
        for g in range(h2t_ref.shape[1] // LANES):
            lanes = slice(g * LANES, (g + 1) * LANES)
            w = None
            for hd in range(PEER_HEADS):
                sel = r2_ref[hd, :, lanes] < l1_ref[hd, i1:i1 + 1, lanes]
                wh = jnp.where(sel, e2_ref[hd, :, lanes], 0.0) * e1_ref[hd, i1:i1 + 1, lanes]
                w = wh if w is None else w + wh
            pw_ref[rows, lanes] = (_gelu(st_ref[rows, lanes]) * w).astype(BF16)
    acc_ref[...] += _dot(up_ref[...], pw_ref[...])

    @pl.when(j == pl.num_programs(2) - 1)
    def _():
        gt2 = mod_ref[0, 5:6, :]
        x2 = x1_ref[0] + gt2 * acc_ref[...].T
        inv = lax.rsqrt(jnp.mean(x2 * x2, axis=-1, keepdims=True) + EPS)
        o_ref[0] = (x2 * inv) * gf_ref[...]


def _peer(h2t, down, up_t, r2, e2, l1, e1, x1, mod3, g_final, tt, ec):
    b, s, d = x1.shape
    ne = down.shape[0]
    rpc = ec // PEER_N_KEYS
    spb = s // tt
    tok = lambda bi, si, j: (0, 0, bi * spb + si)
    return pl.pallas_call(
        functools.partial(_peer_kernel, rows_per_chunk=rpc),
        grid=(b, spb, ne // ec),
        in_specs=[pl.BlockSpec((d, tt), lambda bi, si, j: (0, bi * spb + si)),
                  pl.BlockSpec((ec, d), lambda bi, si, j: (j, 0)),
                  pl.BlockSpec((d, ec), lambda bi, si, j: (0, j)),
                  pl.BlockSpec((PEER_HEADS, PEER_N_KEYS, tt), tok),
                  pl.BlockSpec((PEER_HEADS, PEER_N_KEYS, tt), tok),
                  pl.BlockSpec((PEER_HEADS, rpc, tt), lambda bi, si, j: (0, j, bi * spb + si)),
                  pl.BlockSpec((PEER_HEADS, rpc, tt), lambda bi, si, j: (0, j, bi * spb + si)),
                  pl.BlockSpec((1, tt, d), lambda bi, si, j: (bi, si, 0)),
                  pl.BlockSpec((1, N_MOD, d), lambda bi, si, j: (bi, 0, 0)),
                  pl.BlockSpec((1, d), lambda bi, si, j: (0, 0))],
        out_specs=pl.BlockSpec((1, tt, d), lambda bi, si, j: (bi, si, 0)),
        out_shape=jax.ShapeDtypeStruct((b, s, d), F32),
        scratch_shapes=[pltpu.VMEM((d, tt), F32), pltpu.VMEM((ec, tt), F32), pltpu.VMEM((ec, tt), BF16)],
        compiler_params=_cparams("arbitrary", "arbitrary", "arbitrary"),
        name="peer",
    )(h2t, down, up_t, r2, e2, l1, e1, x1, mod3, g_final)


def _pad_heads(w):
    d = w.shape[0]
    w = w.reshape(d, N_HEADS, HEAD_DIM)
    return jnp.pad(w, ((0, 0), (0, 0), (0, LANES - HEAD_DIM))).reshape(d, HP)


def _bias_constants():
    pq = np.zeros((3 * LANES, HP), np.float32)
    pk = np.zeros((3 * LANES, HP), np.float32)
    cq = np.zeros((1, HP), np.float32)
    ck = np.zeros((1, HP), np.float32)
    cv = np.zeros((1, HP), np.float32)
    for h in range(N_HEADS):
        for part in range(3):
            pq[part * LANES + h, h * LANES + LANE_F_Q + part] = 1.0
            pk[part * LANES + h, h * LANES + LANE_ONE_Q + part] = -1.0
            cq[0, h * LANES + LANE_ONE_Q + part] = 1.0
            ck[0, h * LANES + LANE_F_Q + part] = 1.0
        cv[0, h * LANES + LANE_ONE_V] = 1.0
    return (jnp.asarray(pq, BF16), jnp.asarray(pk, BF16), jnp.asarray(cq), jnp.asarray(ck), jnp.asarray(cv))


def _tile(n, pref):
    return pref if n % pref == 0 else n


def kernel(x, c, w_ada, b_ada, g_norm_mix, w_in, b_forget, g_out_fox, g_out_sb, w_out, g_norm_ffn,
           w_query, sub_keys, expert_down, expert_up, g_final):
    b, s, d = x.shape
    assert w_ada.shape[0] == 1, "single-layer block: the final RMSNorm is fused into the PEER kernel"
    fw = N_HEADS * HEAD_DIM
    scale = HEAD_DIM ** -0.5
    pq, pk, cq, ck, cv = _bias_constants()
    tm = _tile(s, 512)
    c_pad = jnp.pad(c, ((0, -b % 8), (0, 0)))

    for l in range(1):
        mod = _ada(c_pad, w_ada[l], b_ada[l][None, :])[:b]
        mod3 = mod.reshape(b, N_MOD, d)

        wi = w_in[l]
        o1 = 3 * fw
        o2 = o1 + N_HEADS
        w_all = jnp.concatenate([
            _pad_heads(wi[:, 0:fw] * scale), _pad_heads(wi[:, fw:2 * fw]), _pad_heads(wi[:, 2 * fw:o1]),
            _pad_heads(wi[:, o2:o2 + fw] * scale), _pad_heads(wi[:, o2 + fw:o2 + 2 * fw]),
            _pad_heads(wi[:, o2 + 2 * fw:o2 + 3 * fw])], axis=1).astype(BF16)
        wf = jnp.pad(wi[:, o1:o2], ((0, 0), (0, LANES - N_HEADS)))
        wf_hi = wf.astype(BF16)
        wf_lo = (wf - wf_hi.astype(F32)).astype(BF16)
        b_f = jnp.pad(b_forget[l], (0, LANES - N_HEADS))[None, :]

        fq, fk, fv, sq, sk, sv = _inproj(x, mod3, g_norm_mix[l][None, :], w_all, wf_hi, wf_lo, b_f,
                                         pq, pk, cq, ck, cv, tm)

        g_fox = jnp.pad(g_out_fox[l], (0, LANES - HEAD_DIM))[None, :]
        g_sb = jnp.pad(g_out_sb[l], (0, LANES - HEAD_DIM))[None, :]
        fo = _fox(fq, fk, fv, g_fox, _tile(s, 256), _tile(s, 256))
        so = _sb(sq, sk, sv, g_sb, _tile(s, 256), _tile(s, 256))

        wo = w_out[l]
        w_out_p = jnp.pad(wo.reshape(2 * N_HEADS, HEAD_DIM, d),
                          ((0, 0), (0, LANES - HEAD_DIM), (0, 0))).reshape(2 * HP, d).astype(BF16)
        wc = _wc(sub_keys[l].reshape(PEER_HEADS * 2, PEER_N_KEYS, LANES), w_query[l])
        wc_hi = wc.astype(BF16)
        wc_lo = (wc - wc_hi.astype(F32)).astype(BF16)
        x1, h2t, scores_t = _outproj(x, fo, so, mod3, g_norm_ffn[l][None, :], w_out_p, wc_hi, wc_lo, tm)

        r2, e2, l1, e1 = _topk(scores_t, _tile(b * s, 512))
        x = _peer(h2t, expert_down[l].astype(BF16), expert_up[l].T.astype(BF16), r2, e2, l1, e1,
                  x1, mod3, g_final[None, :], _tile(s, 512), 1024)
    return x
```

```python
import functools

import numpy as np
import jax
import jax.numpy as jnp
from jax import lax
from jax.experimental import pallas as pl
from jax.experimental.pallas import tpu as pltpu

F32 = jnp.float32
BF16 = jnp.bfloat16

HEAD_DIM = 64
N_HEADS = 8
PEER_HEADS = 8
PEER_N_KEYS = 128
PEER_TOPK = 16
N_MOD = 6
EPS = 1e-6

LANES = 128
VMEM_LIMIT = 56 * 1024 * 1024

HP = N_HEADS * LANES
LANE_ONE_Q = HEAD_DIM
LANE_F_Q = HEAD_DIM + 3
LANE_ONE_V = HEAD_DIM
NEG_INF = float("-inf")
EXP_UNDERFLOW = -106.0


def _cparams(*sem):
    return pltpu.CompilerParams(dimension_semantics=sem, vmem_limit_bytes=VMEM_LIMIT)


def _dot(a, b):
    return jnp.dot(a, b, preferred_element_type=F32)


def _dot_nt(a, b):
    return lax.dot_general(a, b, (((1,), (1,)), ((), ())), preferred_element_type=F32)


def _split2(x):
    hi = x.astype(BF16)
    lo = (x - hi.astype(F32)).astype(BF16)
    return hi, lo


def _split3(x):
    hi = x.astype(BF16)
    r = x - hi.astype(F32)
    mid = r.astype(BF16)
    lo = (r - mid.astype(F32)).astype(BF16)
    return hi, mid, lo


def _softplus(z):
    return jnp.maximum(z, 0.0) + jnp.log(1.0 + jnp.exp(-jnp.abs(z)))


def _ada_kernel(c_ref, w_ref, b_ref, o_ref):
    c = c_ref[...]
    a = c / (1.0 + jnp.exp(-c))
    ah, al = _split2(a)
    wh, wl = _split2(w_ref[...])
    o_ref[...] = _dot(ah, wh) + _dot(ah, wl) + _dot(al, wh) + b_ref[...]


def _ada(c_pad, w_ada, b_ada):
    rows, d = c_pad.shape
    n = w_ada.shape[1]
    bn = 1024
    return pl.pallas_call(
        _ada_kernel,
        grid=(n // bn,),
        in_specs=[pl.BlockSpec((rows, d), lambda j: (0, 0)),
                  pl.BlockSpec((d, bn), lambda j: (0, j)),
                  pl.BlockSpec((1, bn), lambda j: (0, j))],
        out_specs=pl.BlockSpec((rows, bn), lambda j: (0, j)),
        out_shape=jax.ShapeDtypeStruct((rows, n), F32),
        compiler_params=_cparams("arbitrary"),
        name="ada",
    )(c_pad, w_ada, b_ada)


def _wc_kernel(sk_ref, wq_ref, o_ref):
    sh, sl = _split2(sk_ref[0])
    wh, wl = _split2(wq_ref[...])
    o_ref[...] = _dot_nt(sh, wh) + _dot_nt(sh, wl) + _dot_nt(sl, wh)


def _wc(sub_keys, w_query):
    d = w_query.shape[0]
    nhp = sub_keys.shape[0]
    return pl.pallas_call(
        _wc_kernel,
        grid=(nhp,),
        in_specs=[pl.BlockSpec((1, PEER_N_KEYS, LANES), lambda i: (i, 0, 0)),
                  pl.BlockSpec((d, LANES), lambda i: (0, i))],
        out_specs=pl.BlockSpec((PEER_N_KEYS, d), lambda i: (i, 0)),
        out_shape=jax.ShapeDtypeStruct((nhp * PEER_N_KEYS, d), F32),
        compiler_params=_cparams("arbitrary"),
        name="wc",
    )(sub_keys, w_query)


def _inproj_kernel(x_ref, mod_ref, g_ref, w_ref, wfh_ref, wfl_ref, bf_ref, pq_ref, pk_ref,
                   cq_ref, ck_ref, cv_ref,
                   fq_ref, fk_ref, fv_ref, sq_ref, sk_ref, sv_ref, carry_ref):
    tm = x_ref.shape[1]

    @pl.when(pl.program_id(1) == 0)
    def _():
        carry_ref[...] = jnp.zeros_like(carry_ref)

    x = x_ref[0]
    inv = lax.rsqrt(jnp.mean(x * x, axis=-1, keepdims=True) + EPS)
    sh1 = mod_ref[0, 0:1, :]
    sc1 = mod_ref[0, 1:2, :]
    h = (x * inv) * g_ref[...] * (1.0 + sc1) + sh1
    hh, hl = _split2(h)

    fl = _dot(hh, wfh_ref[...]) + _dot(hh, wfl_ref[...]) + _dot(hl, wfh_ref[...]) + bf_ref[...]
    lf = jnp.minimum(fl, 0.0) - jnp.log(1.0 + jnp.exp(-jnp.abs(fl)))
    row = lax.broadcasted_iota(jnp.int32, (tm, tm), 0)
    col = lax.broadcasted_iota(jnp.int32, (tm, tm), 1)
    tri = jnp.where(col <= row, 1.0, 0.0).astype(BF16)
    l0, l1, l2 = _split3(lf)
    fcum = _dot(tri, l0) + _dot(tri, l1) + _dot(tri, l2) + carry_ref[...]
    carry_ref[...] = fcum[tm - 1:tm, :]
    f0, f1, f2 = _split3(fcum)
    fcat = jnp.concatenate([f0, f1, f2], axis=1)

    fq_ref[0] = (_dot(hh, w_ref[:, 0 * HP:1 * HP]) + _dot(fcat, pq_ref[...]) + cq_ref[...]).astype(BF16)
    fk_ref[0] = (_dot(hh, w_ref[:, 1 * HP:2 * HP]) + _dot(fcat, pk_ref[...]) + ck_ref[...]).astype(BF16)
    fv_ref[0] = (_dot(hh, w_ref[:, 2 * HP:3 * HP]) + cv_ref[...]).astype(BF16)
    sq_ref[0] = _dot(hh, w_ref[:, 3 * HP:4 * HP]).astype(BF16)
    sk_ref[0] = _dot(hh, w_ref[:, 4 * HP:5 * HP]).astype(BF16)
    sv_ref[0] = _dot(hh, w_ref[:, 5 * HP:6 * HP]).astype(BF16)


def _inproj(x, mod3, g_mix, w_all, wf_hi, wf_lo, b_f, pq, pk, cq, ck, cv, tm):
    b, s, d = x.shape
    const = lambda shape: pl.BlockSpec(shape, lambda bi, si: (0,) * len(shape),
                                       pipeline_mode=pl.Buffered(1))
    out_spec = pl.BlockSpec((1, tm, HP), lambda bi, si: (bi, si, 0))
    out_sds = jax.ShapeDtypeStruct((b, s, HP), BF16)
    return pl.pallas_call(
        _inproj_kernel,
        grid=(b, s // tm),
        in_specs=[pl.BlockSpec((1, tm, d), lambda bi, si: (bi, si, 0)),
                  pl.BlockSpec((1, N_MOD, d), lambda bi, si: (bi, 0, 0)),
                  const((1, d)),
                  const((d, 6 * HP)),
                  const((d, LANES)), const((d, LANES)), const((1, LANES)),
                  const((3 * LANES, HP)), const((3 * LANES, HP)),
                  const((1, HP)), const((1, HP)), const((1, HP))],
        out_specs=[out_spec] * 6,
        out_shape=[out_sds] * 6,
        scratch_shapes=[pltpu.VMEM((1, LANES), F32)],
        compiler_params=_cparams("arbitrary", "arbitrary"),
        name="inproj",
    )(x, mod3, g_mix, w_all, wf_hi, wf_lo, b_f, pq, pk, cq, ck, cv)


def _head_norm(o, g):
    lane = lax.broadcasted_iota(jnp.int32, o.shape, 1)
    o = jnp.where(lane < HEAD_DIM, o, 0.0)
    ms = jnp.sum(o * o, axis=-1, keepdims=True) * (1.0 / HEAD_DIM)
    return o * lax.rsqrt(ms + EPS) * g


def _fox_kernel(q_ref, k_ref, v_ref, g_ref, o_ref, m_ref, acc_ref, *, tq, tk):
    s_len = q_ref.shape[1]
    n_diag = tq // tk
    lane1 = lax.broadcasted_iota(jnp.int32, (1, LANES), 1)
    head_lanes = lane1 < HEAD_DIM
    row = lax.broadcasted_iota(jnp.int32, (tq, tk), 0)
    col = lax.broadcasted_iota(jnp.int32, (tq, tk), 1)

    def knorm(i, mx):
        kb = k_ref[0, pl.ds(pl.multiple_of(i * tk, tk), tk), :].astype(F32)
        kb = jnp.where(head_lanes, kb, 0.0)
        n2 = jnp.sum(kb * kb, axis=-1, keepdims=True)
        return jnp.maximum(mx, jnp.max(n2, axis=0, keepdims=True))

    kmax = jnp.sqrt(lax.fori_loop(0, s_len // tk, knorm, jnp.zeros((1, 1), F32)))

    def q_tile(qi, carry):
        q0 = pl.multiple_of(qi * tq, tq)
        q = q_ref[0, pl.ds(q0, tq), :]
        qf = q.astype(F32)
        qn = jnp.sqrt(jnp.sum(jnp.where(head_lanes, qf * qf, 0.0), axis=-1, keepdims=True))
        f_t = jnp.sum(jnp.where((lane1 >= LANE_F_Q) & (lane1 < LANE_F_Q + 3), qf, 0.0),
                      axis=-1, keepdims=True)
        reach = qn * kmax + f_t
        m_ref[...] = jnp.full_like(m_ref, NEG_INF)
        acc_ref[...] = jnp.zeros_like(acc_ref)

        def step(start, mask):
            k = k_ref[0, pl.ds(start, tk), :]
            v = v_ref[0, pl.ds(start, tk), :]
            s = _dot_nt(q, k)
            if mask is not None:
                s = jnp.where(mask, s, NEG_INF)
            m_prev = m_ref[...]
            m_new = jnp.maximum(m_prev, jnp.max(s, axis=-1, keepdims=True))
            p = jnp.exp(s - m_new)
            acc_ref[...] = jnp.exp(m_prev - m_new) * acc_ref[...] + _dot(p.astype(BF16), v)
            m_ref[...] = m_new

        def block_matters(kb):
            last = k_ref[0, pl.ds(pl.multiple_of(kb * tk + tk - 16, 16), 16), :][15:16, :].astype(F32)
            neg_f_last = jnp.sum(jnp.where((lane1 >= LANE_ONE_Q) & (lane1 < LANE_ONE_Q + 3), last, 0.0),
                                 axis=-1, keepdims=True)
            return jnp.max(reach + neg_f_last - m_ref[...]) > EXP_UNDERFLOW

        for d in range(n_diag):
            step(pl.multiple_of(q0 + d * tk, tk), col + d * tk <= row)

        def cond(st):
            return st[1]

        def body(st):
            kb = st[0]
            step(pl.multiple_of(kb * tk, tk), None)
            nxt = kb - 1
            return nxt, (nxt >= 0) & block_matters(jnp.maximum(nxt, 0))

        kb0 = qi * n_diag - 1
        lax.while_loop(cond, body, (kb0, (kb0 >= 0) & block_matters(jnp.maximum(kb0, 0))))

        acc = acc_ref[...]
        lane = lax.broadcasted_iota(jnp.int32, acc.shape, 1)
        l = jnp.sum(jnp.where(lane == LANE_ONE_V, acc, 0.0), axis=-1, keepdims=True)
        o_ref[0, pl.ds(q0, tq), :] = _head_norm(acc / l, g_ref[...]).astype(o_ref.dtype)
        return carry

    lax.fori_loop(0, s_len // tq, q_tile, 0)


def _attn_call(body, name, q, k, v, g, tq):
    b, s, _ = q.shape
    spec = pl.BlockSpec((1, s, LANES), lambda bi, hi: (bi, 0, hi))
    return pl.pallas_call(
        body,
        grid=(b, N_HEADS),
        in_specs=[spec, spec, spec, pl.BlockSpec((1, LANES), lambda bi, hi: (0, 0))],
        out_specs=spec,
        out_shape=jax.ShapeDtypeStruct((b, s, HP), BF16),
        scratch_shapes=[pltpu.VMEM((tq, 1), F32), pltpu.VMEM((tq, LANES), F32)],
        compiler_params=_cparams("arbitrary", "arbitrary"),
        name=name,
    )(q, k, v, g)


def _fox(q, k, v, g, tq, tk):
    return _attn_call(functools.partial(_fox_kernel, tq=tq, tk=tk), "fox", q, k, v, g, tq)


def _sb_kernel(q_ref, k_ref, v_ref, g_ref, o_ref, c_ref, acc_ref, *, tq, tk):
    s_len = q_ref.shape[1]
    n_diag = tq // tk
    jrow = lax.broadcasted_iota(jnp.int32, (tk, tk), 0)
    scol = lax.broadcasted_iota(jnp.int32, (tk, tk), 1)
    later = jnp.where(jrow > scol, 1.0, 0.0).astype(BF16)
    row = lax.broadcasted_iota(jnp.int32, (tq, tk), 0)
    col = lax.broadcasted_iota(jnp.int32, (tq, tk), 1)

    def q_tile(qi, carry):
        q0 = pl.multiple_of(qi * tq, tq)
        q = q_ref[0, pl.ds(q0, tq), :]
        c_ref[...] = jnp.zeros_like(c_ref)
        acc_ref[...] = jnp.zeros_like(acc_ref)

        def step(start, mask):
            k = k_ref[0, pl.ds(start, tk), :]
            v = v_ref[0, pl.ds(start, tk), :]
            z = _dot_nt(q, k)
            sp = _softplus(z)
            spm = sp if mask is None else jnp.where(mask, sp, 0.0)
            hi, lo = _split2(spm)
            rest = _dot(hi, later) + _dot(lo, later) + c_ref[...]
            a = jnp.exp(z - sp - rest)
            if mask is not None:
                a = jnp.where(mask, a, 0.0)
            acc_ref[...] += _dot(a.astype(BF16), v)
            c_ref[...] += jnp.sum(spm, axis=-1, keepdims=True)

        def more_matters():
            return jnp.min(c_ref[...]) < -EXP_UNDERFLOW

        for d in reversed(range(n_diag)):
            step(pl.multiple_of(q0 + d * tk, tk), col + d * tk < row)

        def cond(st):
            return st[1]

        def body(st):
            kb = st[0]
            step(pl.multiple_of(kb * tk, tk), None)
            nxt = kb - 1
            return nxt, (nxt >= 0) & more_matters()

        kb0 = qi * n_diag - 1
        lax.while_loop(cond, body, (kb0, (kb0 >= 0) & more_matters()))
        o_ref[0, pl.ds(q0, tq), :] = _head_norm(acc_ref[...], g_ref[...]).astype(o_ref.dtype)
        return carry

    lax.fori_loop(0, s_len // tq, q_tile, 0)


def _sb(q, k, v, g, tq, tk):
    return _attn_call(functools.partial(_sb_kernel, tq=tq, tk=tk), "sb", q, k, v, g, tq)


def _outproj_kernel(x_ref, fo_ref, so_ref, mod_ref, g_ref, wo_ref, wch_ref, wcl_ref,
                    x1_ref, h2t_ref, sc_ref):
    gt1 = mod_ref[0, 2:3, :]
    sh2 = mod_ref[0, 3:4, :]
    sc2 = mod_ref[0, 4:5, :]
    mixed = _dot(fo_ref[0], wo_ref[0:HP, :]) + _dot(so_ref[0], wo_ref[HP:2 * HP, :])
    x1 = x_ref[0] + gt1 * mixed
    x1_ref[0] = x1
    inv = lax.rsqrt(jnp.mean(x1 * x1, axis=-1, keepdims=True) + EPS)
    h2 = (x1 * inv) * g_ref[...] * (1.0 + sc2) + sh2
    h2t = h2.T
    th, tl = _split2(h2t)
    h2t_ref[...] = th
    sc_ref[...] = _dot(wch_ref[...], th) + _dot(wch_ref[...], tl) + _dot(wcl_ref[...], th)


def _outproj(x, fo, so, mod3, g_ffn, w_out_p, wc_hi, wc_lo, tm):
    b, s, d = x.shape
    t = b * s
    nsc = wc_hi.shape[0]
    const = lambda shape: pl.BlockSpec(shape, lambda bi, si: (0,) * len(shape),
                                       pipeline_mode=pl.Buffered(1))
    spb = s // tm
    return pl.pallas_call(
        _outproj_kernel,
        grid=(b, spb),
        in_specs=[pl.BlockSpec((1, tm, d), lambda bi, si: (bi, si, 0)),
                  pl.BlockSpec((1, tm, HP), lambda bi, si: (bi, si, 0)),
                  pl.BlockSpec((1, tm, HP), lambda bi, si: (bi, si, 0)),
                  pl.BlockSpec((1, N_MOD, d), lambda bi, si: (bi, 0, 0)),
                  const((1, d)),
                  const((2 * HP, d)),
                  const((nsc, d)), const((nsc, d))],
        out_specs=[pl.BlockSpec((1, tm, d), lambda bi, si: (bi, si, 0)),
                   pl.BlockSpec((d, tm), lambda bi, si: (0, bi * spb + si)),
                   pl.BlockSpec((nsc, tm), lambda bi, si: (0, bi * spb + si))],
        out_shape=[jax.ShapeDtypeStruct((b, s, d), F32),
                   jax.ShapeDtypeStruct((d, t), BF16),
                   jax.ShapeDtypeStruct((nsc, t), F32)],
        compiler_params=_cparams("arbitrary", "arbitrary"),
        name="outproj",
    )(x, fo, so, mod3, g_ffn, w_out_p, wc_hi, wc_lo)


def _top16(s, exact):
    n, t = s.shape
    idx = lax.broadcasted_iota(jnp.int32, (n, t), 0).astype(F32)
    krow = lax.broadcasted_iota(jnp.int32, (PEER_TOPK, t), 0)
    rank = jnp.full((n, t), float(PEER_TOPK), F32)
    vals = jnp.zeros((PEER_TOPK, t), F32)
    for k in range(PEER_TOPK):
        m = jnp.max(s, axis=0, keepdims=True)
        hit = s == m
        if exact:
            hit = idx == jnp.min(jnp.where(hit, idx, float(n)), axis=0, keepdims=True)
        rank = jnp.where(hit, float(k), rank)
        s = jnp.where(hit, NEG_INF, s)
        vals = jnp.where(krow == k, m, vals)
    return vals, rank


SUBLANES = 8
N_CAND = PEER_TOPK + (SUBLANES - 1) * SUBLANES + SUBLANES


def _cand_flat_pos():
    r = lax.broadcasted_iota(jnp.int32, (N_CAND, LANES), 0)
    mid_lo, hi_lo = PEER_TOPK, N_CAND - SUBLANES
    sub_shift = SUBLANES.bit_length() - 1
    mid = (jnp.right_shift(r - mid_lo, sub_shift) + 1) * PEER_TOPK + jnp.bitwise_and(r - mid_lo, SUBLANES - 1)
    hi = (r - hi_lo + SUBLANES) * PEER_TOPK
    return jnp.where(r < mid_lo, r, jnp.where(r < hi_lo, mid, hi)).astype(F32)


def _pair_top16(v1, v2, r1, exact):
    cand = jnp.concatenate(
        [v1[0:1, :] + v2]
        + [v1[a:a + 1, :] + v2[0:SUBLANES, :] for a in range(1, SUBLANES)]
        + [v1[SUBLANES:, :] + v2[0:1, :]], axis=0)
    pos = _cand_flat_pos() if exact else None
    c0 = cand[0:1, :]
    zsum = jnp.zeros((1, LANES), F32)
    for k in range(PEER_TOPK):
        m = jnp.max(cand, axis=0, keepdims=True)
        hit = cand == m
        if exact:
            hit = pos == jnp.min(jnp.where(hit, pos, float(PEER_TOPK * PEER_TOPK)), axis=0, keepdims=True)
        cand = jnp.where(hit, NEG_INF, cand)
        zsum = zsum + jnp.exp(m - c0)
    taken = jnp.where(cand == NEG_INF, 1.0, 0.0)
    n_taken = jnp.sum(taken, axis=0, keepdims=True)
    lrow = jnp.zeros(r1.shape, F32)
    for a in range(PEER_TOPK):
        if a == 0:
            cnt_a = jnp.sum(taken[0:PEER_TOPK, :], axis=0, keepdims=True)
        elif a < SUBLANES:
            lo = PEER_TOPK + SUBLANES * (a - 1)
            cnt_a = jnp.sum(taken[lo:lo + SUBLANES, :], axis=0, keepdims=True)
        else:
            lo = N_CAND - SUBLANES + a - SUBLANES
            cnt_a = taken[lo:lo + 1, :]
        lrow = jnp.where(r1 == float(a), cnt_a, lrow)
    return lrow, zsum, n_taken


def _topk_kernel(sc_ref, r2_ref, e2_ref, l1_ref, e1_ref):
    n = PEER_N_KEYS
    tl = sc_ref.shape[1]
    ngrp = tl // LANES

    def body(it, carry):
        hd = lax.div(it, ngrp)
        gi = lax.rem(it, ngrp)
        lanes = pl.ds(pl.multiple_of(gi * LANES, LANES), LANES)
        s1 = sc_ref[pl.ds(pl.multiple_of(hd * 2 * n, n), n), lanes]
        s2 = sc_ref[pl.ds(pl.multiple_of(hd * 2 * n + n, n), n), lanes]

        def run(exact):
            v1, r1 = _top16(s1, exact)
            v2, r2 = _top16(s2, exact)
            lrow, zsum, n_taken = _pair_top16(v1, v2, r1, exact)
            r2_ref[hd, :, lanes] = pltpu.bitcast(r2.astype(BF16), jnp.uint32)
            e2_ref[hd, :, lanes] = pltpu.bitcast(jnp.exp(s2 - v2[0:1, :]).astype(BF16), jnp.uint32)
            l1_ref[hd, :, lanes] = _dup_bf16_words(lrow)
            e1_ref[hd, :, lanes] = _dup_bf16_words(jnp.exp(s1 - v1[0:1, :]) / zsum)
            k = float(PEER_TOPK)
            picked = lambda r: jnp.sum(jnp.where(r < k, 1.0, 0.0), axis=0, keepdims=True)
            return jnp.abs(picked(r1) - k) + jnp.abs(picked(r2) - k) + jnp.abs(n_taken - k)

        excess = run(exact=False)

        @pl.when(jnp.max(excess) > 0.0)
        def _():
            run(exact=True)

        return carry

    lax.fori_loop(0, PEER_HEADS * ngrp, body, 0)


def _topk(scores_t, tl):
    nsc, t = scores_t.shape
    spec = lambda rows: pl.BlockSpec((PEER_HEADS, rows, tl), lambda i: (0, 0, i))
    sds = lambda rows: jax.ShapeDtypeStruct((PEER_HEADS, rows, t), jnp.uint32)
    half, full = PEER_N_KEYS // 2, PEER_N_KEYS
    return pl.pallas_call(
        _topk_kernel,
        grid=(t // tl,),
        in_specs=[pl.BlockSpec((nsc, tl), lambda i: (0, i))],
        out_specs=[spec(half), spec(half), spec(full), spec(full)],
        out_shape=[sds(half), sds(half), sds(full), sds(full)],
        compiler_params=_cparams("arbitrary"),
        name="topk",
    )(scores_t)


def _gelu(x):
    return 0.5 * x * (1.0 + lax.erf(x * (2.0 ** -0.5)))


def _dup_bf16_words(x):
    bits = pltpu.bitcast(x.astype(BF16).astype(F32), jnp.uint32)
    return bits | (bits >> 16)


def _bf16_rows(words, n):
    return pltpu.bitcast(jnp.broadcast_to(words, (n // 2, words.shape[1])), BF16)


def _peer_kernel(h2t_ref, dn_ref, up_ref, r2_ref, e2_ref, l1_ref, e1_ref, x1_ref, mod_ref, gf_ref,
                 o_ref, acc_ref, st_ref, pw_ref, *, rows_per_chunk):
    j = pl.program_id(2)
    n = PEER_N_KEYS

    @pl.when(j == 0)
    def _():
        acc_ref[...] = jnp.zeros_like(acc_ref)

    st_ref[...] = _dot(dn_ref[...], h2t_ref[...])
    zero = jnp.zeros((n, LANES), BF16)
    pair = 2
    for g in range(h2t_ref.shape[1] // LANES):
        lanes = slice(g * LANES, (g + 1) * LANES)
        for i1 in range(0, rows_per_chunk, pair):
            w = [None] * pair
            for hd in range(PEER_HEADS):
                r2 = pltpu.bitcast(r2_ref[hd, :, lanes], BF16)
                e2 = pltpu.bitcast(e2_ref[hd, :, lanes], BF16)
                for u in range(pair):
                    cnt = _bf16_rows(l1_ref[hd, i1 + u:i1 + u + 1, lanes], n)
                    e1 = _bf16_rows(e1_ref[hd, i1 + u:i1 + u + 1, lanes], n)
                    wh = jnp.where(r2 < cnt, e2, zero) * e1
                    w[u] = wh if w[u] is None else w[u] + wh
            for u in range(pair):
                rows = slice((i1 + u) * n, (i1 + u + 1) * n)
                pw_ref[rows, lanes] = _gelu(st_ref[rows, lanes]).astype(BF16) * w[u]
    acc_ref[...] += _dot(up_ref[...], pw_ref[...])

    @pl.when(j == pl.num_programs(2) - 1)
    def _():
        gt2 = mod_ref[0, 5:6, :]
        x2 = x1_ref[0] + gt2 * acc_ref[...].T
        inv = lax.rsqrt(jnp.mean(x2 * x2, axis=-1, keepdims=True) + EPS)
        o_ref[0] = (x2 * inv) * gf_ref[...]


def _peer(h2t, down, up_t, r2, e2, l1, e1, x1, mod3, g_final, tt, ec):
    b, s, d = x1.shape
    ne = down.shape[0]
    rpc = ec // PEER_N_KEYS
    spb = s // tt
    tok = lambda bi, si, j: (0, 0, bi * spb + si)
    return pl.pallas_call(
        functools.partial(_peer_kernel, rows_per_chunk=rpc),
        grid=(b, spb, ne // ec),
        in_specs=[pl.BlockSpec((d, tt), lambda bi, si, j: (0, bi * spb + si)),
                  pl.BlockSpec((ec, d), lambda bi, si, j: (j, 0)),
                  pl.BlockSpec((d, ec), lambda bi, si, j: (0, j)),
                  pl.BlockSpec((PEER_HEADS, PEER_N_KEYS // 2, tt), tok),
                  pl.BlockSpec((PEER_HEADS, PEER_N_KEYS // 2, tt), tok),
                  pl.BlockSpec((PEER_HEADS, rpc, tt), lambda bi, si, j: (0, j, bi * spb + si)),
                  pl.BlockSpec((PEER_HEADS, rpc, tt), lambda bi, si, j: (0, j, bi * spb + si)),
                  pl.BlockSpec((1, tt, d), lambda bi, si, j: (bi, si, 0)),
                  pl.BlockSpec((1, N_MOD, d), lambda bi, si, j: (bi, 0, 0)),
                  pl.BlockSpec((1, d), lambda bi, si, j: (0, 0))],
        out_specs=pl.BlockSpec((1, tt, d), lambda bi, si, j: (bi, si, 0)),
        out_shape=jax.ShapeDtypeStruct((b, s, d), F32),
        scratch_shapes=[pltpu.VMEM((d, tt), F32), pltpu.VMEM((ec, tt), F32), pltpu.VMEM((ec, tt), BF16)],
        compiler_params=_cparams("arbitrary", "arbitrary", "arbitrary"),
        name="peer",
    )(h2t, down, up_t, r2, e2, l1, e1, x1, mod3, g_final)


def _pad_heads(w):
    d = w.shape[0]
    w = w.reshape(d, N_HEADS, HEAD_DIM)
    return jnp.pad(w, ((0, 0), (0, 0), (0, LANES - HEAD_DIM))).reshape(d, HP)


def _bias_constants():
    pq = np.zeros((3 * LANES, HP), np.float32)
    pk = np.zeros((3 * LANES, HP), np.float32)
    cq = np.zeros((1, HP), np.float32)
    ck = np.zeros((1, HP), np.float32)
    cv = np.zeros((1, HP), np.float32)
    for h in range(N_HEADS):
        for part in range(3):
            pq[part * LANES + h, h * LANES + LANE_F_Q + part] = 1.0
            pk[part * LANES + h, h * LANES + LANE_ONE_Q + part] = -1.0
            cq[0, h * LANES + LANE_ONE_Q + part] = 1.0
            ck[0, h * LANES + LANE_F_Q + part] = 1.0
        cv[0, h * LANES + LANE_ONE_V] = 1.0
    return (jnp.asarray(pq, BF16), jnp.asarray(pk, BF16), jnp.asarray(cq), jnp.asarray(ck), jnp.asarray(cv))


def _tile(n, pref):
    return pref if n % pref == 0 else n


def kernel(x, c, w_ada, b_ada, g_norm_mix, w_in, b_forget, g_out_fox, g_out_sb, w_out, g_norm_ffn,
           w_query, sub_keys, expert_down, expert_up, g_final):
    b, s, d = x.shape
    assert w_ada.shape[0] == 1, "single-layer block: the final RMSNorm is fused into the PEER kernel"
    fw = N_HEADS * HEAD_DIM
    scale = HEAD_DIM ** -0.5
    pq, pk, cq, ck, cv = _bias_constants()
    tm = _tile(s, 512)
    c_pad = jnp.pad(c, ((0, -b % 8), (0, 0)))

    for l in range(1):
        mod = _ada(c_pad, w_ada[l], b_ada[l][None, :])[:b]
        mod3 = mod.reshape(b, N_MOD, d)

        wi = w_in[l]
        o1 = 3 * fw
        o2 = o1 + N_HEADS
        w_all = jnp.concatenate([
            _pad_heads(wi[:, 0:fw] * scale), _pad_heads(wi[:, fw:2 * fw]), _pad_heads(wi[:, 2 * fw:o1]),
            _pad_heads(wi[:, o2:o2 + fw] * scale), _pad_heads(wi[:, o2 + fw:o2 + 2 * fw]),
            _pad_heads(wi[:, o2 + 2 * fw:o2 + 3 * fw])], axis=1).astype(BF16)
        wf = jnp.pad(wi[:, o1:o2], ((0, 0), (0, LANES - N_HEADS)))
        wf_hi = wf.astype(BF16)
        wf_lo = (wf - wf_hi.astype(F32)).astype(BF16)
        b_f = jnp.pad(b_forget[l], (0, LANES - N_HEADS))[None, :]

        fq, fk, fv, sq, sk, sv = _inproj(x, mod3, g_norm_mix[l][None, :], w_all, wf_hi, wf_lo, b_f,
                                         pq, pk, cq, ck, cv, tm)

        g_fox = jnp.pad(g_out_fox[l], (0, LANES - HEAD_DIM))[None, :]
        g_sb = jnp.pad(g_out_sb[l], (0, LANES - HEAD_DIM))[None, :]
        fo = _fox(fq, fk, fv, g_fox, _tile(s, 256), _tile(s, 256))
        so = _sb(sq, sk, sv, g_sb, _tile(s, 256), _tile(s, 256))

        wo = w_out[l]
        w_out_p = jnp.pad(wo.reshape(2 * N_HEADS, HEAD_DIM, d),
                          ((0, 0), (0, LANES - HEAD_DIM), (0, 0))).reshape(2 * HP, d).astype(BF16)
        wc = _wc(sub_keys[l].reshape(PEER_HEADS * 2, PEER_N_KEYS, LANES), w_query[l])
        wc_hi = wc.astype(BF16)
        wc_lo = (wc - wc_hi.astype(F32)).astype(BF16)
        x1, h2t, scores_t = _outproj(x, fo, so, mod3, g_norm_ffn[l][None, :], w_out_p, wc_hi, wc_lo, tm)

        r2, e2, l1, e1 = _topk(scores_t, _tile(b * s, 512))
        x = _peer(h2t, expert_down[l].astype(BF16), expert_up[l].T.astype(BF16), r2, e2, l1, e1,
                  x1, mod3, g_final[None, :], _tile(s, 512), 1024)
    return x
```

```python
import functools

import numpy as np
import jax
import jax.numpy as jnp
from jax import lax
from jax.experimental import pallas as pl
from jax.experimental.pallas import tpu as pltpu

F32 = jnp.float32
BF16 = jnp.bfloat16

HEAD_DIM = 64
N_HEADS = 8
PEER_HEADS = 8
PEER_N_KEYS = 128
PEER_TOPK = 16
N_MOD = 6
EPS = 1e-6

LANES = 128
VMEM_LIMIT = 56 * 1024 * 1024

HP = N_HEADS * LANES
LANE_ONE_Q = HEAD_DIM
LANE_F_Q = HEAD_DIM + 3
LANE_ONE_V = HEAD_DIM
NEG_INF = float("-inf")
EXP_UNDERFLOW = -106.0


def _cparams(*sem, flags=None):
    return pltpu.CompilerParams(dimension_semantics=sem, vmem_limit_bytes=VMEM_LIMIT, flags=flags)


def _dot(a, b):
    return jnp.dot(a, b, preferred_element_type=F32)


def _dot_nt(a, b):
    return lax.dot_general(a, b, (((1,), (1,)), ((), ())), preferred_element_type=F32)


def _split2(x):
    hi = x.astype(BF16)
    lo = (x - hi.astype(F32)).astype(BF16)
    return hi, lo


def _split3(x):
    hi = x.astype(BF16)
    r = x - hi.astype(F32)
    mid = r.astype(BF16)
    lo = (r - mid.astype(F32)).astype(BF16)
    return hi, mid, lo


def _softplus(z):
    return jnp.maximum(z, 0.0) + jnp.log(1.0 + jnp.exp(-jnp.abs(z)))


def _ada_kernel(c_ref, w_ref, b_ref, o_ref):
    c = c_ref[...]
    a = c / (1.0 + jnp.exp(-c))
    ah, al = _split2(a)
    wh, wl = _split2(w_ref[...])
    o_ref[...] = _dot(ah, wh) + _dot(ah, wl) + _dot(al, wh) + b_ref[...]


def _ada(c_pad, w_ada, b_ada):
    rows, d = c_pad.shape
    n = w_ada.shape[1]
    bn = 1024
    return pl.pallas_call(
        _ada_kernel,
        grid=(n // bn,),
        in_specs=[pl.BlockSpec((rows, d), lambda j: (0, 0)),
                  pl.BlockSpec((d, bn), lambda j: (0, j)),
                  pl.BlockSpec((1, bn), lambda j: (0, j))],
        out_specs=pl.BlockSpec((rows, bn), lambda j: (0, j)),
        out_shape=jax.ShapeDtypeStruct((rows, n), F32),
        compiler_params=_cparams("arbitrary"),
        name="ada",
    )(c_pad, w_ada, b_ada)


def _wc_kernel(sk_ref, wq_ref, o_ref):
    sh, sl = _split2(sk_ref[0])
    wh, wl = _split2(wq_ref[...])
    o_ref[...] = _dot_nt(sh, wh) + _dot_nt(sh, wl) + _dot_nt(sl, wh)


def _wc(sub_keys, w_query):
    d = w_query.shape[0]
    nhp = sub_keys.shape[0]
    return pl.pallas_call(
        _wc_kernel,
        grid=(nhp,),
        in_specs=[pl.BlockSpec((1, PEER_N_KEYS, LANES), lambda i: (i, 0, 0)),
                  pl.BlockSpec((d, LANES), lambda i: (0, i))],
        out_specs=pl.BlockSpec((PEER_N_KEYS, d), lambda i: (i, 0)),
        out_shape=jax.ShapeDtypeStruct((nhp * PEER_N_KEYS, d), F32),
        compiler_params=_cparams("arbitrary"),
        name="wc",
    )(sub_keys, w_query)


def _inproj_kernel(x_ref, mod_ref, g_ref, w_ref, wfh_ref, wfl_ref, bf_ref, pq_ref, pk_ref,
                   cq_ref, ck_ref, cv_ref,
                   fq_ref, fk_ref, fv_ref, sq_ref, sk_ref, sv_ref, carry_ref):
    tm = x_ref.shape[1]

    @pl.when(pl.program_id(1) == 0)
    def _():
        carry_ref[...] = jnp.zeros_like(carry_ref)

    x = x_ref[0]
    inv = lax.rsqrt(jnp.mean(x * x, axis=-1, keepdims=True) + EPS)
    sh1 = mod_ref[0, 0:1, :]
    sc1 = mod_ref[0, 1:2, :]
    h = (x * inv) * g_ref[...] * (1.0 + sc1) + sh1
    hh, hl = _split2(h)

    fl = _dot(hh, wfh_ref[...]) + _dot(hh, wfl_ref[...]) + _dot(hl, wfh_ref[...]) + bf_ref[...]
    lf = jnp.minimum(fl, 0.0) - jnp.log(1.0 + jnp.exp(-jnp.abs(fl)))
    row = lax.broadcasted_iota(jnp.int32, (tm, tm), 0)
    col = lax.broadcasted_iota(jnp.int32, (tm, tm), 1)
    tri = jnp.where(col <= row, 1.0, 0.0).astype(BF16)
    l0, l1, l2 = _split3(lf)
    fcum = _dot(tri, l0) + _dot(tri, l1) + _dot(tri, l2) + carry_ref[...]
    carry_ref[...] = fcum[tm - 1:tm, :]
    f0, f1, f2 = _split3(fcum)
    fcat = jnp.concatenate([f0, f1, f2], axis=1)

    fq_ref[0] = (_dot(hh, w_ref[:, 0 * HP:1 * HP]) + _dot(fcat, pq_ref[...]) + cq_ref[...]).astype(BF16)
    fk_ref[0] = (_dot(hh, w_ref[:, 1 * HP:2 * HP]) + _dot(fcat, pk_ref[...]) + ck_ref[...]).astype(BF16)
    fv_ref[0] = (_dot(hh, w_ref[:, 2 * HP:3 * HP]) + cv_ref[...]).astype(BF16)
    sq_ref[0] = _dot(hh, w_ref[:, 3 * HP:4 * HP]).astype(BF16)
    sk_ref[0] = _dot(hh, w_ref[:, 4 * HP:5 * HP]).astype(BF16)
    sv_ref[0] = _dot(hh, w_ref[:, 5 * HP:6 * HP]).astype(BF16)


def _inproj(x, mod3, g_mix, w_all, wf_hi, wf_lo, b_f, pq, pk, cq, ck, cv, tm):
    b, s, d = x.shape
    const = lambda shape: pl.BlockSpec(shape, lambda bi, si: (0,) * len(shape),
                                       pipeline_mode=pl.Buffered(1))
    out_spec = pl.BlockSpec((1, tm, HP), lambda bi, si: (bi, si, 0))
    out_sds = jax.ShapeDtypeStruct((b, s, HP), BF16)
    return pl.pallas_call(
        _inproj_kernel,
        grid=(b, s // tm),
        in_specs=[pl.BlockSpec((1, tm, d), lambda bi, si: (bi, si, 0)),
                  pl.BlockSpec((1, N_MOD, d), lambda bi, si: (bi, 0, 0)),
                  const((1, d)),
                  const((d, 6 * HP)),
                  const((d, LANES)), const((d, LANES)), const((1, LANES)),
                  const((3 * LANES, HP)), const((3 * LANES, HP)),
                  const((1, HP)), const((1, HP)), const((1, HP))],
        out_specs=[out_spec] * 6,
        out_shape=[out_sds] * 6,
        scratch_shapes=[pltpu.VMEM((1, LANES), F32)],
        compiler_params=_cparams("arbitrary", "arbitrary"),
        name="inproj",
    )(x, mod3, g_mix, w_all, wf_hi, wf_lo, b_f, pq, pk, cq, ck, cv)


def _head_norm(o, g):
    lane = lax.broadcasted_iota(jnp.int32, o.shape, 1)
    o = jnp.where(lane < HEAD_DIM, o, 0.0)
    ms = jnp.sum(o * o, axis=-1, keepdims=True) * (1.0 / HEAD_DIM)
    return o * lax.rsqrt(ms + EPS) * g


def _fox_kernel(q_ref, k_ref, v_ref, g_ref, o_ref, m_ref, acc_ref, *, tq, tk):
    s_len = q_ref.shape[1]
    n_diag = tq // tk
    lane1 = lax.broadcasted_iota(jnp.int32, (1, LANES), 1)
    head_lanes = lane1 < HEAD_DIM
    row = lax.broadcasted_iota(jnp.int32, (tq, tk), 0)
    col = lax.broadcasted_iota(jnp.int32, (tq, tk), 1)

    def knorm(i, mx):
        kb = k_ref[0, pl.ds(pl.multiple_of(i * tk, tk), tk), :].astype(F32)
        kb = jnp.where(head_lanes, kb, 0.0)
        n2 = jnp.sum(kb * kb, axis=-1, keepdims=True)
        return jnp.maximum(mx, jnp.max(n2, axis=0, keepdims=True))

    kmax = jnp.sqrt(lax.fori_loop(0, s_len // tk, knorm, jnp.zeros((1, 1), F32)))

    def q_tile(qi, carry):
        q0 = pl.multiple_of(qi * tq, tq)
        q = q_ref[0, pl.ds(q0, tq), :]
        qf = q.astype(F32)
        qn = jnp.sqrt(jnp.sum(jnp.where(head_lanes, qf * qf, 0.0), axis=-1, keepdims=True))
        f_t = jnp.sum(jnp.where((lane1 >= LANE_F_Q) & (lane1 < LANE_F_Q + 3), qf, 0.0),
                      axis=-1, keepdims=True)
        reach = qn * kmax + f_t
        m_ref[...] = jnp.full_like(m_ref, NEG_INF)
        acc_ref[...] = jnp.zeros_like(acc_ref)

        def step(start, mask):
            k = k_ref[0, pl.ds(start, tk), :]
            v = v_ref[0, pl.ds(start, tk), :]
            s = _dot_nt(q, k)
            if mask is not None:
                s = jnp.where(mask, s, NEG_INF)
            m_prev = m_ref[...]
            m_new = jnp.maximum(m_prev, jnp.max(s, axis=-1, keepdims=True))
            p = jnp.exp(s - m_new)
            acc_ref[...] = jnp.exp(m_prev - m_new) * acc_ref[...] + _dot(p.astype(BF16), v)
            m_ref[...] = m_new

        def block_matters(kb):
            last = k_ref[0, pl.ds(pl.multiple_of(kb * tk + tk - 16, 16), 16), :][15:16, :].astype(F32)
            neg_f_last = jnp.sum(jnp.where((lane1 >= LANE_ONE_Q) & (lane1 < LANE_ONE_Q + 3), last, 0.0),
                                 axis=-1, keepdims=True)
            return jnp.max(reach + neg_f_last - m_ref[...]) > EXP_UNDERFLOW

        for d in range(n_diag):
            step(pl.multiple_of(q0 + d * tk, tk), col + d * tk <= row)

        def cond(st):
            return st[1]

        def body(st):
            kb = st[0]
            step(pl.multiple_of(kb * tk, tk), None)
            nxt = kb - 1
            return nxt, (nxt >= 0) & block_matters(jnp.maximum(nxt, 0))

        kb0 = qi * n_diag - 1
        lax.while_loop(cond, body, (kb0, (kb0 >= 0) & block_matters(jnp.maximum(kb0, 0))))

        acc = acc_ref[...]
        lane = lax.broadcasted_iota(jnp.int32, acc.shape, 1)
        l = jnp.sum(jnp.where(lane == LANE_ONE_V, acc, 0.0), axis=-1, keepdims=True)
        o_ref[0, pl.ds(q0, tq), :] = _head_norm(acc / l, g_ref[...]).astype(o_ref.dtype)
        return carry

    lax.fori_loop(0, s_len // tq, q_tile, 0)


def _attn_call(body, name, q, k, v, g, tq):
    b, s, _ = q.shape
    spec = pl.BlockSpec((1, s, LANES), lambda bi, hi: (bi, 0, hi))
    return pl.pallas_call(
        body,
        grid=(b, N_HEADS),
        in_specs=[spec, spec, spec, pl.BlockSpec((1, LANES), lambda bi, hi: (0, 0))],
        out_specs=spec,
        out_shape=jax.ShapeDtypeStruct((b, s, HP), BF16),
        scratch_shapes=[pltpu.VMEM((tq, 1), F32), pltpu.VMEM((tq, LANES), F32)],
        compiler_params=_cparams("arbitrary", "arbitrary"),
        name=name,
    )(q, k, v, g)


def _fox(q, k, v, g, tq, tk):
    return _attn_call(functools.partial(_fox_kernel, tq=tq, tk=tk), "fox", q, k, v, g, tq)


def _sb_kernel(q_ref, k_ref, v_ref, g_ref, o_ref, c_ref, acc_ref, *, tq, tk):
    s_len = q_ref.shape[1]
    n_diag = tq // tk
    jrow = lax.broadcasted_iota(jnp.int32, (tk, tk), 0)
    scol = lax.broadcasted_iota(jnp.int32, (tk, tk), 1)
    later = jnp.where(jrow > scol, 1.0, 0.0).astype(BF16)
    row = lax.broadcasted_iota(jnp.int32, (tq, tk), 0)
    col = lax.broadcasted_iota(jnp.int32, (tq, tk), 1)

    def q_tile(qi, carry):
        q0 = pl.multiple_of(qi * tq, tq)
        q = q_ref[0, pl.ds(q0, tq), :]
        c_ref[...] = jnp.zeros_like(c_ref)
        acc_ref[...] = jnp.zeros_like(acc_ref)

        def step(start, mask):
            k = k_ref[0, pl.ds(start, tk), :]
            v = v_ref[0, pl.ds(start, tk), :]
            z = _dot_nt(q, k)
            sp = _softplus(z)
            spm = sp if mask is None else jnp.where(mask, sp, 0.0)
            hi, lo = _split2(spm)
            rest = _dot(hi, later) + _dot(lo, later) + c_ref[...]
            a = jnp.exp(z - sp - rest)
            if mask is not None:
                a = jnp.where(mask, a, 0.0)
            acc_ref[...] += _dot(a.astype(BF16), v)
            c_ref[...] += jnp.sum(spm, axis=-1, keepdims=True)

        def more_matters():
            return jnp.min(c_ref[...]) < -EXP_UNDERFLOW

        for d in reversed(range(n_diag)):
            step(pl.multiple_of(q0 + d * tk, tk), col + d * tk < row)

        def cond(st):
            return st[1]

        def body(st):
            kb = st[0]
            step(pl.multiple_of(kb * tk, tk), None)
            nxt = kb - 1
            return nxt, (nxt >= 0) & more_matters()

        kb0 = qi * n_diag - 1
        lax.while_loop(cond, body, (kb0, (kb0 >= 0) & more_matters()))
        o_ref[0, pl.ds(q0, tq), :] = _head_norm(acc_ref[...], g_ref[...]).astype(o_ref.dtype)
        return carry

    lax.fori_loop(0, s_len // tq, q_tile, 0)


def _sb(q, k, v, g, tq, tk):
    return _attn_call(functools.partial(_sb_kernel, tq=tq, tk=tk), "sb", q, k, v, g, tq)


def _outproj_kernel(x_ref, fo_ref, so_ref, mod_ref, g_ref, wo_ref, wch_ref, wcl_ref,
                    x1_ref, h2t_ref, sc_ref):
    gt1 = mod_ref[0, 2:3, :]
    sh2 = mod_ref[0, 3:4, :]
    sc2 = mod_ref[0, 4:5, :]
    mixed = _dot(fo_ref[0], wo_ref[0:HP, :]) + _dot(so_ref[0], wo_ref[HP:2 * HP, :])
    x1 = x_ref[0] + gt1 * mixed
    x1_ref[0] = x1
    inv = lax.rsqrt(jnp.mean(x1 * x1, axis=-1, keepdims=True) + EPS)
    h2 = (x1 * inv) * g_ref[...] * (1.0 + sc2) + sh2
    h2t = h2.T
    th, tl = _split2(h2t)
    h2t_ref[...] = th
    sc_ref[...] = _dot(wch_ref[...], th) + _dot(wch_ref[...], tl) + _dot(wcl_ref[...], th)


def _outproj(x, fo, so, mod3, g_ffn, w_out_p, wc_hi, wc_lo, tm):
    b, s, d = x.shape
    t = b * s
    nsc = wc_hi.shape[0]
    const = lambda shape: pl.BlockSpec(shape, lambda bi, si: (0,) * len(shape),
                                       pipeline_mode=pl.Buffered(1))
    spb = s // tm
    return pl.pallas_call(
        _outproj_kernel,
        grid=(b, spb),
        in_specs=[pl.BlockSpec((1, tm, d), lambda bi, si: (bi, si, 0)),
                  pl.BlockSpec((1, tm, HP), lambda bi, si: (bi, si, 0)),
                  pl.BlockSpec((1, tm, HP), lambda bi, si: (bi, si, 0)),
                  pl.BlockSpec((1, N_MOD, d), lambda bi, si: (bi, 0, 0)),
                  const((1, d)),
                  const((2 * HP, d)),
                  const((nsc, d)), const((nsc, d))],
        out_specs=[pl.BlockSpec((1, tm, d), lambda bi, si: (bi, si, 0)),
                   pl.BlockSpec((d, tm), lambda bi, si: (0, bi * spb + si)),
                   pl.BlockSpec((nsc, tm), lambda bi, si: (0, bi * spb + si))],
        out_shape=[jax.ShapeDtypeStruct((b, s, d), F32),
                   jax.ShapeDtypeStruct((d, t), BF16),
                   jax.ShapeDtypeStruct((nsc, t), F32)],
        compiler_params=_cparams("arbitrary", "arbitrary"),
        name="outproj",
    )(x, fo, so, mod3, g_ffn, w_out_p, wc_hi, wc_lo)


def _top16(s, exact):
    n, t = s.shape
    idx = lax.broadcasted_iota(jnp.int32, (n, t), 0).astype(F32)
    krow = lax.broadcasted_iota(jnp.int32, (PEER_TOPK, t), 0)
    rank = jnp.full((n, t), float(PEER_TOPK), F32)
    vals = jnp.zeros((PEER_TOPK, t), F32)
    for k in range(PEER_TOPK):
        m = jnp.max(s, axis=0, keepdims=True)
        hit = s == m
        if exact:
            hit = idx == jnp.min(jnp.where(hit, idx, float(n)), axis=0, keepdims=True)
        rank = jnp.where(hit, float(k), rank)
        s = jnp.where(hit, NEG_INF, s)
        vals = jnp.where(krow == k, m, vals)
    return vals, rank


SUBLANES = 8
N_CAND = PEER_TOPK + (SUBLANES - 1) * SUBLANES + SUBLANES


def _cand_flat_pos():
    r = lax.broadcasted_iota(jnp.int32, (N_CAND, LANES), 0)
    mid_lo, hi_lo = PEER_TOPK, N_CAND - SUBLANES
    sub_shift = SUBLANES.bit_length() - 1
    mid = (jnp.right_shift(r - mid_lo, sub_shift) + 1) * PEER_TOPK + jnp.bitwise_and(r - mid_lo, SUBLANES - 1)
    hi = (r - hi_lo + SUBLANES) * PEER_TOPK
    return jnp.where(r < mid_lo, r, jnp.where(r < hi_lo, mid, hi)).astype(F32)


def _pair_top16(v1, v2, r1, exact):
    cand = jnp.concatenate(
        [v1[0:1, :] + v2]
        + [v1[a:a + 1, :] + v2[0:SUBLANES, :] for a in range(1, SUBLANES)]
        + [v1[SUBLANES:, :] + v2[0:1, :]], axis=0)
    pos = _cand_flat_pos() if exact else None
    c0 = cand[0:1, :]
    zsum = jnp.zeros((1, LANES), F32)
    for k in range(PEER_TOPK):
        m = jnp.max(cand, axis=0, keepdims=True)
        hit = cand == m
        if exact:
            hit = pos == jnp.min(jnp.where(hit, pos, float(PEER_TOPK * PEER_TOPK)), axis=0, keepdims=True)
        cand = jnp.where(hit, NEG_INF, cand)
        zsum = zsum + jnp.exp(m - c0)
    taken = jnp.where(cand == NEG_INF, 1.0, 0.0)
    n_taken = jnp.sum(taken, axis=0, keepdims=True)
    lrow = jnp.zeros(r1.shape, F32)
    for a in range(PEER_TOPK):
        if a == 0:
            cnt_a = jnp.sum(taken[0:PEER_TOPK, :], axis=0, keepdims=True)
        elif a < SUBLANES:
            lo = PEER_TOPK + SUBLANES * (a - 1)
            cnt_a = jnp.sum(taken[lo:lo + SUBLANES, :], axis=0, keepdims=True)
        else:
            lo = N_CAND - SUBLANES + a - SUBLANES
            cnt_a = taken[lo:lo + 1, :]
        lrow = jnp.where(r1 == float(a), cnt_a, lrow)
    return lrow, zsum, n_taken


def _topk_kernel(sc_ref, r2_ref, e2_ref, l1_ref, e1_ref):
    n = PEER_N_KEYS
    tl = sc_ref.shape[1]
    ngrp = tl // LANES

    def body(it, carry):
        hd = lax.div(it, ngrp)
        gi = lax.rem(it, ngrp)
        lanes = pl.ds(pl.multiple_of(gi * LANES, LANES), LANES)
        s1 = sc_ref[pl.ds(pl.multiple_of(hd * 2 * n, n), n), lanes]
        s2 = sc_ref[pl.ds(pl.multiple_of(hd * 2 * n + n, n), n), lanes]

        def run(exact):
            v1, r1 = _top16(s1, exact)
            v2, r2 = _top16(s2, exact)
            lrow, zsum, n_taken = _pair_top16(v1, v2, r1, exact)
            r2_ref[hd, :, lanes] = pltpu.bitcast(r2.astype(BF16), jnp.uint32)
            e2_ref[hd, :, lanes] = pltpu.bitcast(jnp.exp(s2 - v2[0:1, :]).astype(BF16), jnp.uint32)
            l1_ref[hd, :, lanes] = _dup_bf16_words(lrow)
            e1_ref[hd, :, lanes] = _dup_bf16_words(jnp.exp(s1 - v1[0:1, :]) / zsum)
            k = float(PEER_TOPK)
            picked = lambda r: jnp.sum(jnp.where(r < k, 1.0, 0.0), axis=0, keepdims=True)
            return jnp.abs(picked(r1) - k) + jnp.abs(picked(r2) - k) + jnp.abs(n_taken - k)

        excess = run(exact=False)

        @pl.when(jnp.max(excess) > 0.0)
        def _():
            run(exact=True)

        return carry

    lax.fori_loop(0, PEER_HEADS * ngrp, body, 0)


def _topk(scores_t, tl):
    nsc, t = scores_t.shape
    spec = lambda rows: pl.BlockSpec((PEER_HEADS, rows, tl), lambda i: (0, 0, i))
    sds = lambda rows: jax.ShapeDtypeStruct((PEER_HEADS, rows, t), jnp.uint32)
    half, full = PEER_N_KEYS // 2, PEER_N_KEYS
    return pl.pallas_call(
        _topk_kernel,
        grid=(t // tl,),
        in_specs=[pl.BlockSpec((nsc, tl), lambda i: (0, i))],
        out_specs=[spec(half), spec(half), spec(full), spec(full)],
        out_shape=[sds(half), sds(half), sds(full), sds(full)],
        compiler_params=_cparams("arbitrary"),
        name="topk",
    )(scores_t)


def _gelu(x):
    return 0.5 * x * (1.0 + lax.erf(x * (2.0 ** -0.5)))


def _dup_bf16_words(x):
    bits = pltpu.bitcast(x.astype(BF16).astype(F32), jnp.uint32)
    return bits | (bits >> 16)


def _bf16_rows(words, n):
    return pltpu.bitcast(jnp.broadcast_to(words, (n // 2, words.shape[1])), BF16)


PEER_PIPE_LAG = 2


def _peer_kernel(h2t_ref, dn_ref, up_ref, r2_ref, e2_ref, l1_ref, e1_ref, x1_ref, mod_ref, gf_ref,
                 o_ref, acc_ref, st0_ref, st1_ref, pw0_ref, pw1_ref, *, rows_per_chunk):
    j = pl.program_id(2)
    n = PEER_N_KEYS

    @pl.when(j == 0)
    def _():
        acc_ref[...] = jnp.zeros_like(acc_ref)
        st1_ref[...] = jnp.zeros_like(st1_ref)
        pw0_ref[...] = jnp.zeros_like(pw0_ref)

    tt = h2t_ref.shape[1]
    mxu_w = 2 * LANES

    def gate(st_r, pw_w, g):
        zero = jnp.zeros((n, LANES), BF16)
        pair = 2
        lanes = slice(g * LANES, (g + 1) * LANES)
        for i1 in range(0, rows_per_chunk, pair):
            w = [None] * pair
            for hd in range(PEER_HEADS):
                r2 = pltpu.bitcast(r2_ref[hd, :, lanes], BF16)
                e2 = pltpu.bitcast(e2_ref[hd, :, lanes], BF16)
                for u in range(pair):
                    cnt = _bf16_rows(l1_ref[hd, i1 + u:i1 + u + 1, lanes], n)
                    e1 = _bf16_rows(e1_ref[hd, i1 + u:i1 + u + 1, lanes], n)
                    wh = jnp.where(r2 < cnt, e2, zero) * e1
                    w[u] = wh if w[u] is None else w[u] + wh
            for u in range(pair):
                rows = slice((i1 + u) * n, (i1 + u + 1) * n)
                pw_w[rows, lanes] = _gelu(st_r[rows, lanes]).astype(BF16) * w[u]

    def stages(st_w, st_r, pw_w, pw_r):
        g = 0
        for c in range(tt // mxu_w):
            cols = slice(c * mxu_w, (c + 1) * mxu_w)
            st_w[:, cols] = _dot(dn_ref[...], h2t_ref[:, cols])
            gate(st_r, pw_w, g)
            acc_ref[:, cols] += _dot(up_ref[...], pw_r[:, cols])
            gate(st_r, pw_w, g + 1)
            g += 2

    @pl.when(lax.rem(j, 2) == 0)
    def _():
        stages(st0_ref, st1_ref, pw1_ref, pw0_ref)

    @pl.when(lax.rem(j, 2) == 1)
    def _():
        stages(st1_ref, st0_ref, pw0_ref, pw1_ref)

    @pl.when(j == pl.num_programs(2) - 1)
    def _():
        gt2 = mod_ref[0, 5:6, :]
        x2 = x1_ref[0] + gt2 * acc_ref[...].T
        inv = lax.rsqrt(jnp.mean(x2 * x2, axis=-1, keepdims=True) + EPS)
        o_ref[0] = (x2 * inv) * gf_ref[...]


def _peer(h2t, down, up_t, r2, e2, l1, e1, x1, mod3, g_final, tt, ec):
    b, s, d = x1.shape
    ne = down.shape[0]
    rpc = ec // PEER_N_KEYS
    spb = s // tt
    nj = ne // ec
    tok = lambda bi, si, j: (0, 0, bi * spb + si)
    chunk = lambda j, lag: jnp.clip(j - lag, 0, nj - 1)
    gate_tab = pl.BlockSpec((PEER_HEADS, rpc, tt), lambda bi, si, j: (0, chunk(j, 1), bi * spb + si))
    return pl.pallas_call(
        functools.partial(_peer_kernel, rows_per_chunk=rpc),
        grid=(b, spb, nj + PEER_PIPE_LAG),
        in_specs=[pl.BlockSpec((d, tt), lambda bi, si, j: (0, bi * spb + si)),
                  pl.BlockSpec((ec, d), lambda bi, si, j: (chunk(j, 0), 0)),
                  pl.BlockSpec((d, ec), lambda bi, si, j: (0, chunk(j, 2))),
                  pl.BlockSpec((PEER_HEADS, PEER_N_KEYS // 2, tt), tok),
                  pl.BlockSpec((PEER_HEADS, PEER_N_KEYS // 2, tt), tok),
                  gate_tab, gate_tab,
                  pl.BlockSpec((1, tt, d), lambda bi, si, j: (bi, si, 0)),
                  pl.BlockSpec((1, N_MOD, d), lambda bi, si, j: (bi, 0, 0)),
                  pl.BlockSpec((1, d), lambda bi, si, j: (0, 0))],
        out_specs=pl.BlockSpec((1, tt, d), lambda bi, si, j: (bi, si, 0)),
        out_shape=jax.ShapeDtypeStruct((b, s, d), F32),
        scratch_shapes=[pltpu.VMEM((d, tt), F32),
                        pltpu.VMEM((ec, tt), F32), pltpu.VMEM((ec, tt), F32),
                        pltpu.VMEM((ec, tt), BF16), pltpu.VMEM((ec, tt), BF16)],
        compiler_params=_cparams("arbitrary", "arbitrary", "arbitrary"),
        name="peer",
    )(h2t, down, up_t, r2, e2, l1, e1, x1, mod3, g_final)


def _pad_heads(w):
    d = w.shape[0]
    w = w.reshape(d, N_HEADS, HEAD_DIM)
    return jnp.pad(w, ((0, 0), (0, 0), (0, LANES - HEAD_DIM))).reshape(d, HP)


def _bias_constants():
    pq = np.zeros((3 * LANES, HP), np.float32)
    pk = np.zeros((3 * LANES, HP), np.float32)
    cq = np.zeros((1, HP), np.float32)
    ck = np.zeros((1, HP), np.float32)
    cv = np.zeros((1, HP), np.float32)
    for h in range(N_HEADS):
        for part in range(3):
            pq[part * LANES + h, h * LANES + LANE_F_Q + part] = 1.0
            pk[part * LANES + h, h * LANES + LANE_ONE_Q + part] = -1.0
            cq[0, h * LANES + LANE_ONE_Q + part] = 1.0
            ck[0, h * LANES + LANE_F_Q + part] = 1.0
        cv[0, h * LANES + LANE_ONE_V] = 1.0
    return (jnp.asarray(pq, BF16), jnp.asarray(pk, BF16), jnp.asarray(cq), jnp.asarray(ck), jnp.asarray(cv))


def _tile(n, pref):
    return pref if n % pref == 0 else n


def kernel(x, c, w_ada, b_ada, g_norm_mix, w_in, b_forget, g_out_fox, g_out_sb, w_out, g_norm_ffn,
           w_query, sub_keys, expert_down, expert_up, g_final):
    b, s, d = x.shape
    assert w_ada.shape[0] == 1, "single-layer block: the final RMSNorm is fused into the PEER kernel"
    fw = N_HEADS * HEAD_DIM
    scale = HEAD_DIM ** -0.5
    pq, pk, cq, ck, cv = _bias_constants()
    tm = _tile(s, 512)
    c_pad = jnp.pad(c, ((0, -b % 8), (0, 0)))

    for l in range(1):
        mod = _ada(c_pad, w_ada[l], b_ada[l][None, :])[:b]
        mod3 = mod.reshape(b, N_MOD, d)

        wi = w_in[l]
        o1 = 3 * fw
        o2 = o1 + N_HEADS
        w_all = jnp.concatenate([
            _pad_heads(wi[:, 0:fw] * scale), _pad_heads(wi[:, fw:2 * fw]), _pad_heads(wi[:, 2 * fw:o1]),
            _pad_heads(wi[:, o2:o2 + fw] * scale), _pad_heads(wi[:, o2 + fw:o2 + 2 * fw]),
            _pad_heads(wi[:, o2 + 2 * fw:o2 + 3 * fw])], axis=1).astype(BF16)
        wf = jnp.pad(wi[:, o1:o2], ((0, 0), (0, LANES - N_HEADS)))
        wf_hi = wf.astype(BF16)
        wf_lo = (wf - wf_hi.astype(F32)).astype(BF16)
        b_f = jnp.pad(b_forget[l], (0, LANES - N_HEADS))[None, :]

        fq, fk, fv, sq, sk, sv = _inproj(x, mod3, g_norm_mix[l][None, :], w_all, wf_hi, wf_lo, b_f,
                                         pq, pk, cq, ck, cv, tm)

        g_fox = jnp.pad(g_out_fox[l], (0, LANES - HEAD_DIM))[None, :]
        g_sb = jnp.pad(g_out_sb[l], (0, LANES - HEAD_DIM))[None, :]
        fo = _fox(fq, fk, fv, g_fox, _tile(s, 256), _tile(s, 256))
        so = _sb(sq, sk, sv, g_sb, _tile(s, 256), _tile(s, 256))

        wo = w_out[l]
        w_out_p = jnp.pad(wo.reshape(2 * N_HEADS, HEAD_DIM, d),
                          ((0, 0), (0, LANES - HEAD_DIM), (0, 0))).reshape(2 * HP, d).astype(BF16)
        wc = _wc(sub_keys[l].reshape(PEER_HEADS * 2, PEER_N_KEYS, LANES), w_query[l])
        wc_hi = wc.astype(BF16)
        wc_lo = (wc - wc_hi.astype(F32)).astype(BF16)
        x1, h2t, scores_t = _outproj(x, fo, so, mod3, g_norm_ffn[l][None, :], w_out_p, wc_hi, wc_lo, tm)

        r2, e2, l1, e1 = _topk(scores_t, _tile(b * s, 512))
        x = _peer(h2t, expert_down[l].astype(BF16), expert_up[l].T.astype(BF16), r2, e2, l1, e1,
                  x1, mod3, g_final[None, :], _tile(s, 512), 1024)
    return x
```

```python
import functools

import numpy as np
import jax
import jax.numpy as jnp
from jax import lax
from jax.experimental import pallas as pl
from jax.experimental.pallas import tpu as pltpu

F32 = jnp.float32
BF16 = jnp.bfloat16

HEAD_DIM = 64
N_HEADS = 8
PEER_HEADS = 8
PEER_N_KEYS = 128
PEER_TOPK = 16
N_MOD = 6
EPS = 1e-6

LANES = 128
VMEM_LIMIT = 56 * 1024 * 1024

HP = N_HEADS * LANES
LANE_ONE_Q = HEAD_DIM
LANE_F_Q = HEAD_DIM + 3
LANE_ONE_V = HEAD_DIM
NEG_INF = float("-inf")
EXP_UNDERFLOW = -106.0


def _cparams(*sem, flags=None):
    return pltpu.CompilerParams(dimension_semantics=sem, vmem_limit_bytes=VMEM_LIMIT, flags=flags)


def _dot(a, b):
    return jnp.dot(a, b, preferred_element_type=F32)


def _dot_nt(a, b):
    return lax.dot_general(a, b, (((1,), (1,)), ((), ())), preferred_element_type=F32)


def _split2(x):
    hi = x.astype(BF16)
    lo = (x - hi.astype(F32)).astype(BF16)
    return hi, lo


def _split3(x):
    hi = x.astype(BF16)
    r = x - hi.astype(F32)
    mid = r.astype(BF16)
    lo = (r - mid.astype(F32)).astype(BF16)
    return hi, mid, lo


def _softplus(z):
    return jnp.maximum(z, 0.0) + jnp.log(1.0 + jnp.exp(-jnp.abs(z)))


def _ada_kernel(c_ref, w_ref, b_ref, o_ref):
    c = c_ref[...]
    a = c / (1.0 + jnp.exp(-c))
    ah, al = _split2(a)
    wh, wl = _split2(w_ref[...])
    o_ref[...] = _dot(ah, wh) + _dot(ah, wl) + _dot(al, wh) + b_ref[...]


def _ada(c_pad, w_ada, b_ada):
    rows, d = c_pad.shape
    n = w_ada.shape[1]
    bn = 1024
    return pl.pallas_call(
        _ada_kernel,
        grid=(n // bn,),
        in_specs=[pl.BlockSpec((rows, d), lambda j: (0, 0)),
                  pl.BlockSpec((d, bn), lambda j: (0, j)),
                  pl.BlockSpec((1, bn), lambda j: (0, j))],
        out_specs=pl.BlockSpec((rows, bn), lambda j: (0, j)),
        out_shape=jax.ShapeDtypeStruct((rows, n), F32),
        compiler_params=_cparams("arbitrary"),
        name="ada",
    )(c_pad, w_ada, b_ada)


def _wc_kernel(sk_ref, wq_ref, o_ref):
    sh, sl = _split2(sk_ref[0])
    wh, wl = _split2(wq_ref[...])
    o_ref[...] = _dot_nt(sh, wh) + _dot_nt(sh, wl) + _dot_nt(sl, wh)


def _wc(sub_keys, w_query):
    d = w_query.shape[0]
    nhp = sub_keys.shape[0]
    return pl.pallas_call(
        _wc_kernel,
        grid=(nhp,),
        in_specs=[pl.BlockSpec((1, PEER_N_KEYS, LANES), lambda i: (i, 0, 0)),
                  pl.BlockSpec((d, LANES), lambda i: (0, i))],
        out_specs=pl.BlockSpec((PEER_N_KEYS, d), lambda i: (i, 0)),
        out_shape=jax.ShapeDtypeStruct((nhp * PEER_N_KEYS, d), F32),
        compiler_params=_cparams("arbitrary"),
        name="wc",
    )(sub_keys, w_query)


def _inproj_kernel(x_ref, mod_ref, g_ref, w_ref, wfh_ref, wfl_ref, bf_ref, pq_ref, pk_ref,
                   cq_ref, ck_ref, cv_ref,
                   fq_ref, fk_ref, fv_ref, sq_ref, sk_ref, sv_ref, carry_ref):
    tm = x_ref.shape[1]

    @pl.when(pl.program_id(1) == 0)
    def _():
        carry_ref[...] = jnp.zeros_like(carry_ref)

    x = x_ref[0]
    inv = lax.rsqrt(jnp.mean(x * x, axis=-1, keepdims=True) + EPS)
    sh1 = mod_ref[0, 0:1, :]
    sc1 = mod_ref[0, 1:2, :]
    h = (x * inv) * g_ref[...] * (1.0 + sc1) + sh1
    hh, hl = _split2(h)

    fl = _dot(hh, wfh_ref[...]) + _dot(hh, wfl_ref[...]) + _dot(hl, wfh_ref[...]) + bf_ref[...]
    lf = jnp.minimum(fl, 0.0) - jnp.log(1.0 + jnp.exp(-jnp.abs(fl)))
    row = lax.broadcasted_iota(jnp.int32, (tm, tm), 0)
    col = lax.broadcasted_iota(jnp.int32, (tm, tm), 1)
    tri = jnp.where(col <= row, 1.0, 0.0).astype(BF16)
    l0, l1, l2 = _split3(lf)
    fcum = _dot(tri, l0) + _dot(tri, l1) + _dot(tri, l2) + carry_ref[...]
    carry_ref[...] = fcum[tm - 1:tm, :]
    f0, f1, f2 = _split3(fcum)
    fcat = jnp.concatenate([f0, f1, f2], axis=1)

    fq_ref[0] = (_dot(hh, w_ref[:, 0 * HP:1 * HP]) + _dot(fcat, pq_ref[...]) + cq_ref[...]).astype(BF16)
    fk_ref[0] = (_dot(hh, w_ref[:, 1 * HP:2 * HP]) + _dot(fcat, pk_ref[...]) + ck_ref[...]).astype(BF16)
    fv_ref[0] = (_dot(hh, w_ref[:, 2 * HP:3 * HP]) + cv_ref[...]).astype(BF16)
    sq_ref[0] = _dot(hh, w_ref[:, 3 * HP:4 * HP]).astype(BF16)
    sk_ref[0] = _dot(hh, w_ref[:, 4 * HP:5 * HP]).astype(BF16)
    sv_ref[0] = _dot(hh, w_ref[:, 5 * HP:6 * HP]).astype(BF16)


def _inproj(x, mod3, g_mix, w_all, wf_hi, wf_lo, b_f, pq, pk, cq, ck, cv, tm):
    b, s, d = x.shape
    const = lambda shape: pl.BlockSpec(shape, lambda bi, si: (0,) * len(shape),
                                       pipeline_mode=pl.Buffered(1))
    out_spec = pl.BlockSpec((1, tm, HP), lambda bi, si: (bi, si, 0))
    out_sds = jax.ShapeDtypeStruct((b, s, HP), BF16)
    return pl.pallas_call(
        _inproj_kernel,
        grid=(b, s // tm),
        in_specs=[pl.BlockSpec((1, tm, d), lambda bi, si: (bi, si, 0)),
                  pl.BlockSpec((1, N_MOD, d), lambda bi, si: (bi, 0, 0)),
                  const((1, d)),
                  const((d, 6 * HP)),
                  const((d, LANES)), const((d, LANES)), const((1, LANES)),
                  const((3 * LANES, HP)), const((3 * LANES, HP)),
                  const((1, HP)), const((1, HP)), const((1, HP))],
        out_specs=[out_spec] * 6,
        out_shape=[out_sds] * 6,
        scratch_shapes=[pltpu.VMEM((1, LANES), F32)],
        compiler_params=_cparams("arbitrary", "arbitrary"),
        name="inproj",
    )(x, mod3, g_mix, w_all, wf_hi, wf_lo, b_f, pq, pk, cq, ck, cv)


def _head_norm(o, g):
    lane = lax.broadcasted_iota(jnp.int32, o.shape, 1)
    o = jnp.where(lane < HEAD_DIM, o, 0.0)
    ms = jnp.sum(o * o, axis=-1, keepdims=True) * (1.0 / HEAD_DIM)
    return o * lax.rsqrt(ms + EPS) * g


def _fox_kernel(q_ref, k_ref, v_ref, g_ref, o_ref, m_ref, acc_ref, *, tq, tk):
    s_len = q_ref.shape[1]
    n_diag = tq // tk
    lane1 = lax.broadcasted_iota(jnp.int32, (1, LANES), 1)
    head_lanes = lane1 < HEAD_DIM
    row = lax.broadcasted_iota(jnp.int32, (tq, tk), 0)
    col = lax.broadcasted_iota(jnp.int32, (tq, tk), 1)

    def knorm(i, mx):
        kb = k_ref[0, pl.ds(pl.multiple_of(i * tk, tk), tk), :].astype(F32)
        kb = jnp.where(head_lanes, kb, 0.0)
        n2 = jnp.sum(kb * kb, axis=-1, keepdims=True)
        return jnp.maximum(mx, jnp.max(n2, axis=0, keepdims=True))

    kmax = jnp.sqrt(lax.fori_loop(0, s_len // tk, knorm, jnp.zeros((1, 1), F32)))

    def q_tile(qi, carry):
        q0 = pl.multiple_of(qi * tq, tq)
        q = q_ref[0, pl.ds(q0, tq), :]
        qf = q.astype(F32)
        qn = jnp.sqrt(jnp.sum(jnp.where(head_lanes, qf * qf, 0.0), axis=-1, keepdims=True))
        f_t = jnp.sum(jnp.where((lane1 >= LANE_F_Q) & (lane1 < LANE_F_Q + 3), qf, 0.0),
                      axis=-1, keepdims=True)
        reach = qn * kmax + f_t
        m_ref[...] = jnp.full_like(m_ref, NEG_INF)
        acc_ref[...] = jnp.zeros_like(acc_ref)

        def step(start, mask, r0=0):
            k = k_ref[0, pl.ds(start, tk), :]
            v = v_ref[0, pl.ds(start, tk), :]
            s = _dot_nt(q[r0:], k)
            if mask is not None:
                s = jnp.where(mask[r0:], s, NEG_INF)
            m_prev = m_ref[r0:, :]
            m_new = jnp.maximum(m_prev, jnp.max(s, axis=-1, keepdims=True))
            p = jnp.exp(s - jnp.concatenate([m_new] * (tk // LANES), axis=1))
            acc_ref[r0:, :] = jnp.exp(m_prev - m_new) * acc_ref[r0:, :] + _dot(p.astype(BF16), v)
            m_ref[r0:, :] = m_new

        def block_matters(kb):
            last = k_ref[0, pl.ds(pl.multiple_of(kb * tk + tk - 16, 16), 16), :][15:16, :].astype(F32)
            neg_f_last = jnp.sum(jnp.where((lane1 >= LANE_ONE_Q) & (lane1 < LANE_ONE_Q + 3), last, 0.0),
                                 axis=-1, keepdims=True)
            return jnp.max(reach + neg_f_last - m_ref[:, 0:1]) > EXP_UNDERFLOW

        for d in range(n_diag):
            step(pl.multiple_of(q0 + d * tk, tk), col + d * tk <= row, d * tk)

        def cond(st):
            return st[1]

        def body(st):
            kb = st[0]
            step(pl.multiple_of(kb * tk, tk), None)
            nxt = kb - 1
            return nxt, (nxt >= 0) & block_matters(jnp.maximum(nxt, 0))

        kb0 = qi * n_diag - 1
        lax.while_loop(cond, body, (kb0, (kb0 >= 0) & block_matters(jnp.maximum(kb0, 0))))

        acc = acc_ref[...]
        lane = lax.broadcasted_iota(jnp.int32, acc.shape, 1)
        l = jnp.sum(jnp.where(lane == LANE_ONE_V, acc, 0.0), axis=-1, keepdims=True)
        o_ref[0, pl.ds(q0, tq), :] = _head_norm(acc / l, g_ref[...]).astype(o_ref.dtype)
        return carry

    lax.fori_loop(0, s_len // tq, q_tile, 0)


def _attn_call(body, name, q, k, v, g, tq):
    b, s, _ = q.shape
    spec = pl.BlockSpec((1, s, LANES), lambda bi, hi: (bi, 0, hi))
    return pl.pallas_call(
        body,
        grid=(b, N_HEADS),
        in_specs=[spec, spec, spec, pl.BlockSpec((1, LANES), lambda bi, hi: (0, 0))],
        out_specs=spec,
        out_shape=jax.ShapeDtypeStruct((b, s, HP), BF16),
        scratch_shapes=[pltpu.VMEM((tq, LANES), F32), pltpu.VMEM((tq, LANES), F32)],
        compiler_params=_cparams("arbitrary", "arbitrary"),
        name=name,
    )(q, k, v, g)


def _fox(q, k, v, g, tq, tk):
    return _attn_call(functools.partial(_fox_kernel, tq=tq, tk=tk), "fox", q, k, v, g, tq)


def _sb_kernel(q_ref, k_ref, v_ref, g_ref, o_ref, c_ref, acc_ref, *, tq, tk):
    s_len = q_ref.shape[1]
    n_diag = tq // tk
    jrow = lax.broadcasted_iota(jnp.int32, (tk, tk), 0)
    scol = lax.broadcasted_iota(jnp.int32, (tk, tk), 1)
    later = jnp.where(jrow > scol, 1.0, 0.0).astype(BF16)
    row = lax.broadcasted_iota(jnp.int32, (tq, tk), 0)
    col = lax.broadcasted_iota(jnp.int32, (tq, tk), 1)

    def q_tile(qi, carry):
        q0 = pl.multiple_of(qi * tq, tq)
        q = q_ref[0, pl.ds(q0, tq), :]
        c_ref[...] = jnp.zeros_like(c_ref)
        acc_ref[...] = jnp.zeros_like(acc_ref)

        def step(start, mask, r0=0):
            k = k_ref[0, pl.ds(start, tk), :]
            v = v_ref[0, pl.ds(start, tk), :]
            z = _dot_nt(q[r0:], k)
            sp = _softplus(z)
            spm = sp if mask is None else jnp.where(mask[r0:], sp, 0.0)
            hi, lo = _split2(spm)
            c_prev = c_ref[r0:, :]
            rest = _dot(hi, later) + _dot(lo, later) + jnp.concatenate([c_prev] * (tk // LANES), axis=1)
            a = jnp.exp(z - sp - rest)
            if mask is not None:
                a = jnp.where(mask[r0:], a, 0.0)
            acc_ref[r0:, :] += _dot(a.astype(BF16), v)
            c_ref[r0:, :] = c_prev + jnp.sum(spm, axis=-1, keepdims=True)

        def more_matters():
            return jnp.min(c_ref[:, 0:1]) < -EXP_UNDERFLOW

        for d in reversed(range(n_diag)):
            step(pl.multiple_of(q0 + d * tk, tk), col + d * tk < row, d * tk)

        def cond(st):
            return st[1]

        def body(st):
            kb = st[0]
            step(pl.multiple_of(kb * tk, tk), None)
            nxt = kb - 1
            return nxt, (nxt >= 0) & more_matters()

        kb0 = qi * n_diag - 1
        lax.while_loop(cond, body, (kb0, (kb0 >= 0) & more_matters()))
        o_ref[0, pl.ds(q0, tq), :] = _head_norm(acc_ref[...], g_ref[...]).astype(o_ref.dtype)
        return carry

    lax.fori_loop(0, s_len // tq, q_tile, 0)


def _sb(q, k, v, g, tq, tk):
    return _attn_call(functools.partial(_sb_kernel, tq=tq, tk=tk), "sb", q, k, v, g, tq)


def _outproj_kernel(x_ref, fo_ref, so_ref, mod_ref, g_ref, wo_ref, wch_ref, wcl_ref,
                    x1_ref, h2t_ref, sc_ref):
    gt1 = mod_ref[0, 2:3, :]
    sh2 = mod_ref[0, 3:4, :]
    sc2 = mod_ref[0, 4:5, :]
    mixed = _dot(fo_ref[0], wo_ref[0:HP, :]) + _dot(so_ref[0], wo_ref[HP:2 * HP, :])
    x1 = x_ref[0] + gt1 * mixed
    x1_ref[0] = x1
    inv = lax.rsqrt(jnp.mean(x1 * x1, axis=-1, keepdims=True) + EPS)
    h2 = (x1 * inv) * g_ref[...] * (1.0 + sc2) + sh2
    h2t = h2.T
    th, tl = _split2(h2t)
    h2t_ref[...] = th
    sc_ref[...] = _dot(wch_ref[...], th) + _dot(wch_ref[...], tl) + _dot(wcl_ref[...], th)


def _outproj(x, fo, so, mod3, g_ffn, w_out_p, wc_hi, wc_lo, tm):
    b, s, d = x.shape
    t = b * s
    nsc = wc_hi.shape[0]
    const = lambda shape: pl.BlockSpec(shape, lambda bi, si: (0,) * len(shape),
                                       pipeline_mode=pl.Buffered(1))
    spb = s // tm
    return pl.pallas_call(
        _outproj_kernel,
        grid=(b, spb),
        in_specs=[pl.BlockSpec((1, tm, d), lambda bi, si: (bi, si, 0)),
                  pl.BlockSpec((1, tm, HP), lambda bi, si: (bi, si, 0)),
                  pl.BlockSpec((1, tm, HP), lambda bi, si: (bi, si, 0)),
                  pl.BlockSpec((1, N_MOD, d), lambda bi, si: (bi, 0, 0)),
                  const((1, d)),
                  const((2 * HP, d)),
                  const((nsc, d)), const((nsc, d))],
        out_specs=[pl.BlockSpec((1, tm, d), lambda bi, si: (bi, si, 0)),
                   pl.BlockSpec((d, tm), lambda bi, si: (0, bi * spb + si)),
                   pl.BlockSpec((nsc, tm), lambda bi, si: (0, bi * spb + si))],
        out_shape=[jax.ShapeDtypeStruct((b, s, d), F32),
                   jax.ShapeDtypeStruct((d, t), BF16),
                   jax.ShapeDtypeStruct((nsc, t), F32)],
        compiler_params=_cparams("arbitrary", "arbitrary"),
        name="outproj",
    )(x, fo, so, mod3, g_ffn, w_out_p, wc_hi, wc_lo)


def _top16(s, exact):
    n, t = s.shape
    idx = lax.broadcasted_iota(jnp.int32, (n, t), 0).astype(F32)
    krow = lax.broadcasted_iota(jnp.int32, (PEER_TOPK, t), 0)
    rank = jnp.full((n, t), float(PEER_TOPK), F32)
    vals = jnp.zeros((PEER_TOPK, t), F32)
    for k in range(PEER_TOPK):
        m = jnp.max(s, axis=0, keepdims=True)
        hit = s == m
        if exact:
            hit = idx == jnp.min(jnp.where(hit, idx, float(n)), axis=0, keepdims=True)
        rank = jnp.where(hit, float(k), rank)
        s = jnp.where(hit, NEG_INF, s)
        vals = jnp.where(krow == k, m, vals)
    return vals, rank


SUBLANES = 8
N_CAND = PEER_TOPK + (SUBLANES - 1) * SUBLANES + SUBLANES


def _cand_flat_pos():
    r = lax.broadcasted_iota(jnp.int32, (N_CAND, LANES), 0)
    mid_lo, hi_lo = PEER_TOPK, N_CAND - SUBLANES
    sub_shift = SUBLANES.bit_length() - 1
    mid = (jnp.right_shift(r - mid_lo, sub_shift) + 1) * PEER_TOPK + jnp.bitwise_and(r - mid_lo, SUBLANES - 1)
    hi = (r - hi_lo + SUBLANES) * PEER_TOPK
    return jnp.where(r < mid_lo, r, jnp.where(r < hi_lo, mid, hi)).astype(F32)


def _pair_top16(v1, v2, r1, exact):
    cand = jnp.concatenate(
        [v1[0:1, :] + v2]
        + [v1[a:a + 1, :] + v2[0:SUBLANES, :] for a in range(1, SUBLANES)]
        + [v1[SUBLANES:, :] + v2[0:1, :]], axis=0)
    pos = _cand_flat_pos() if exact else None
    c0 = cand[0:1, :]
    zsum = jnp.zeros((1, LANES), F32)
    for k in range(PEER_TOPK):
        m = jnp.max(cand, axis=0, keepdims=True)
        hit = cand == m
        if exact:
            hit = pos == jnp.min(jnp.where(hit, pos, float(PEER_TOPK * PEER_TOPK)), axis=0, keepdims=True)
        cand = jnp.where(hit, NEG_INF, cand)
        zsum = zsum + jnp.exp(m - c0)
    taken = jnp.where(cand == NEG_INF, 1.0, 0.0)
    n_taken = jnp.sum(taken, axis=0, keepdims=True)
    lrow = jnp.zeros(r1.shape, F32)
    for a in range(PEER_TOPK):
        if a == 0:
            cnt_a = jnp.sum(taken[0:PEER_TOPK, :], axis=0, keepdims=True)
        elif a < SUBLANES:
            lo = PEER_TOPK + SUBLANES * (a - 1)
            cnt_a = jnp.sum(taken[lo:lo + SUBLANES, :], axis=0, keepdims=True)
        else:
            lo = N_CAND - SUBLANES + a - SUBLANES
            cnt_a = taken[lo:lo + 1, :]
        lrow = jnp.where(r1 == float(a), cnt_a, lrow)
    return lrow, zsum, n_taken


def _topk_kernel(sc_ref, r2_ref, e2_ref, l1_ref, e1_ref):
    n = PEER_N_KEYS
    tl = sc_ref.shape[1]
    ngrp = tl // LANES

    def body(it, carry):
        hd = lax.div(it, ngrp)
        gi = lax.rem(it, ngrp)
        lanes = pl.ds(pl.multiple_of(gi * LANES, LANES), LANES)
        s1 = sc_ref[pl.ds(pl.multiple_of(hd * 2 * n, n), n), lanes]
        s2 = sc_ref[pl.ds(pl.multiple_of(hd * 2 * n + n, n), n), lanes]

        def run(exact):
            v1, r1 = _top16(s1, exact)
            v2, r2 = _top16(s2, exact)
            lrow, zsum, n_taken = _pair_top16(v1, v2, r1, exact)
            r2_ref[hd, :, lanes] = pltpu.bitcast(r2.astype(BF16), jnp.uint32)
            e2_ref[hd, :, lanes] = pltpu.bitcast(jnp.exp(s2 - v2[0:1, :]).astype(BF16), jnp.uint32)
            l1_ref[hd, :, lanes] = _dup_bf16_words(lrow)
            e1_ref[hd, :, lanes] = _dup_bf16_words(jnp.exp(s1 - v1[0:1, :]) / zsum)
            k = float(PEER_TOPK)
            picked = lambda r: jnp.sum(jnp.where(r < k, 1.0, 0.0), axis=0, keepdims=True)
            return jnp.abs(picked(r1) - k) + jnp.abs(picked(r2) - k) + jnp.abs(n_taken - k)

        excess = run(exact=False)

        @pl.when(jnp.max(excess) > 0.0)
        def _():
            run(exact=True)

        return carry

    lax.fori_loop(0, PEER_HEADS * ngrp, body, 0)


def _topk(scores_t, tl):
    nsc, t = scores_t.shape
    spec = lambda rows: pl.BlockSpec((PEER_HEADS, rows, tl), lambda i: (0, 0, i))
    sds = lambda rows: jax.ShapeDtypeStruct((PEER_HEADS, rows, t), jnp.uint32)
    half, full = PEER_N_KEYS // 2, PEER_N_KEYS
    return pl.pallas_call(
        _topk_kernel,
        grid=(t // tl,),
        in_specs=[pl.BlockSpec((nsc, tl), lambda i: (0, i))],
        out_specs=[spec(half), spec(half), spec(full), spec(full)],
        out_shape=[sds(half), sds(half), sds(full), sds(full)],
        compiler_params=_cparams("arbitrary"),
        name="topk",
    )(scores_t)


def _gelu(x):
    return 0.5 * x * (1.0 + lax.erf(x * (2.0 ** -0.5)))


def _dup_bf16_words(x):
    bits = pltpu.bitcast(x.astype(BF16).astype(F32), jnp.uint32)
    return bits | (bits >> 16)


def _bf16_rows(words, n):
    return pltpu.bitcast(jnp.broadcast_to(words, (n // 2, words.shape[1])), BF16)


def _peer_kernel(h2t_ref, dn_ref, up_ref, r2_ref, e2_ref, l1_ref, e1_ref, x1_ref, mod_ref, gf_ref,
                 o_ref, acc_ref, st_ref, pw_ref, *, rows_per_chunk):
    j = pl.program_id(2)
    n = PEER_N_KEYS

    @pl.when(j == 0)
    def _():
        acc_ref[...] = jnp.zeros_like(acc_ref)

    st_ref[...] = _dot(dn_ref[...], h2t_ref[...])
    zero = jnp.zeros((n, LANES), BF16)
    pair = 2
    for g in range(h2t_ref.shape[1] // LANES):
        lanes = slice(g * LANES, (g + 1) * LANES)
        for i1 in range(0, rows_per_chunk, pair):
            w = [None] * pair
            for hd in range(PEER_HEADS):
                r2 = pltpu.bitcast(r2_ref[hd, :, lanes], BF16)
                e2 = pltpu.bitcast(e2_ref[hd, :, lanes], BF16)
                for u in range(pair):
                    cnt = _bf16_rows(l1_ref[hd, i1 + u:i1 + u + 1, lanes], n)
                    e1 = _bf16_rows(e1_ref[hd, i1 + u:i1 + u + 1, lanes], n)
                    wh = jnp.where(r2 < cnt, e2, zero) * e1
                    w[u] = wh if w[u] is None else w[u] + wh
            for u in range(pair):
                rows = slice((i1 + u) * n, (i1 + u + 1) * n)
                pw_ref[rows, lanes] = _gelu(st_ref[rows, lanes]).astype(BF16) * w[u]
    acc_ref[...] += _dot(up_ref[...], pw_ref[...])

    @pl.when(j == pl.num_programs(2) - 1)
    def _():
        gt2 = mod_ref[0, 5:6, :]
        x2 = x1_ref[0] + gt2 * acc_ref[...].T
        inv = lax.rsqrt(jnp.mean(x2 * x2, axis=-1, keepdims=True) + EPS)
        o_ref[0] = (x2 * inv) * gf_ref[...]


def _peer(h2t, down, up_t, r2, e2, l1, e1, x1, mod3, g_final, tt, ec):
    b, s, d = x1.shape
    ne = down.shape[0]
    rpc = ec // PEER_N_KEYS
    spb = s // tt
    tok = lambda bi, si, j: (0, 0, bi * spb + si)
    gate_tab = pl.BlockSpec((PEER_HEADS, rpc, tt), lambda bi, si, j: (0, j, bi * spb + si))
    return pl.pallas_call(
        functools.partial(_peer_kernel, rows_per_chunk=rpc),
        grid=(b, spb, ne // ec),
        in_specs=[pl.BlockSpec((d, tt), lambda bi, si, j: (0, bi * spb + si)),
                  pl.BlockSpec((ec, d), lambda bi, si, j: (j, 0)),
                  pl.BlockSpec((d, ec), lambda bi, si, j: (0, j)),
                  pl.BlockSpec((PEER_HEADS, PEER_N_KEYS // 2, tt), tok),
                  pl.BlockSpec((PEER_HEADS, PEER_N_KEYS // 2, tt), tok),
                  gate_tab, gate_tab,
                  pl.BlockSpec((1, tt, d), lambda bi, si, j: (bi, si, 0)),
                  pl.BlockSpec((1, N_MOD, d), lambda bi, si, j: (bi, 0, 0)),
                  pl.BlockSpec((1, d), lambda bi, si, j: (0, 0))],
        out_specs=pl.BlockSpec((1, tt, d), lambda bi, si, j: (bi, si, 0)),
        out_shape=jax.ShapeDtypeStruct((b, s, d), F32),
        scratch_shapes=[pltpu.VMEM((d, tt), F32), pltpu.VMEM((ec, tt), F32), pltpu.VMEM((ec, tt), BF16)],
        compiler_params=_cparams("arbitrary", "arbitrary", "arbitrary"),
        name="peer",
    )(h2t, down, up_t, r2, e2, l1, e1, x1, mod3, g_final)


def _pad_heads(w):
    d = w.shape[0]
    w = w.reshape(d, N_HEADS, HEAD_DIM)
    return jnp.pad(w, ((0, 0), (0, 0), (0, LANES - HEAD_DIM))).reshape(d, HP)


def _bias_constants():
    pq = np.zeros((3 * LANES, HP), np.float32)
    pk = np.zeros((3 * LANES, HP), np.float32)
    cq = np.zeros((1, HP), np.float32)
    ck = np.zeros((1, HP), np.float32)
    cv = np.zeros((1, HP), np.float32)
    for h in range(N_HEADS):
        for part in range(3):
            pq[part * LANES + h, h * LANES + LANE_F_Q + part] = 1.0
            pk[part * LANES + h, h * LANES + LANE_ONE_Q + part] = -1.0
            cq[0, h * LANES + LANE_ONE_Q + part] = 1.0
            ck[0, h * LANES + LANE_F_Q + part] = 1.0
        cv[0, h * LANES + LANE_ONE_V] = 1.0
    return (jnp.asarray(pq, BF16), jnp.asarray(pk, BF16), jnp.asarray(cq), jnp.asarray(ck), jnp.asarray(cv))


def _tile(n, pref):
    return pref if n % pref == 0 else n


def kernel(x, c, w_ada, b_ada, g_norm_mix, w_in, b_forget, g_out_fox, g_out_sb, w_out, g_norm_ffn,
           w_query, sub_keys, expert_down, expert_up, g_final):
    b, s, d = x.shape
    assert w_ada.shape[0] == 1, "single-layer block: the final RMSNorm is fused into the PEER kernel"
    fw = N_HEADS * HEAD_DIM
    scale = HEAD_DIM ** -0.5
    pq, pk, cq, ck, cv = _bias_constants()
    tm = _tile(s, 512)
    c_pad = jnp.pad(c, ((0, -b % 8), (0, 0)))

    for l in range(1):
        mod = _ada(c_pad, w_ada[l], b_ada[l][None, :])[:b]
        mod3 = mod.reshape(b, N_MOD, d)

        wi = w_in[l]
        o1 = 3 * fw
        o2 = o1 + N_HEADS
        w_all = jnp.concatenate([
            _pad_heads(wi[:, 0:fw] * scale), _pad_heads(wi[:, fw:2 * fw]), _pad_heads(wi[:, 2 * fw:o1]),
            _pad_heads(wi[:, o2:o2 + fw] * scale), _pad_heads(wi[:, o2 + fw:o2 + 2 * fw]),
            _pad_heads(wi[:, o2 + 2 * fw:o2 + 3 * fw])], axis=1).astype(BF16)
        wf = jnp.pad(wi[:, o1:o2], ((0, 0), (0, LANES - N_HEADS)))
        wf_hi = wf.astype(BF16)
        wf_lo = (wf - wf_hi.astype(F32)).astype(BF16)
        b_f = jnp.pad(b_forget[l], (0, LANES - N_HEADS))[None, :]

        fq, fk, fv, sq, sk, sv = _inproj(x, mod3, g_norm_mix[l][None, :], w_all, wf_hi, wf_lo, b_f,
                                         pq, pk, cq, ck, cv, tm)

        g_fox = jnp.pad(g_out_fox[l], (0, LANES - HEAD_DIM))[None, :]
        g_sb = jnp.pad(g_out_sb[l], (0, LANES - HEAD_DIM))[None, :]
        fo = _fox(fq, fk, fv, g_fox, _tile(s, 512), _tile(s, 256))
        so = _sb(sq, sk, sv, g_sb, _tile(s, 512), _tile(s, 256))

        wo = w_out[l]
        w_out_p = jnp.pad(wo.reshape(2 * N_HEADS, HEAD_DIM, d),
                          ((0, 0), (0, LANES - HEAD_DIM), (0, 0))).reshape(2 * HP, d).astype(BF16)
        wc = _wc(sub_keys[l].reshape(PEER_HEADS * 2, PEER_N_KEYS, LANES), w_query[l])
        wc_hi = wc.astype(BF16)
        wc_lo = (wc - wc_hi.astype(F32)).astype(BF16)
        x1, h2t, scores_t = _outproj(x, fo, so, mod3, g_norm_ffn[l][None, :], w_out_p, wc_hi, wc_lo, tm)

        r2, e2, l1, e1 = _topk(scores_t, _tile(b * s, 512))
        x = _peer(h2t, expert_down[l].astype(BF16), expert_up[l].T.astype(BF16), r2, e2, l1, e1,
                  x1, mod3, g_final[None, :], _tile(s, 512), 1024)
    return x
```

```python
import functools

import numpy as np
import jax
import jax.numpy as jnp
from jax import lax
from jax.experimental import pallas as pl
from jax.experimental.pallas import tpu as pltpu

F32 = jnp.float32
BF16 = jnp.bfloat16

HEAD_DIM = 64
N_HEADS = 8
PEER_HEADS = 8
PEER_N_KEYS = 128
PEER_TOPK = 16
N_MOD = 6
EPS = 1e-6

LANES = 128
VMEM_LIMIT = 56 * 1024 * 1024

HP = N_HEADS * LANES
LANE_ONE_Q = HEAD_DIM
LANE_F_Q = HEAD_DIM + 3
LANE_ONE_V = HEAD_DIM
NEG_INF = float("-inf")
LOG2E = 1.4426950408889634
EXP2_UNDERFLOW = -153.0


def _cparams(*sem, flags=None):
    return pltpu.CompilerParams(dimension_semantics=sem, vmem_limit_bytes=VMEM_LIMIT, flags=flags)


def _dot(a, b):
    return jnp.dot(a, b, preferred_element_type=F32)


def _dot_nt(a, b):
    return lax.dot_general(a, b, (((1,), (1,)), ((), ())), preferred_element_type=F32)


def _split2(x):
    hi = x.astype(BF16)
    lo = (x - hi.astype(F32)).astype(BF16)
    return hi, lo


def _split3(x):
    hi = x.astype(BF16)
    r = x - hi.astype(F32)
    mid = r.astype(BF16)
    lo = (r - mid.astype(F32)).astype(BF16)
    return hi, mid, lo


def _softplus2(z):
    return jnp.maximum(z, 0.0) + jnp.log2(1.0 + jnp.exp2(-jnp.abs(z)))


def _ada_kernel(c_ref, w_ref, b_ref, o_ref):
    c = c_ref[...]
    a = c / (1.0 + jnp.exp(-c))
    ah, al = _split2(a)
    wh, wl = _split2(w_ref[...])
    o_ref[...] = _dot(ah, wh) + _dot(ah, wl) + _dot(al, wh) + b_ref[...]


def _ada(c_pad, w_ada, b_ada):
    rows, d = c_pad.shape
    n = w_ada.shape[1]
    bn = 1024
    return pl.pallas_call(
        _ada_kernel,
        grid=(n // bn,),
        in_specs=[pl.BlockSpec((rows, d), lambda j: (0, 0)),
                  pl.BlockSpec((d, bn), lambda j: (0, j)),
                  pl.BlockSpec((1, bn), lambda j: (0, j))],
        out_specs=pl.BlockSpec((rows, bn), lambda j: (0, j)),
        out_shape=jax.ShapeDtypeStruct((rows, n), F32),
        compiler_params=_cparams("arbitrary"),
        name="ada",
    )(c_pad, w_ada, b_ada)


def _wc_kernel(sk_ref, wq_ref, o_ref):
    sh, sl = _split2(sk_ref[0])
    wh, wl = _split2(wq_ref[...])
    o_ref[...] = _dot_nt(sh, wh) + _dot_nt(sh, wl) + _dot_nt(sl, wh)


def _wc(sub_keys, w_query):
    d = w_query.shape[0]
    nhp = sub_keys.shape[0]
    return pl.pallas_call(
        _wc_kernel,
        grid=(nhp,),
        in_specs=[pl.BlockSpec((1, PEER_N_KEYS, LANES), lambda i: (i, 0, 0)),
                  pl.BlockSpec((d, LANES), lambda i: (0, i))],
        out_specs=pl.BlockSpec((PEER_N_KEYS, d), lambda i: (i, 0)),
        out_shape=jax.ShapeDtypeStruct((nhp * PEER_N_KEYS, d), F32),
        compiler_params=_cparams("arbitrary"),
        name="wc",
    )(sub_keys, w_query)


def _inproj_kernel(x_ref, mod_ref, g_ref, w_ref, wfh_ref, wfl_ref, bf_ref, pq_ref, pk_ref,
                   cq_ref, ck_ref, cv_ref,
                   fq_ref, fk_ref, fv_ref, sq_ref, sk_ref, sv_ref, carry_ref):
    tm = x_ref.shape[1]

    @pl.when(pl.program_id(1) == 0)
    def _():
        carry_ref[...] = jnp.zeros_like(carry_ref)

    x = x_ref[0]
    inv = lax.rsqrt(jnp.mean(x * x, axis=-1, keepdims=True) + EPS)
    sh1 = mod_ref[0, 0:1, :]
    sc1 = mod_ref[0, 1:2, :]
    h = (x * inv) * g_ref[...] * (1.0 + sc1) + sh1
    hh, hl = _split2(h)

    fl = _dot(hh, wfh_ref[...]) + _dot(hh, wfl_ref[...]) + _dot(hl, wfh_ref[...]) + bf_ref[...]
    lf = jnp.minimum(fl, 0.0) - jnp.log(1.0 + jnp.exp(-jnp.abs(fl)))
    row = lax.broadcasted_iota(jnp.int32, (tm, tm), 0)
    col = lax.broadcasted_iota(jnp.int32, (tm, tm), 1)
    tri = jnp.where(col <= row, 1.0, 0.0).astype(BF16)
    l0, l1, l2 = _split3(lf)
    fcum = _dot(tri, l0) + _dot(tri, l1) + _dot(tri, l2) + carry_ref[...]
    carry_ref[...] = fcum[tm - 1:tm, :]
    f0, f1, f2 = _split3(fcum * LOG2E)
    fcat = jnp.concatenate([f0, f1, f2], axis=1)

    fq_ref[0] = (_dot(hh, w_ref[:, 0 * HP:1 * HP]) + _dot(fcat, pq_ref[...]) + cq_ref[...]).astype(BF16)
    fk_ref[0] = (_dot(hh, w_ref[:, 1 * HP:2 * HP]) + _dot(fcat, pk_ref[...]) + ck_ref[...]).astype(BF16)
    fv_ref[0] = (_dot(hh, w_ref[:, 2 * HP:3 * HP]) + cv_ref[...]).astype(BF16)
    sq_ref[0] = _dot(hh, w_ref[:, 3 * HP:4 * HP]).astype(BF16)
    sk_ref[0] = _dot(hh, w_ref[:, 4 * HP:5 * HP]).astype(BF16)
    sv_ref[0] = _dot(hh, w_ref[:, 5 * HP:6 * HP]).astype(BF16)


def _inproj(x, mod3, g_mix, w_all, wf_hi, wf_lo, b_f, pq, pk, cq, ck, cv, tm):
    b, s, d = x.shape
    const = lambda shape: pl.BlockSpec(shape, lambda bi, si: (0,) * len(shape),
                                       pipeline_mode=pl.Buffered(1))
    out_spec = pl.BlockSpec((1, tm, HP), lambda bi, si: (bi, si, 0))
    out_sds = jax.ShapeDtypeStruct((b, s, HP), BF16)
    return pl.pallas_call(
        _inproj_kernel,
        grid=(b, s // tm),
        in_specs=[pl.BlockSpec((1, tm, d), lambda bi, si: (bi, si, 0)),
                  pl.BlockSpec((1, N_MOD, d), lambda bi, si: (bi, 0, 0)),
                  const((1, d)),
                  const((d, 6 * HP)),
                  const((d, LANES)), const((d, LANES)), const((1, LANES)),
                  const((3 * LANES, HP)), const((3 * LANES, HP)),
                  const((1, HP)), const((1, HP)), const((1, HP))],
        out_specs=[out_spec] * 6,
        out_shape=[out_sds] * 6,
        scratch_shapes=[pltpu.VMEM((1, LANES), F32)],
        compiler_params=_cparams("arbitrary", "arbitrary"),
        name="inproj",
    )(x, mod3, g_mix, w_all, wf_hi, wf_lo, b_f, pq, pk, cq, ck, cv)


def _head_norm(o, g):
    lane = lax.broadcasted_iota(jnp.int32, o.shape, 1)
    o = jnp.where(lane < HEAD_DIM, o, 0.0)
    ms = jnp.sum(o * o, axis=-1, keepdims=True) * (1.0 / HEAD_DIM)
    return o * lax.rsqrt(ms + EPS) * g


def _fox_kernel(q_ref, k_ref, v_ref, g_ref, o_ref, m_ref, acc_ref, *, tq, tk):
    s_len = q_ref.shape[1]
    n_diag = tq // tk
    lane1 = lax.broadcasted_iota(jnp.int32, (1, LANES), 1)
    head_lanes = lane1 < HEAD_DIM
    row = lax.broadcasted_iota(jnp.int32, (tq, tk), 0)
    col = lax.broadcasted_iota(jnp.int32, (tq, tk), 1)

    def knorm(i, mx):
        kb = k_ref[0, pl.ds(pl.multiple_of(i * tk, tk), tk), :].astype(F32)
        kb = jnp.where(head_lanes, kb, 0.0)
        n2 = jnp.sum(kb * kb, axis=-1, keepdims=True)
        return jnp.maximum(mx, jnp.max(n2, axis=0, keepdims=True))

    kmax = jnp.sqrt(lax.fori_loop(0, s_len // tk, knorm, jnp.zeros((1, 1), F32)))

    def q_tile(qi, carry):
        q0 = pl.multiple_of(qi * tq, tq)
        q = q_ref[0, pl.ds(q0, tq), :]
        qf = q.astype(F32)
        qn = jnp.sqrt(jnp.sum(jnp.where(head_lanes, qf * qf, 0.0), axis=-1, keepdims=True))
        f_t = jnp.sum(jnp.where((lane1 >= LANE_F_Q) & (lane1 < LANE_F_Q + 3), qf, 0.0),
                      axis=-1, keepdims=True)
        reach = qn * kmax + f_t
        m_ref[...] = jnp.full_like(m_ref, NEG_INF)
        acc_ref[...] = jnp.zeros_like(acc_ref)

        def step(start, mask, r0=0):
            k = k_ref[0, pl.ds(start, tk), :]
            v = v_ref[0, pl.ds(start, tk), :]
            s = _dot_nt(q[r0:], k)
            if mask is not None:
                s = jnp.where(mask[r0:], s, NEG_INF)
            m_prev = m_ref[r0:, :]
            m_new = jnp.maximum(m_prev, jnp.max(s, axis=-1, keepdims=True))
            p = jnp.exp2(s - jnp.concatenate([m_new] * (tk // LANES), axis=1))
            acc_ref[r0:, :] = jnp.exp2(m_prev - m_new) * acc_ref[r0:, :] + _dot(p.astype(BF16), v)
            m_ref[r0:, :] = m_new

        def block_matters(kb):
            last = k_ref[0, pl.ds(pl.multiple_of(kb * tk + tk - 16, 16), 16), :][15:16, :].astype(F32)
            neg_f_last = jnp.sum(jnp.where((lane1 >= LANE_ONE_Q) & (lane1 < LANE_ONE_Q + 3), last, 0.0),
                                 axis=-1, keepdims=True)
            return jnp.max(reach + neg_f_last - m_ref[:, 0:1]) > EXP2_UNDERFLOW

        for d in range(n_diag):
            step(pl.multiple_of(q0 + d * tk, tk), col + d * tk <= row, d * tk)

        def cond(st):
            return st[1]

        def body(st):
            kb = st[0]
            step(pl.multiple_of(kb * tk, tk), None)
            nxt = kb - 1
            return nxt, (nxt >= 0) & block_matters(jnp.maximum(nxt, 0))

        kb0 = qi * n_diag - 1
        lax.while_loop(cond, body, (kb0, (kb0 >= 0) & block_matters(jnp.maximum(kb0, 0))))

        acc = acc_ref[...]
        lane = lax.broadcasted_iota(jnp.int32, acc.shape, 1)
        l = jnp.sum(jnp.where(lane == LANE_ONE_V, acc, 0.0), axis=-1, keepdims=True)
        o_ref[0, pl.ds(q0, tq), :] = _head_norm(acc / l, g_ref[...]).astype(o_ref.dtype)
        return carry

    lax.fori_loop(0, s_len // tq, q_tile, 0)


def _attn_call(body, name, q, k, v, g, tq):
    b, s, _ = q.shape
    spec = pl.BlockSpec((1, s, LANES), lambda bi, hi: (bi, 0, hi))
    return pl.pallas_call(
        body,
        grid=(b, N_HEADS),
        in_specs=[spec, spec, spec, pl.BlockSpec((1, LANES), lambda bi, hi: (0, 0))],
        out_specs=spec,
        out_shape=jax.ShapeDtypeStruct((b, s, HP), BF16),
        scratch_shapes=[pltpu.VMEM((tq, LANES), F32), pltpu.VMEM((tq, LANES), F32)],
        compiler_params=_cparams("arbitrary", "arbitrary"),
        name=name,
    )(q, k, v, g)


def _fox(q, k, v, g, tq, tk):
    return _attn_call(functools.partial(_fox_kernel, tq=tq, tk=tk), "fox", q, k, v, g, tq)


def _sb_kernel(q_ref, k_ref, v_ref, g_ref, o_ref, c_ref, acc_ref, *, tq, tk):
    s_len = q_ref.shape[1]
    n_diag = tq // tk
    jrow = lax.broadcasted_iota(jnp.int32, (tk, tk), 0)
    scol = lax.broadcasted_iota(jnp.int32, (tk, tk), 1)
    later = jnp.where(jrow > scol, 1.0, 0.0).astype(BF16)
    row = lax.broadcasted_iota(jnp.int32, (tq, tk), 0)
    col = lax.broadcasted_iota(jnp.int32, (tq, tk), 1)

    def q_tile(qi, carry):
        q0 = pl.multiple_of(qi * tq, tq)
        q = q_ref[0, pl.ds(q0, tq), :]
        c_ref[...] = jnp.zeros_like(c_ref)
        acc_ref[...] = jnp.zeros_like(acc_ref)

        def step(start, mask, r0=0):
            k = k_ref[0, pl.ds(start, tk), :]
            v = v_ref[0, pl.ds(start, tk), :]
            z = _dot_nt(q[r0:], k)
            sp = _softplus2(z)
            spm = sp if mask is None else jnp.where(mask[r0:], sp, 0.0)
            hi, lo = _split2(spm)
            c_prev = c_ref[r0:, :]
            rest = _dot(hi, later) + _dot(lo, later) + jnp.concatenate([c_prev] * (tk // LANES), axis=1)
            a = jnp.exp2(z - sp - rest)
            if mask is not None:
                a = jnp.where(mask[r0:], a, 0.0)
            acc_ref[r0:, :] += _dot(a.astype(BF16), v)
            c_ref[r0:, :] = c_prev + jnp.sum(spm, axis=-1, keepdims=True)

        def more_matters():
            return jnp.min(c_ref[:, 0:1]) < -EXP2_UNDERFLOW

        for d in reversed(range(n_diag)):
            step(pl.multiple_of(q0 + d * tk, tk), col + d * tk < row, d * tk)

        def cond(st):
            return st[1]

        def body(st):
            kb = st[0]
            step(pl.multiple_of(kb * tk, tk), None)
            nxt = kb - 1
            return nxt, (nxt >= 0) & more_matters()

        kb0 = qi * n_diag - 1
        lax.while_loop(cond, body, (kb0, (kb0 >= 0) & more_matters()))
        o_ref[0, pl.ds(q0, tq), :] = _head_norm(acc_ref[...], g_ref[...]).astype(o_ref.dtype)
        return carry

    lax.fori_loop(0, s_len // tq, q_tile, 0)


def _sb(q, k, v, g, tq, tk):
    return _attn_call(functools.partial(_sb_kernel, tq=tq, tk=tk), "sb", q, k, v, g, tq)


def _outproj_kernel(x_ref, fo_ref, so_ref, mod_ref, g_ref, wo_ref, wch_ref, wcl_ref,
                    x1_ref, h2t_ref, sc_ref):
    gt1 = mod_ref[0, 2:3, :]
    sh2 = mod_ref[0, 3:4, :]
    sc2 = mod_ref[0, 4:5, :]
    mixed = _dot(fo_ref[0], wo_ref[0:HP, :]) + _dot(so_ref[0], wo_ref[HP:2 * HP, :])
    x1 = x_ref[0] + gt1 * mixed
    x1_ref[0] = x1
    inv = lax.rsqrt(jnp.mean(x1 * x1, axis=-1, keepdims=True) + EPS)
    h2 = (x1 * inv) * g_ref[...] * (1.0 + sc2) + sh2
    h2t = h2.T
    th, tl = _split2(h2t)
    h2t_ref[...] = th
    sc_ref[...] = _dot(wch_ref[...], th) + _dot(wch_ref[...], tl) + _dot(wcl_ref[...], th)


def _outproj(x, fo, so, mod3, g_ffn, w_out_p, wc_hi, wc_lo, tm):
    b, s, d = x.shape
    t = b * s
    nsc = wc_hi.shape[0]
    const = lambda shape: pl.BlockSpec(shape, lambda bi, si: (0,) * len(shape),
                                       pipeline_mode=pl.Buffered(1))
    spb = s // tm
    return pl.pallas_call(
        _outproj_kernel,
        grid=(b, spb),
        in_specs=[pl.BlockSpec((1, tm, d), lambda bi, si: (bi, si, 0)),
                  pl.BlockSpec((1, tm, HP), lambda bi, si: (bi, si, 0)),
                  pl.BlockSpec((1, tm, HP), lambda bi, si: (bi, si, 0)),
                  pl.BlockSpec((1, N_MOD, d), lambda bi, si: (bi, 0, 0)),
                  const((1, d)),
                  const((2 * HP, d)),
                  const((nsc, d)), const((nsc, d))],
        out_specs=[pl.BlockSpec((1, tm, d), lambda bi, si: (bi, si, 0)),
                   pl.BlockSpec((d, tm), lambda bi, si: (0, bi * spb + si)),
                   pl.BlockSpec((nsc, tm), lambda bi, si: (0, bi * spb + si))],
        out_shape=[jax.ShapeDtypeStruct((b, s, d), F32),
                   jax.ShapeDtypeStruct((d, t), BF16),
                   jax.ShapeDtypeStruct((nsc, t), F32)],
        compiler_params=_cparams("arbitrary", "arbitrary"),
        name="outproj",
    )(x, fo, so, mod3, g_ffn, w_out_p, wc_hi, wc_lo)


def _top16(s, exact):
    n, t = s.shape
    idx = lax.broadcasted_iota(jnp.int32, (n, t), 0).astype(F32)
    krow = lax.broadcasted_iota(jnp.int32, (PEER_TOPK, t), 0)
    rank = jnp.full((n, t), float(PEER_TOPK), F32)
    vals = jnp.zeros((PEER_TOPK, t), F32)
    for k in range(PEER_TOPK):
        m = jnp.max(s, axis=0, keepdims=True)
        hit = s == m
        if exact:
            hit = idx == jnp.min(jnp.where(hit, idx, float(n)), axis=0, keepdims=True)
        rank = jnp.where(hit, float(k), rank)
        s = jnp.where(hit, NEG_INF, s)
        vals = jnp.where(krow == k, m, vals)
    return vals, rank


SUBLANES = 8
N_CAND = PEER_TOPK + (SUBLANES - 1) * SUBLANES + SUBLANES


def _cand_flat_pos():
    r = lax.broadcasted_iota(jnp.int32, (N_CAND, LANES), 0)
    mid_lo, hi_lo = PEER_TOPK, N_CAND - SUBLANES
    sub_shift = SUBLANES.bit_length() - 1
    mid = (jnp.right_shift(r - mid_lo, sub_shift) + 1) * PEER_TOPK + jnp.bitwise_and(r - mid_lo, SUBLANES - 1)
    hi = (r - hi_lo + SUBLANES) * PEER_TOPK
    return jnp.where(r < mid_lo, r, jnp.where(r < hi_lo, mid, hi)).astype(F32)


def _pair_top16(v1, v2, r1, exact):
    cand = jnp.concatenate(
        [v1[0:1, :] + v2]
        + [v1[a:a + 1, :] + v2[0:SUBLANES, :] for a in range(1, SUBLANES)]
        + [v1[SUBLANES:, :] + v2[0:1, :]], axis=0)
    pos = _cand_flat_pos() if exact else None
    c0 = cand[0:1, :]
    zsum = jnp.zeros((1, LANES), F32)
    for k in range(PEER_TOPK):
        m = jnp.max(cand, axis=0, keepdims=True)
        hit = cand == m
        if exact:
            hit = pos == jnp.min(jnp.where(hit, pos, float(PEER_TOPK * PEER_TOPK)), axis=0, keepdims=True)
        cand = jnp.where(hit, NEG_INF, cand)
        zsum = zsum + jnp.exp(m - c0)
    taken = jnp.where(cand == NEG_INF, 1.0, 0.0)
    n_taken = jnp.sum(taken, axis=0, keepdims=True)
    lrow = jnp.zeros(r1.shape, F32)
    for a in range(PEER_TOPK):
        if a == 0:
            cnt_a = jnp.sum(taken[0:PEER_TOPK, :], axis=0, keepdims=True)
        elif a < SUBLANES:
            lo = PEER_TOPK + SUBLANES * (a - 1)
            cnt_a = jnp.sum(taken[lo:lo + SUBLANES, :], axis=0, keepdims=True)
        else:
            lo = N_CAND - SUBLANES + a - SUBLANES
            cnt_a = taken[lo:lo + 1, :]
        lrow = jnp.where(r1 == float(a), cnt_a, lrow)
    return lrow, zsum, n_taken


def _topk_kernel(sc_ref, r2_ref, e2_ref, l1_ref, e1_ref):
    n = PEER_N_KEYS
    tl = sc_ref.shape[1]
    ngrp = tl // LANES

    def run(item, exact):
        hd = lax.div(item, ngrp)
        gi = lax.rem(item, ngrp)
        lanes = pl.ds(pl.multiple_of(gi * LANES, LANES), LANES)
        s1 = sc_ref[pl.ds(pl.multiple_of(hd * 2 * n, n), n), lanes]
        s2 = sc_ref[pl.ds(pl.multiple_of(hd * 2 * n + n, n), n), lanes]
        v1, r1 = _top16(s1, exact)
        v2, r2 = _top16(s2, exact)
        lrow, zsum, n_taken = _pair_top16(v1, v2, r1, exact)
        r2_ref[hd, :, lanes] = pltpu.bitcast(r2.astype(BF16), jnp.uint32)
        e2_ref[hd, :, lanes] = pltpu.bitcast(jnp.exp(s2 - v2[0:1, :]).astype(BF16), jnp.uint32)
        l1_ref[hd, :, lanes] = _dup_bf16_words(lrow)
        e1_ref[hd, :, lanes] = _dup_bf16_words(jnp.exp(s1 - v1[0:1, :]) / zsum)
        k = float(PEER_TOPK)
        picked = lambda r: jnp.sum(jnp.where(r < k, 1.0, 0.0), axis=0, keepdims=True)
        return jnp.abs(picked(r1) - k) + jnp.abs(picked(r2) - k) + jnp.abs(n_taken - k)

    together = 2 if (PEER_HEADS * ngrp) % 2 == 0 else 1

    def body(it, carry):
        items = [it * together + u for u in range(together)]
        excess = [run(item, exact=False) for item in items]
        for item, ex in zip(items, excess):
            @pl.when(jnp.max(ex) > 0.0)
            def _():
                run(item, exact=True)

        return carry

    lax.fori_loop(0, PEER_HEADS * ngrp // together, body, 0)


def _topk(scores_t, tl):
    nsc, t = scores_t.shape
    spec = lambda rows: pl.BlockSpec((PEER_HEADS, rows, tl), lambda i: (0, 0, i))
    sds = lambda rows: jax.ShapeDtypeStruct((PEER_HEADS, rows, t), jnp.uint32)
    half, full = PEER_N_KEYS // 2, PEER_N_KEYS
    return pl.pallas_call(
        _topk_kernel,
        grid=(t // tl,),
        in_specs=[pl.BlockSpec((nsc, tl), lambda i: (0, i))],
        out_specs=[spec(half), spec(half), spec(full), spec(full)],
        out_shape=[sds(half), sds(half), sds(full), sds(full)],
        compiler_params=_cparams("arbitrary"),
        name="topk",
    )(scores_t)


def _gelu(x):
    return 0.5 * x * (1.0 + lax.erf(x * (2.0 ** -0.5)))


def _dup_bf16_words(x):
    bits = pltpu.bitcast(x.astype(BF16).astype(F32), jnp.uint32)
    return bits | (bits >> 16)


def _bf16_rows(words, n):
    return pltpu.bitcast(jnp.broadcast_to(words, (n // 2, words.shape[1])), BF16)


def _peer_kernel(h2t_ref, dn_ref, up_ref, r2_ref, e2_ref, l1_ref, e1_ref, x1_ref, mod_ref, gf_ref,
                 o_ref, acc_ref, st_ref, pw_ref, *, rows_per_chunk):
    j = pl.program_id(2)
    n = PEER_N_KEYS

    @pl.when(j == 0)
    def _():
        acc_ref[...] = jnp.zeros_like(acc_ref)

    st_ref[...] = _dot(dn_ref[...], h2t_ref[...])
    zero = jnp.zeros((n, LANES), BF16)
    pair = 2
    for g in range(h2t_ref.shape[1] // LANES):
        lanes = slice(g * LANES, (g + 1) * LANES)
        for i1 in range(0, rows_per_chunk, pair):
            w = [None] * pair
            for hd in range(PEER_HEADS):
                r2 = pltpu.bitcast(r2_ref[hd, :, lanes], BF16)
                e2 = pltpu.bitcast(e2_ref[hd, :, lanes], BF16)
                for u in range(pair):
                    cnt = _bf16_rows(l1_ref[hd, i1 + u:i1 + u + 1, lanes], n)
                    e1 = _bf16_rows(e1_ref[hd, i1 + u:i1 + u + 1, lanes], n)
                    wh = jnp.where(r2 < cnt, e2, zero) * e1
                    w[u] = wh if w[u] is None else w[u] + wh
            for u in range(pair):
                rows = slice((i1 + u) * n, (i1 + u + 1) * n)
                pw_ref[rows, lanes] = _gelu(st_ref[rows, lanes].astype(BF16)) * w[u]
    acc_ref[...] += _dot(up_ref[...], pw_ref[...])

    @pl.when(j == pl.num_programs(2) - 1)
    def _():
        gt2 = mod_ref[0, 5:6, :]
        x2 = x1_ref[0] + gt2 * acc_ref[...].T
        inv = lax.rsqrt(jnp.mean(x2 * x2, axis=-1, keepdims=True) + EPS)
        o_ref[0] = (x2 * inv) * gf_ref[...]


def _peer(h2t, down, up_t, r2, e2, l1, e1, x1, mod3, g_final, tt, ec):
    b, s, d = x1.shape
    ne = down.shape[0]
    rpc = ec // PEER_N_KEYS
    spb = s // tt
    tok = lambda bi, si, j: (0, 0, bi * spb + si)
    gate_tab = pl.BlockSpec((PEER_HEADS, rpc, tt), lambda bi, si, j: (0, j, bi * spb + si))
    return pl.pallas_call(
        functools.partial(_peer_kernel, rows_per_chunk=rpc),
        grid=(b, spb, ne // ec),
        in_specs=[pl.BlockSpec((d, tt), lambda bi, si, j: (0, bi * spb + si)),
                  pl.BlockSpec((ec, d), lambda bi, si, j: (j, 0)),
                  pl.BlockSpec((d, ec), lambda bi, si, j: (0, j)),
                  pl.BlockSpec((PEER_HEADS, PEER_N_KEYS // 2, tt), tok),
                  pl.BlockSpec((PEER_HEADS, PEER_N_KEYS // 2, tt), tok),
                  gate_tab, gate_tab,
                  pl.BlockSpec((1, tt, d), lambda bi, si, j: (bi, si, 0)),
                  pl.BlockSpec((1, N_MOD, d), lambda bi, si, j: (bi, 0, 0)),
                  pl.BlockSpec((1, d), lambda bi, si, j: (0, 0))],
        out_specs=pl.BlockSpec((1, tt, d), lambda bi, si, j: (bi, si, 0)),
        out_shape=jax.ShapeDtypeStruct((b, s, d), F32),
        scratch_shapes=[pltpu.VMEM((d, tt), F32), pltpu.VMEM((ec, tt), F32), pltpu.VMEM((ec, tt), BF16)],
        compiler_params=_cparams("arbitrary", "arbitrary", "arbitrary"),
        name="peer",
    )(h2t, down, up_t, r2, e2, l1, e1, x1, mod3, g_final)


def _pad_heads(w):
    d = w.shape[0]
    w = w.reshape(d, N_HEADS, HEAD_DIM)
    return jnp.pad(w, ((0, 0), (0, 0), (0, LANES - HEAD_DIM))).reshape(d, HP)


def _bias_constants():
    pq = np.zeros((3 * LANES, HP), np.float32)
    pk = np.zeros((3 * LANES, HP), np.float32)
    cq = np.zeros((1, HP), np.float32)
    ck = np.zeros((1, HP), np.float32)
    cv = np.zeros((1, HP), np.float32)
    for h in range(N_HEADS):
        for part in range(3):
            pq[part * LANES + h, h * LANES + LANE_F_Q + part] = 1.0
            pk[part * LANES + h, h * LANES + LANE_ONE_Q + part] = -1.0
            cq[0, h * LANES + LANE_ONE_Q + part] = 1.0
            ck[0, h * LANES + LANE_F_Q + part] = 1.0
        cv[0, h * LANES + LANE_ONE_V] = 1.0
    return (jnp.asarray(pq, BF16), jnp.asarray(pk, BF16), jnp.asarray(cq), jnp.asarray(ck), jnp.asarray(cv))


def _tile(n, pref):
    return pref if n % pref == 0 else n


def kernel(x, c, w_ada, b_ada, g_norm_mix, w_in, b_forget, g_out_fox, g_out_sb, w_out, g_norm_ffn,
           w_query, sub_keys, expert_down, expert_up, g_final):
    b, s, d = x.shape
    assert w_ada.shape[0] == 1, "single-layer block: the final RMSNorm is fused into the PEER kernel"
    fw = N_HEADS * HEAD_DIM
    scale = HEAD_DIM ** -0.5 * LOG2E
    pq, pk, cq, ck, cv = _bias_constants()
    tm = _tile(s, 512)
    c_pad = jnp.pad(c, ((0, -b % 8), (0, 0)))

    for l in range(1):
        mod = _ada(c_pad, w_ada[l], b_ada[l][None, :])[:b]
        mod3 = mod.reshape(b, N_MOD, d)

        wi = w_in[l]
        o1 = 3 * fw
        o2 = o1 + N_HEADS
        w_all = jnp.concatenate([
            _pad_heads(wi[:, 0:fw] * scale), _pad_heads(wi[:, fw:2 * fw]), _pad_heads(wi[:, 2 * fw:o1]),
            _pad_heads(wi[:, o2:o2 + fw] * scale), _pad_heads(wi[:, o2 + fw:o2 + 2 * fw]),
            _pad_heads(wi[:, o2 + 2 * fw:o2 + 3 * fw])], axis=1).astype(BF16)
        wf = jnp.pad(wi[:, o1:o2], ((0, 0), (0, LANES - N_HEADS)))
        wf_hi = wf.astype(BF16)
        wf_lo = (wf - wf_hi.astype(F32)).astype(BF16)
        b_f = jnp.pad(b_forget[l], (0, LANES - N_HEADS))[None, :]

        fq, fk, fv, sq, sk, sv = _inproj(x, mod3, g_norm_mix[l][None, :], w_all, wf_hi, wf_lo, b_f,
                                         pq, pk, cq, ck, cv, tm)

        g_fox = jnp.pad(g_out_fox[l], (0, LANES - HEAD_DIM))[None, :]
        g_sb = jnp.pad(g_out_sb[l], (0, LANES - HEAD_DIM))[None, :]
        fo = _fox(fq, fk, fv, g_fox, _tile(s, 512), _tile(s, 256))
        so = _sb(sq, sk, sv, g_sb, _tile(s, 512), _tile(s, 256))

        wo = w_out[l]
        w_out_p = jnp.pad(wo.reshape(2 * N_HEADS, HEAD_DIM, d),
                          ((0, 0), (0, LANES - HEAD_DIM), (0, 0))).reshape(2 * HP, d).astype(BF16)
        wc = _wc(sub_keys[l].reshape(PEER_HEADS * 2, PEER_N_KEYS, LANES), w_query[l])
        wc_hi = wc.astype(BF16)
        wc_lo = (wc - wc_hi.astype(F32)).astype(BF16)
        x1, h2t, scores_t = _outproj(x, fo, so, mod3, g_norm_ffn[l][None, :], w_out_p, wc_hi, wc_lo, tm)

        r2, e2, l1, e1 = _topk(scores_t, _tile(b * s, 512))
        x = _peer(h2t, expert_down[l].astype(BF16), expert_up[l].T.astype(BF16), r2, e2, l1, e1,
                  x1, mod3, g_final[None, :], _tile(s, 1024), 1024)
    return x
```

```python
import functools

import numpy as np
import jax
import jax.numpy as jnp
from jax import lax
from jax.experimental import pallas as pl
from jax.experimental.pallas import tpu as pltpu

F32 = jnp.float32
BF16 = jnp.bfloat16

HEAD_DIM = 64
N_HEADS = 8
PEER_HEADS = 8
PEER_N_KEYS = 128
PEER_TOPK = 16
N_MOD = 6
EPS = 1e-6

LANES = 128
VMEM_LIMIT = 56 * 1024 * 1024

HP = N_HEADS * LANES
LANE_ONE_Q = HEAD_DIM
LANE_F_Q = HEAD_DIM + 3
LANE_ONE_V = HEAD_DIM
NEG_INF = float("-inf")
LOG2E = 1.4426950408889634
EXP2_UNDERFLOW = -153.0


def _cparams(*sem, flags=None):
    return pltpu.CompilerParams(dimension_semantics=sem, vmem_limit_bytes=VMEM_LIMIT, flags=flags)


def _dot(a, b):
    return jnp.dot(a, b, preferred_element_type=F32)


def _dot_nt(a, b):
    return lax.dot_general(a, b, (((1,), (1,)), ((), ())), preferred_element_type=F32)


def _split2(x):
    hi = x.astype(BF16)
    lo = (x - hi.astype(F32)).astype(BF16)
    return hi, lo


def _split3(x):
    hi = x.astype(BF16)
    r = x - hi.astype(F32)
    mid = r.astype(BF16)
    lo = (r - mid.astype(F32)).astype(BF16)
    return hi, mid, lo


def _softplus2(z):
    return jnp.maximum(z, 0.0) + jnp.log2(1.0 + jnp.exp2(-jnp.abs(z)))


def _ada_kernel(c_ref, w_ref, b_ref, o_ref):
    c = c_ref[...]
    a = c / (1.0 + jnp.exp(-c))
    ah, al = _split2(a)
    wh, wl = _split2(w_ref[...])
    o_ref[...] = _dot(ah, wh) + _dot(ah, wl) + _dot(al, wh) + b_ref[...]


def _ada(c_pad, w_ada, b_ada):
    rows, d = c_pad.shape
    n = w_ada.shape[1]
    bn = 1024
    return pl.pallas_call(
        _ada_kernel,
        grid=(n // bn,),
        in_specs=[pl.BlockSpec((rows, d), lambda j: (0, 0)),
                  pl.BlockSpec((d, bn), lambda j: (0, j)),
                  pl.BlockSpec((1, bn), lambda j: (0, j))],
        out_specs=pl.BlockSpec((rows, bn), lambda j: (0, j)),
        out_shape=jax.ShapeDtypeStruct((rows, n), F32),
        compiler_params=_cparams("arbitrary"),
        name="ada",
    )(c_pad, w_ada, b_ada)


def _wc_kernel(sk_ref, wq_ref, o_ref):
    sh, sl = _split2(sk_ref[0])
    wh, wl = _split2(wq_ref[...])
    o_ref[...] = _dot_nt(sh, wh) + _dot_nt(sh, wl) + _dot_nt(sl, wh)


def _wc(sub_keys, w_query):
    d = w_query.shape[0]
    nhp = sub_keys.shape[0]
    return pl.pallas_call(
        _wc_kernel,
        grid=(nhp,),
        in_specs=[pl.BlockSpec((1, PEER_N_KEYS, LANES), lambda i: (i, 0, 0)),
                  pl.BlockSpec((d, LANES), lambda i: (0, i))],
        out_specs=pl.BlockSpec((PEER_N_KEYS, d), lambda i: (i, 0)),
        out_shape=jax.ShapeDtypeStruct((nhp * PEER_N_KEYS, d), F32),
        compiler_params=_cparams("arbitrary"),
        name="wc",
    )(sub_keys, w_query)


def _inproj_kernel(x_ref, mod_ref, g_ref, w_ref, wfh_ref, wfl_ref, bf_ref, pq_ref, pk_ref,
                   cq_ref, ck_ref, cv_ref,
                   fq_ref, fk_ref, fv_ref, sq_ref, sk_ref, sv_ref, carry_ref):
    tm = x_ref.shape[1]

    @pl.when(pl.program_id(1) == 0)
    def _():
        carry_ref[...] = jnp.zeros_like(carry_ref)

    x = x_ref[0]
    inv = lax.rsqrt(jnp.mean(x * x, axis=-1, keepdims=True) + EPS)
    sh1 = mod_ref[0, 0:1, :]
    sc1 = mod_ref[0, 1:2, :]
    h = (x * inv) * g_ref[...] * (1.0 + sc1) + sh1
    hh, hl = _split2(h)

    fl = _dot(hh, wfh_ref[...]) + _dot(hh, wfl_ref[...]) + _dot(hl, wfh_ref[...]) + bf_ref[...]
    lf = jnp.minimum(fl, 0.0) - jnp.log(1.0 + jnp.exp(-jnp.abs(fl)))
    row = lax.broadcasted_iota(jnp.int32, (tm, tm), 0)
    col = lax.broadcasted_iota(jnp.int32, (tm, tm), 1)
    tri = jnp.where(col <= row, 1.0, 0.0).astype(BF16)
    l0, l1, l2 = _split3(lf)
    fcum = _dot(tri, l0) + _dot(tri, l1) + _dot(tri, l2) + carry_ref[...]
    carry_ref[...] = fcum[tm - 1:tm, :]
    f0, f1, f2 = _split3(fcum * LOG2E)
    fcat = jnp.concatenate([f0, f1, f2], axis=1)

    fq_ref[0] = (_dot(hh, w_ref[:, 0 * HP:1 * HP]) + _dot(fcat, pq_ref[...]) + cq_ref[...]).astype(BF16)
    fk_ref[0] = (_dot(hh, w_ref[:, 1 * HP:2 * HP]) + _dot(fcat, pk_ref[...]) + ck_ref[...]).astype(BF16)
    fv_ref[0] = (_dot(hh, w_ref[:, 2 * HP:3 * HP]) + cv_ref[...]).astype(BF16)
    sq_ref[0] = _dot(hh, w_ref[:, 3 * HP:4 * HP]).astype(BF16)
    sk_ref[0] = _dot(hh, w_ref[:, 4 * HP:5 * HP]).astype(BF16)
    sv_ref[0] = _dot(hh, w_ref[:, 5 * HP:6 * HP]).astype(BF16)


def _inproj(x, mod3, g_mix, w_all, wf_hi, wf_lo, b_f, pq, pk, cq, ck, cv, tm):
    b, s, d = x.shape
    const = lambda shape: pl.BlockSpec(shape, lambda bi, si: (0,) * len(shape),
                                       pipeline_mode=pl.Buffered(1))
    out_spec = pl.BlockSpec((1, tm, HP), lambda bi, si: (bi, si, 0))
    out_sds = jax.ShapeDtypeStruct((b, s, HP), BF16)
    return pl.pallas_call(
        _inproj_kernel,
        grid=(b, s // tm),
        in_specs=[pl.BlockSpec((1, tm, d), lambda bi, si: (bi, si, 0)),
                  pl.BlockSpec((1, N_MOD, d), lambda bi, si: (bi, 0, 0)),
                  const((1, d)),
                  const((d, 6 * HP)),
                  const((d, LANES)), const((d, LANES)), const((1, LANES)),
                  const((3 * LANES, HP)), const((3 * LANES, HP)),
                  const((1, HP)), const((1, HP)), const((1, HP))],
        out_specs=[out_spec] * 6,
        out_shape=[out_sds] * 6,
        scratch_shapes=[pltpu.VMEM((1, LANES), F32)],
        compiler_params=_cparams("arbitrary", "arbitrary"),
        name="inproj",
    )(x, mod3, g_mix, w_all, wf_hi, wf_lo, b_f, pq, pk, cq, ck, cv)


def _head_norm(o, g):
    lane = lax.broadcasted_iota(jnp.int32, o.shape, 1)
    o = jnp.where(lane < HEAD_DIM, o, 0.0)
    ms = jnp.sum(o * o, axis=-1, keepdims=True) * (1.0 / HEAD_DIM)
    return o * lax.rsqrt(ms + EPS) * g


HEADS_PER_STEP = 2


def _head_slices(ref):
    return [slice(h * LANES, (h + 1) * LANES) for h in range(ref.shape[2] // LANES)]


def _fox_kernel(q_ref, k_ref, v_ref, g_ref, o_ref, m_ref, acc_ref, *, tq, tk):
    s_len = q_ref.shape[1]
    heads = _head_slices(q_ref)
    n_diag = tq // tk
    lane1 = lax.broadcasted_iota(jnp.int32, (1, LANES), 1)
    head_lanes = lane1 < HEAD_DIM
    row = lax.broadcasted_iota(jnp.int32, (tq, tk), 0)
    col = lax.broadcasted_iota(jnp.int32, (tq, tk), 1)

    def knorm(i, mx):
        out = []
        for hl, m in zip(heads, mx):
            kb = k_ref[0, pl.ds(pl.multiple_of(i * tk, tk), tk), hl].astype(F32)
            kb = jnp.where(head_lanes, kb, 0.0)
            n2 = jnp.sum(kb * kb, axis=-1, keepdims=True)
            out.append(jnp.maximum(m, jnp.max(n2, axis=0, keepdims=True)))
        return tuple(out)

    kmax = [jnp.sqrt(m) for m in
            lax.fori_loop(0, s_len // tk, knorm, tuple(jnp.zeros((1, 1), F32) for _ in heads))]

    def q_tile(qi, carry):
        q0 = pl.multiple_of(qi * tq, tq)
        q = [q_ref[0, pl.ds(q0, tq), hl] for hl in heads]
        reach = []
        for h, qh in enumerate(q):
            qf = qh.astype(F32)
            qn = jnp.sqrt(jnp.sum(jnp.where(head_lanes, qf * qf, 0.0), axis=-1, keepdims=True))
            f_t = jnp.sum(jnp.where((lane1 >= LANE_F_Q) & (lane1 < LANE_F_Q + 3), qf, 0.0),
                          axis=-1, keepdims=True)
            reach.append(qn * kmax[h] + f_t)
        m_ref[...] = jnp.full_like(m_ref, NEG_INF)
        acc_ref[...] = jnp.zeros_like(acc_ref)

        def step(start, mask, r0=0):
            for h, hl in enumerate(heads):
                k = k_ref[0, pl.ds(start, tk), hl]
                v = v_ref[0, pl.ds(start, tk), hl]
                s = _dot_nt(q[h][r0:], k)
                if mask is not None:
                    s = jnp.where(mask[r0:], s, NEG_INF)
                m_prev = m_ref[h, r0:, :]
                m_new = jnp.maximum(m_prev, jnp.max(s, axis=-1, keepdims=True))
                p = jnp.exp2(s - jnp.concatenate([m_new] * (tk // LANES), axis=1))
                acc_ref[h, r0:, :] = jnp.exp2(m_prev - m_new) * acc_ref[h, r0:, :] + _dot(p.astype(BF16), v)
                m_ref[h, r0:, :] = m_new

        def block_matters(kb):
            worst = None
            for h, hl in enumerate(heads):
                last = k_ref[0, pl.ds(pl.multiple_of(kb * tk + tk - 16, 16), 16), hl][15:16, :].astype(F32)
                neg_f_last = jnp.sum(jnp.where((lane1 >= LANE_ONE_Q) & (lane1 < LANE_ONE_Q + 3), last, 0.0),
                                     axis=-1, keepdims=True)
                bound = reach[h] + neg_f_last - m_ref[h, :, 0:1]
                worst = bound if worst is None else jnp.maximum(worst, bound)
            return jnp.max(worst) > EXP2_UNDERFLOW

        for d in range(n_diag):
            step(pl.multiple_of(q0 + d * tk, tk), col + d * tk <= row, d * tk)

        def cond(st):
            return st[1]

        def body(st):
            kb = st[0]
            step(pl.multiple_of(kb * tk, tk), None)
            nxt = kb - 1
            return nxt, (nxt >= 0) & block_matters(jnp.maximum(nxt, 0))

        kb0 = qi * n_diag - 1
        lax.while_loop(cond, body, (kb0, (kb0 >= 0) & block_matters(jnp.maximum(kb0, 0))))

        for h, hl in enumerate(heads):
            acc = acc_ref[h]
            lane = lax.broadcasted_iota(jnp.int32, acc.shape, 1)
            l = jnp.sum(jnp.where(lane == LANE_ONE_V, acc, 0.0), axis=-1, keepdims=True)
            o_ref[0, pl.ds(q0, tq), hl] = _head_norm(acc / l, g_ref[...]).astype(o_ref.dtype)
        return carry

    lax.fori_loop(0, s_len // tq, q_tile, 0)


def _attn_call(body, name, q, k, v, g, tq):
    b, s, _ = q.shape
    nh = HEADS_PER_STEP
    spec = pl.BlockSpec((1, s, nh * LANES), lambda bi, hi: (bi, 0, hi))
    return pl.pallas_call(
        body,
        grid=(b, N_HEADS // nh),
        in_specs=[spec, spec, spec, pl.BlockSpec((1, LANES), lambda bi, hi: (0, 0))],
        out_specs=spec,
        out_shape=jax.ShapeDtypeStruct((b, s, HP), BF16),
        scratch_shapes=[pltpu.VMEM((nh, tq, LANES), F32), pltpu.VMEM((nh, tq, LANES), F32)],
        compiler_params=_cparams("arbitrary", "arbitrary"),
        name=name,
    )(q, k, v, g)


def _fox(q, k, v, g, tq, tk):
    return _attn_call(functools.partial(_fox_kernel, tq=tq, tk=tk), "fox", q, k, v, g, tq)


def _sb_kernel(q_ref, k_ref, v_ref, g_ref, o_ref, c_ref, acc_ref, *, tq, tk):
    s_len = q_ref.shape[1]
    n_diag = tq // tk
    jrow = lax.broadcasted_iota(jnp.int32, (tk, tk), 0)
    scol = lax.broadcasted_iota(jnp.int32, (tk, tk), 1)
    later = jnp.where(jrow > scol, 1.0, 0.0).astype(BF16)
    row = lax.broadcasted_iota(jnp.int32, (tq, tk), 0)
    col = lax.broadcasted_iota(jnp.int32, (tq, tk), 1)

    heads = _head_slices(q_ref)

    def q_tile(qi, carry):
        q0 = pl.multiple_of(qi * tq, tq)
        q = [q_ref[0, pl.ds(q0, tq), hl] for hl in heads]
        c_ref[...] = jnp.zeros_like(c_ref)
        acc_ref[...] = jnp.zeros_like(acc_ref)

        def step(start, mask, r0=0):
            for h, hl in enumerate(heads):
                k = k_ref[0, pl.ds(start, tk), hl]
                v = v_ref[0, pl.ds(start, tk), hl]
                z = _dot_nt(q[h][r0:], k)
                sp = _softplus2(z)
                spm = sp if mask is None else jnp.where(mask[r0:], sp, 0.0)
                hi, lo = _split2(spm)
                c_prev = c_ref[h, r0:, :]
                rest = _dot(hi, later) + _dot(lo, later) + jnp.concatenate([c_prev] * (tk // LANES), axis=1)
                a = jnp.exp2(z - sp - rest)
                if mask is not None:
                    a = jnp.where(mask[r0:], a, 0.0)
                acc_ref[h, r0:, :] += _dot(a.astype(BF16), v)
                c_ref[h, r0:, :] = c_prev + jnp.sum(spm, axis=-1, keepdims=True)

        def more_matters():
            least = None
            for h in range(len(heads)):
                ch = c_ref[h, :, 0:1]
                least = ch if least is None else jnp.minimum(least, ch)
            return jnp.min(least) < -EXP2_UNDERFLOW

        for d in reversed(range(n_diag)):
            step(pl.multiple_of(q0 + d * tk, tk), col + d * tk < row, d * tk)

        def cond(st):
            return st[1]

        def body(st):
            kb = st[0]
            step(pl.multiple_of(kb * tk, tk), None)
            nxt = kb - 1
            return nxt, (nxt >= 0) & more_matters()

        kb0 = qi * n_diag - 1
        lax.while_loop(cond, body, (kb0, (kb0 >= 0) & more_matters()))
        for h, hl in enumerate(heads):
            o_ref[0, pl.ds(q0, tq), hl] = _head_norm(acc_ref[h], g_ref[...]).astype(o_ref.dtype)
        return carry

    lax.fori_loop(0, s_len // tq, q_tile, 0)


def _sb(q, k, v, g, tq, tk):
    return _attn_call(functools.partial(_sb_kernel, tq=tq, tk=tk), "sb", q, k, v, g, tq)


def _outproj_kernel(x_ref, fo_ref, so_ref, mod_ref, g_ref, wo_ref, wc_ref,
                    x1_ref, h2t_ref, sc_ref):
    gt1 = mod_ref[0, 2:3, :]
    sh2 = mod_ref[0, 3:4, :]
    sc2 = mod_ref[0, 4:5, :]
    mixed = _dot(fo_ref[0], wo_ref[0:HP, :]) + _dot(so_ref[0], wo_ref[HP:2 * HP, :])
    x1 = x_ref[0] + gt1 * mixed
    x1_ref[0] = x1
    inv = lax.rsqrt(jnp.mean(x1 * x1, axis=-1, keepdims=True) + EPS)
    h2 = (x1 * inv) * g_ref[...] * (1.0 + sc2) + sh2
    th = h2.T.astype(BF16)
    h2t_ref[...] = th
    sc_ref[...] = _dot(wc_ref[...], th)


def _outproj(x, fo, so, mod3, g_ffn, w_out_p, wc, tm):
    b, s, d = x.shape
    t = b * s
    nsc = wc.shape[0]
    const = lambda shape: pl.BlockSpec(shape, lambda bi, si: (0,) * len(shape),
                                       pipeline_mode=pl.Buffered(1))
    spb = s // tm
    return pl.pallas_call(
        _outproj_kernel,
        grid=(b, spb),
        in_specs=[pl.BlockSpec((1, tm, d), lambda bi, si: (bi, si, 0)),
                  pl.BlockSpec((1, tm, HP), lambda bi, si: (bi, si, 0)),
                  pl.BlockSpec((1, tm, HP), lambda bi, si: (bi, si, 0)),
                  pl.BlockSpec((1, N_MOD, d), lambda bi, si: (bi, 0, 0)),
                  const((1, d)),
                  const((2 * HP, d)),
                  const((nsc, d))],
        out_specs=[pl.BlockSpec((1, tm, d), lambda bi, si: (bi, si, 0)),
                   pl.BlockSpec((d, tm), lambda bi, si: (0, bi * spb + si)),
                   pl.BlockSpec((nsc, tm), lambda bi, si: (0, bi * spb + si))],
        out_shape=[jax.ShapeDtypeStruct((b, s, d), F32),
                   jax.ShapeDtypeStruct((d, t), BF16),
                   jax.ShapeDtypeStruct((nsc, t), F32)],
        compiler_params=_cparams("arbitrary", "arbitrary"),
        name="outproj",
    )(x, fo, so, mod3, g_ffn, w_out_p, wc)


def _top16(s, exact):
    n, t = s.shape
    idx = lax.broadcasted_iota(jnp.int32, (n, t), 0).astype(F32)
    krow = lax.broadcasted_iota(jnp.int32, (PEER_TOPK, t), 0)
    rank = jnp.full((n, t), float(PEER_TOPK), F32)
    vals = jnp.zeros((PEER_TOPK, t), F32)
    for k in range(PEER_TOPK):
        m = jnp.max(s, axis=0, keepdims=True)
        hit = s == m
        if exact:
            hit = idx == jnp.min(jnp.where(hit, idx, float(n)), axis=0, keepdims=True)
        rank = jnp.where(hit, float(k), rank)
        s = jnp.where(hit, NEG_INF, s)
        vals = jnp.where(krow == k, m, vals)
    return vals, rank


SUBLANES = 8
N_CAND = PEER_TOPK + (SUBLANES - 1) * SUBLANES + SUBLANES


def _cand_flat_pos():
    r = lax.broadcasted_iota(jnp.int32, (N_CAND, LANES), 0)
    mid_lo, hi_lo = PEER_TOPK, N_CAND - SUBLANES
    sub_shift = SUBLANES.bit_length() - 1
    mid = (jnp.right_shift(r - mid_lo, sub_shift) + 1) * PEER_TOPK + jnp.bitwise_and(r - mid_lo, SUBLANES - 1)
    hi = (r - hi_lo + SUBLANES) * PEER_TOPK
    return jnp.where(r < mid_lo, r, jnp.where(r < hi_lo, mid, hi)).astype(F32)


def _pair_top16(v1, v2, r1, exact):
    cand = jnp.concatenate(
        [v1[0:1, :] + v2]
        + [v1[a:a + 1, :] + v2[0:SUBLANES, :] for a in range(1, SUBLANES)]
        + [v1[SUBLANES:, :] + v2[0:1, :]], axis=0)
    pos = _cand_flat_pos() if exact else None
    c0 = cand[0:1, :]
    zsum = jnp.zeros((1, LANES), F32)
    for k in range(PEER_TOPK):
        m = jnp.max(cand, axis=0, keepdims=True)
        hit = cand == m
        if exact:
            hit = pos == jnp.min(jnp.where(hit, pos, float(PEER_TOPK * PEER_TOPK)), axis=0, keepdims=True)
        cand = jnp.where(hit, NEG_INF, cand)
        zsum = zsum + jnp.exp(m - c0)
    taken = jnp.where(cand == NEG_INF, 1.0, 0.0)
    n_taken = jnp.sum(taken, axis=0, keepdims=True)
    lrow = jnp.zeros(r1.shape, F32)
    for a in range(PEER_TOPK):
        if a == 0:
            cnt_a = jnp.sum(taken[0:PEER_TOPK, :], axis=0, keepdims=True)
        elif a < SUBLANES:
            lo = PEER_TOPK + SUBLANES * (a - 1)
            cnt_a = jnp.sum(taken[lo:lo + SUBLANES, :], axis=0, keepdims=True)
        else:
            lo = N_CAND - SUBLANES + a - SUBLANES
            cnt_a = taken[lo:lo + 1, :]
        lrow = jnp.where(r1 == float(a), cnt_a, lrow)
    return lrow, zsum, n_taken


def _topk_kernel(sc_ref, r2_ref, e2_ref, l1_ref, e1_ref):
    n = PEER_N_KEYS
    tl = sc_ref.shape[1]
    ngrp = tl // LANES

    def run(item, exact):
        hd = lax.div(item, ngrp)
        gi = lax.rem(item, ngrp)
        lanes = pl.ds(pl.multiple_of(gi * LANES, LANES), LANES)
        s1 = sc_ref[pl.ds(pl.multiple_of(hd * 2 * n, n), n), lanes]
        s2 = sc_ref[pl.ds(pl.multiple_of(hd * 2 * n + n, n), n), lanes]
        v1, r1 = _top16(s1, exact)
        v2, r2 = _top16(s2, exact)
        lrow, zsum, n_taken = _pair_top16(v1, v2, r1, exact)
        r2_ref[hd, :, lanes] = pltpu.bitcast(r2.astype(BF16), jnp.uint32)
        e2_ref[hd, :, lanes] = pltpu.bitcast(jnp.exp(s2 - v2[0:1, :]).astype(BF16), jnp.uint32)
        l1_ref[hd, :, lanes] = _dup_bf16_words(lrow)
        e1_ref[hd, :, lanes] = _dup_bf16_words(jnp.exp(s1 - v1[0:1, :]) / zsum)
        k = float(PEER_TOPK)
        picked = lambda r: jnp.sum(jnp.where(r < k, 1.0, 0.0), axis=0, keepdims=True)
        return jnp.abs(picked(r1) - k) + jnp.abs(picked(r2) - k) + jnp.abs(n_taken - k)

    together = 2 if (PEER_HEADS * ngrp) % 2 == 0 else 1

    def body(it, carry):
        items = [it * together + u for u in range(together)]
        excess = [run(item, exact=False) for item in items]
        for item, ex in zip(items, excess):
            @pl.when(jnp.max(ex) > 0.0)
            def _():
                run(item, exact=True)

        return carry

    lax.fori_loop(0, PEER_HEADS * ngrp // together, body, 0)


def _topk(scores_t, tl):
    nsc, t = scores_t.shape
    spec = lambda rows: pl.BlockSpec((PEER_HEADS, rows, tl), lambda i: (0, 0, i))
    sds = lambda rows: jax.ShapeDtypeStruct((PEER_HEADS, rows, t), jnp.uint32)
    half, full = PEER_N_KEYS // 2, PEER_N_KEYS
    return pl.pallas_call(
        _topk_kernel,
        grid=(t // tl,),
        in_specs=[pl.BlockSpec((nsc, tl), lambda i: (0, i))],
        out_specs=[spec(half), spec(half), spec(full), spec(full)],
        out_shape=[sds(half), sds(half), sds(full), sds(full)],
        compiler_params=_cparams("arbitrary"),
        name="topk",
    )(scores_t)


def _gelu(x):
    return 0.5 * x * (1.0 + lax.erf(x * (2.0 ** -0.5)))


def _dup_bf16_words(x):
    bits = pltpu.bitcast(x.astype(BF16).astype(F32), jnp.uint32)
    return bits | (bits >> 16)


def _bf16_rows(words, n):
    return pltpu.bitcast(jnp.broadcast_to(words, (n // 2, words.shape[1])), BF16)


def _peer_kernel(h2t_ref, dn_ref, up_ref, r2_ref, e2_ref, l1_ref, e1_ref, x1_ref, mod_ref, gf_ref,
                 o_ref, acc_ref, st_ref, pw_ref, *, rows_per_chunk):
    j = pl.program_id(2)
    n = PEER_N_KEYS

    @pl.when(j == 0)
    def _():
        acc_ref[...] = jnp.zeros_like(acc_ref)

    st_ref[...] = _dot(dn_ref[...], h2t_ref[...])
    zero = jnp.zeros((n, LANES), BF16)
    pair = 2
    for g in range(h2t_ref.shape[1] // LANES):
        lanes = slice(g * LANES, (g + 1) * LANES)
        for i1 in range(0, rows_per_chunk, pair):
            w = [None] * pair
            for hd in range(PEER_HEADS):
                r2 = pltpu.bitcast(r2_ref[hd, :, lanes], BF16)
                e2 = pltpu.bitcast(e2_ref[hd, :, lanes], BF16)
                for u in range(pair):
                    cnt = _bf16_rows(l1_ref[hd, i1 + u:i1 + u + 1, lanes], n)
                    e1 = _bf16_rows(e1_ref[hd, i1 + u:i1 + u + 1, lanes], n)
                    wh = jnp.where(r2 < cnt, e2, zero) * e1
                    w[u] = wh if w[u] is None else w[u] + wh
            for u in range(pair):
                rows = slice((i1 + u) * n, (i1 + u + 1) * n)
                pw_ref[rows, lanes] = _gelu(st_ref[rows, lanes].astype(BF16)) * w[u]
    acc_ref[...] += _dot(up_ref[...], pw_ref[...])

    @pl.when(j == pl.num_programs(2) - 1)
    def _():
        gt2 = mod_ref[0, 5:6, :]
        x2 = x1_ref[0] + gt2 * acc_ref[...].T
        inv = lax.rsqrt(jnp.mean(x2 * x2, axis=-1, keepdims=True) + EPS)
        o_ref[0] = (x2 * inv) * gf_ref[...]


def _peer(h2t, down, up_t, r2, e2, l1, e1, x1, mod3, g_final, tt, ec):
    b, s, d = x1.shape
    ne = down.shape[0]
    rpc = ec // PEER_N_KEYS
    spb = s // tt
    tok = lambda bi, si, j: (0, 0, bi * spb + si)
    gate_tab = pl.BlockSpec((PEER_HEADS, rpc, tt), lambda bi, si, j: (0, j, bi * spb + si))
    return pl.pallas_call(
        functools.partial(_peer_kernel, rows_per_chunk=rpc),
        grid=(b, spb, ne // ec),
        in_specs=[pl.BlockSpec((d, tt), lambda bi, si, j: (0, bi * spb + si)),
                  pl.BlockSpec((ec, d), lambda bi, si, j: (j, 0)),
                  pl.BlockSpec((d, ec), lambda bi, si, j: (0, j)),
                  pl.BlockSpec((PEER_HEADS, PEER_N_KEYS // 2, tt), tok),
                  pl.BlockSpec((PEER_HEADS, PEER_N_KEYS // 2, tt), tok),
                  gate_tab, gate_tab,
                  pl.BlockSpec((1, tt, d), lambda bi, si, j: (bi, si, 0)),
                  pl.BlockSpec((1, N_MOD, d), lambda bi, si, j: (bi, 0, 0)),
                  pl.BlockSpec((1, d), lambda bi, si, j: (0, 0))],
        out_specs=pl.BlockSpec((1, tt, d), lambda bi, si, j: (bi, si, 0)),
        out_shape=jax.ShapeDtypeStruct((b, s, d), F32),
        scratch_shapes=[pltpu.VMEM((d, tt), F32), pltpu.VMEM((ec, tt), F32), pltpu.VMEM((ec, tt), BF16)],
        compiler_params=_cparams("arbitrary", "arbitrary", "arbitrary"),
        name="peer",
    )(h2t, down, up_t, r2, e2, l1, e1, x1, mod3, g_final)


def _pad_heads(w):
    d = w.shape[0]
    w = w.reshape(d, N_HEADS, HEAD_DIM)
    return jnp.pad(w, ((0, 0), (0, 0), (0, LANES - HEAD_DIM))).reshape(d, HP)


def _bias_constants():
    pq = np.zeros((3 * LANES, HP), np.float32)
    pk = np.zeros((3 * LANES, HP), np.float32)
    cq = np.zeros((1, HP), np.float32)
    ck = np.zeros((1, HP), np.float32)
    cv = np.zeros((1, HP), np.float32)
    for h in range(N_HEADS):
        for part in range(3):
            pq[part * LANES + h, h * LANES + LANE_F_Q + part] = 1.0
            pk[part * LANES + h, h * LANES + LANE_ONE_Q + part] = -1.0
            cq[0, h * LANES + LANE_ONE_Q + part] = 1.0
            ck[0, h * LANES + LANE_F_Q + part] = 1.0
        cv[0, h * LANES + LANE_ONE_V] = 1.0
    return (jnp.asarray(pq, BF16), jnp.asarray(pk, BF16), jnp.asarray(cq), jnp.asarray(ck), jnp.asarray(cv))


def _tile(n, pref):
    return pref if n % pref == 0 else n


def kernel(x, c, w_ada, b_ada, g_norm_mix, w_in, b_forget, g_out_fox, g_out_sb, w_out, g_norm_ffn,
           w_query, sub_keys, expert_down, expert_up, g_final):
    b, s, d = x.shape
    assert w_ada.shape[0] == 1, "single-layer block: the final RMSNorm is fused into the PEER kernel"
    fw = N_HEADS * HEAD_DIM
    scale = HEAD_DIM ** -0.5 * LOG2E
    pq, pk, cq, ck, cv = _bias_constants()
    tm = _tile(s, 512)
    c_pad = jnp.pad(c, ((0, -b % 8), (0, 0)))

    for l in range(1):
        mod = _ada(c_pad, w_ada[l], b_ada[l][None, :])[:b]
        mod3 = mod.reshape(b, N_MOD, d)

        wi = w_in[l]
        o1 = 3 * fw
        o2 = o1 + N_HEADS
        w_all = jnp.concatenate([
            _pad_heads(wi[:, 0:fw] * scale), _pad_heads(wi[:, fw:2 * fw]), _pad_heads(wi[:, 2 * fw:o1]),
            _pad_heads(wi[:, o2:o2 + fw] * scale), _pad_heads(wi[:, o2 + fw:o2 + 2 * fw]),
            _pad_heads(wi[:, o2 + 2 * fw:o2 + 3 * fw])], axis=1).astype(BF16)
        wf = jnp.pad(wi[:, o1:o2], ((0, 0), (0, LANES - N_HEADS)))
        wf_hi = wf.astype(BF16)
        wf_lo = (wf - wf_hi.astype(F32)).astype(BF16)
        b_f = jnp.pad(b_forget[l], (0, LANES - N_HEADS))[None, :]

        fq, fk, fv, sq, sk, sv = _inproj(x, mod3, g_norm_mix[l][None, :], w_all, wf_hi, wf_lo, b_f,
                                         pq, pk, cq, ck, cv, tm)

        g_fox = jnp.pad(g_out_fox[l], (0, LANES - HEAD_DIM))[None, :]
        g_sb = jnp.pad(g_out_sb[l], (0, LANES - HEAD_DIM))[None, :]
        fo = _fox(fq, fk, fv, g_fox, _tile(s, 512), _tile(s, 256))
        so = _sb(sq, sk, sv, g_sb, _tile(s, 512), _tile(s, 256))

        wo = w_out[l]
        w_out_p = jnp.pad(wo.reshape(2 * N_HEADS, HEAD_DIM, d),
                          ((0, 0), (0, LANES - HEAD_DIM), (0, 0))).reshape(2 * HP, d).astype(BF16)
        wc = _wc(sub_keys[l].reshape(PEER_HEADS * 2, PEER_N_KEYS, LANES), w_query[l])
        x1, h2t, scores_t = _outproj(x, fo, so, mod3, g_norm_ffn[l][None, :], w_out_p, wc.astype(BF16), tm)

        r2, e2, l1, e1 = _topk(scores_t, _tile(b * s, 512))
        x = _peer(h2t, expert_down[l].astype(BF16), expert_up[l].T.astype(BF16), r2, e2, l1, e1,
                  x1, mod3, g_final[None, :], _tile(s, 1024), 1024)
    return x
```

```python
import functools

import numpy as np
import jax
import jax.numpy as jnp
from jax import lax
from jax.experimental import pallas as pl
from jax.experimental.pallas import tpu as pltpu

F32 = jnp.float32
BF16 = jnp.bfloat16

HEAD_DIM = 64
N_HEADS = 8
PEER_HEADS = 8
PEER_N_KEYS = 128
PEER_TOPK = 16
N_MOD = 6
EPS = 1e-6

LANES = 128
VMEM_LIMIT = 56 * 1024 * 1024

HP = N_HEADS * LANES
LANE_ONE_Q = HEAD_DIM
LANE_F_Q = HEAD_DIM + 3
LANE_ONE_V = HEAD_DIM
NEG_INF = float("-inf")
LOG2E = 1.4426950408889634
EXP2_UNDERFLOW = -153.0


def _cparams(*sem, flags=None):
    return pltpu.CompilerParams(dimension_semantics=sem, vmem_limit_bytes=VMEM_LIMIT, flags=flags)


def _dot(a, b):
    return jnp.dot(a, b, preferred_element_type=F32)


def _dot_nt(a, b):
    return lax.dot_general(a, b, (((1,), (1,)), ((), ())), preferred_element_type=F32)


def _split2(x):
    hi = x.astype(BF16)
    lo = (x - hi.astype(F32)).astype(BF16)
    return hi, lo


def _split3(x):
    hi = x.astype(BF16)
    r = x - hi.astype(F32)
    mid = r.astype(BF16)
    lo = (r - mid.astype(F32)).astype(BF16)
    return hi, mid, lo


def _softplus2(z):
    return jnp.maximum(z, 0.0) + jnp.log2(1.0 + jnp.exp2(-jnp.abs(z)))


def _ada_kernel(c_ref, w_ref, b_ref, o_ref):
    c = c_ref[...]
    a = c / (1.0 + jnp.exp(-c))
    ah, al = _split2(a)
    wh, wl = _split2(w_ref[...])
    o_ref[...] = _dot(ah, wh) + _dot(ah, wl) + _dot(al, wh) + b_ref[...]


def _ada(c_pad, w_ada, b_ada):
    rows, d = c_pad.shape
    n = w_ada.shape[1]
    bn = 1024
    return pl.pallas_call(
        _ada_kernel,
        grid=(n // bn,),
        in_specs=[pl.BlockSpec((rows, d), lambda j: (0, 0)),
                  pl.BlockSpec((d, bn), lambda j: (0, j)),
                  pl.BlockSpec((1, bn), lambda j: (0, j))],
        out_specs=pl.BlockSpec((rows, bn), lambda j: (0, j)),
        out_shape=jax.ShapeDtypeStruct((rows, n), F32),
        compiler_params=_cparams("arbitrary"),
        name="ada",
    )(c_pad, w_ada, b_ada)


def _wc_kernel(sk_ref, wq_ref, o_ref):
    sh, sl = _split2(sk_ref[0])
    wh, wl = _split2(wq_ref[...])
    o_ref[...] = _dot_nt(sh, wh) + _dot_nt(sh, wl) + _dot_nt(sl, wh)


def _wc(sub_keys, w_query):
    d = w_query.shape[0]
    nhp = sub_keys.shape[0]
    return pl.pallas_call(
        _wc_kernel,
        grid=(nhp,),
        in_specs=[pl.BlockSpec((1, PEER_N_KEYS, LANES), lambda i: (i, 0, 0)),
                  pl.BlockSpec((d, LANES), lambda i: (0, i))],
        out_specs=pl.BlockSpec((PEER_N_KEYS, d), lambda i: (i, 0)),
        out_shape=jax.ShapeDtypeStruct((nhp * PEER_N_KEYS, d), F32),
        compiler_params=_cparams("arbitrary"),
        name="wc",
    )(sub_keys, w_query)


def _inproj_kernel(x_ref, mod_ref, g_ref, w_ref, wfh_ref, wfl_ref, bf_ref, pq_ref, pk_ref,
                   cq_ref, ck_ref, cv_ref,
                   fq_ref, fk_ref, fv_ref, sq_ref, sk_ref, sv_ref, carry_ref):
    tm = x_ref.shape[1]

    @pl.when(pl.program_id(1) == 0)
    def _():
        carry_ref[...] = jnp.zeros_like(carry_ref)

    x = x_ref[0]
    inv = lax.rsqrt(jnp.mean(x * x, axis=-1, keepdims=True) + EPS)
    sh1 = mod_ref[0, 0:1, :]
    sc1 = mod_ref[0, 1:2, :]
    h = (x * inv) * g_ref[...] * (1.0 + sc1) + sh1
    hh, hl = _split2(h)

    fl = _dot(hh, wfh_ref[...]) + _dot(hh, wfl_ref[...]) + _dot(hl, wfh_ref[...]) + bf_ref[...]
    lf = jnp.minimum(fl, 0.0) - jnp.log(1.0 + jnp.exp(-jnp.abs(fl)))
    row = lax.broadcasted_iota(jnp.int32, (tm, tm), 0)
    col = lax.broadcasted_iota(jnp.int32, (tm, tm), 1)
    tri = jnp.where(col <= row, 1.0, 0.0).astype(BF16)
    l0, l1, l2 = _split3(lf)
    fcum = _dot(tri, l0) + _dot(tri, l1) + _dot(tri, l2) + carry_ref[...]
    carry_ref[...] = fcum[tm - 1:tm, :]
    f0, f1, f2 = _split3(fcum * LOG2E)
    fcat = jnp.concatenate([f0, f1, f2], axis=1)

    fq_ref[0] = (_dot(hh, w_ref[:, 0 * HP:1 * HP]) + _dot(fcat, pq_ref[...]) + cq_ref[...]).astype(BF16)
    fk_ref[0] = (_dot(hh, w_ref[:, 1 * HP:2 * HP]) + _dot(fcat, pk_ref[...]) + ck_ref[...]).astype(BF16)
    fv_ref[0] = (_dot(hh, w_ref[:, 2 * HP:3 * HP]) + cv_ref[...]).astype(BF16)
    sq_ref[0] = _dot(hh, w_ref[:, 3 * HP:4 * HP]).astype(BF16)
    sk_ref[0] = _dot(hh, w_ref[:, 4 * HP:5 * HP]).astype(BF16)
    sv_ref[0] = _dot(hh, w_ref[:, 5 * HP:6 * HP]).astype(BF16)


def _inproj(x, mod3, g_mix, w_all, wf_hi, wf_lo, b_f, pq, pk, cq, ck, cv, tm):
    b, s, d = x.shape
    const = lambda shape: pl.BlockSpec(shape, lambda bi, si: (0,) * len(shape),
                                       pipeline_mode=pl.Buffered(1))
    out_spec = pl.BlockSpec((1, tm, HP), lambda bi, si: (bi, si, 0))
    out_sds = jax.ShapeDtypeStruct((b, s, HP), BF16)
    return pl.pallas_call(
        _inproj_kernel,
        grid=(b, s // tm),
        in_specs=[pl.BlockSpec((1, tm, d), lambda bi, si: (bi, si, 0)),
                  pl.BlockSpec((1, N_MOD, d), lambda bi, si: (bi, 0, 0)),
                  const((1, d)),
                  const((d, 6 * HP)),
                  const((d, LANES)), const((d, LANES)), const((1, LANES)),
                  const((3 * LANES, HP)), const((3 * LANES, HP)),
                  const((1, HP)), const((1, HP)), const((1, HP))],
        out_specs=[out_spec] * 6,
        out_shape=[out_sds] * 6,
        scratch_shapes=[pltpu.VMEM((1, LANES), F32)],
        compiler_params=_cparams("arbitrary", "arbitrary"),
        name="inproj",
    )(x, mod3, g_mix, w_all, wf_hi, wf_lo, b_f, pq, pk, cq, ck, cv)


def _head_norm(o, g):
    lane = lax.broadcasted_iota(jnp.int32, o.shape, 1)
    o = jnp.where(lane < HEAD_DIM, o, 0.0)
    ms = jnp.sum(o * o, axis=-1, keepdims=True) * (1.0 / HEAD_DIM)
    return o * lax.rsqrt(ms + EPS) * g


HEADS_PER_STEP = 2


def _head_slices(ref):
    return [slice(h * LANES, (h + 1) * LANES) for h in range(ref.shape[2] // LANES)]


def _fox_kernel(q_ref, k_ref, v_ref, g_ref, o_ref, m_ref, acc_ref, *, tq, tk):
    s_len = q_ref.shape[1]
    heads = _head_slices(q_ref)
    n_diag = tq // tk
    lane1 = lax.broadcasted_iota(jnp.int32, (1, LANES), 1)
    head_lanes = lane1 < HEAD_DIM
    row = lax.broadcasted_iota(jnp.int32, (tq, tk), 0)
    col = lax.broadcasted_iota(jnp.int32, (tq, tk), 1)

    def knorm(i, mx):
        out = []
        for hl, m in zip(heads, mx):
            kb = k_ref[0, pl.ds(pl.multiple_of(i * tk, tk), tk), hl].astype(F32)
            kb = jnp.where(head_lanes, kb, 0.0)
            n2 = jnp.sum(kb * kb, axis=-1, keepdims=True)
            out.append(jnp.maximum(m, jnp.max(n2, axis=0, keepdims=True)))
        return tuple(out)

    kmax = [jnp.sqrt(m) for m in
            lax.fori_loop(0, s_len // tk, knorm, tuple(jnp.zeros((1, 1), F32) for _ in heads))]

    def q_tile(qi, carry):
        q0 = pl.multiple_of(qi * tq, tq)
        q = [q_ref[0, pl.ds(q0, tq), hl] for hl in heads]
        reach = []
        for h, qh in enumerate(q):
            qf = qh.astype(F32)
            qn = jnp.sqrt(jnp.sum(jnp.where(head_lanes, qf * qf, 0.0), axis=-1, keepdims=True))
            f_t = jnp.sum(jnp.where((lane1 >= LANE_F_Q) & (lane1 < LANE_F_Q + 3), qf, 0.0),
                          axis=-1, keepdims=True)
            reach.append(qn * kmax[h] + f_t)
        m_ref[...] = jnp.full_like(m_ref, NEG_INF)
        acc_ref[...] = jnp.zeros_like(acc_ref)

        def step(start, mask, r0=0):
            for h, hl in enumerate(heads):
                k = k_ref[0, pl.ds(start, tk), hl]
                v = v_ref[0, pl.ds(start, tk), hl]
                s = _dot_nt(q[h][r0:], k)
                if mask is not None:
                    s = jnp.where(mask[r0:], s, NEG_INF)
                m_prev = m_ref[h, r0:, :]
                m_new = jnp.maximum(m_prev, jnp.max(s, axis=-1, keepdims=True))
                p = jnp.exp2(s - jnp.concatenate([m_new] * (tk // LANES), axis=1))
                acc_ref[h, r0:, :] = jnp.exp2(m_prev - m_new) * acc_ref[h, r0:, :] + _dot(p.astype(BF16), v)
                m_ref[h, r0:, :] = m_new

        def block_matters(kb):
            worst = None
            for h, hl in enumerate(heads):
                last = k_ref[0, pl.ds(pl.multiple_of(kb * tk + tk - 16, 16), 16), hl][15:16, :].astype(F32)
                neg_f_last = jnp.sum(jnp.where((lane1 >= LANE_ONE_Q) & (lane1 < LANE_ONE_Q + 3), last, 0.0),
                                     axis=-1, keepdims=True)
                bound = reach[h] + neg_f_last - m_ref[h, :, 0:1]
                worst = bound if worst is None else jnp.maximum(worst, bound)
            return jnp.max(worst) > EXP2_UNDERFLOW

        for d in range(n_diag):
            step(pl.multiple_of(q0 + d * tk, tk), col + d * tk <= row, d * tk)

        def cond(st):
            return st[1]

        def body(st):
            kb = st[0]
            step(pl.multiple_of(kb * tk, tk), None)
            nxt = kb - 1
            return nxt, (nxt >= 0) & block_matters(jnp.maximum(nxt, 0))

        kb0 = qi * n_diag - 1
        lax.while_loop(cond, body, (kb0, (kb0 >= 0) & block_matters(jnp.maximum(kb0, 0))))

        for h, hl in enumerate(heads):
            acc = acc_ref[h]
            lane = lax.broadcasted_iota(jnp.int32, acc.shape, 1)
            l = jnp.sum(jnp.where(lane == LANE_ONE_V, acc, 0.0), axis=-1, keepdims=True)
            o_ref[0, pl.ds(q0, tq), hl] = _head_norm(acc / l, g_ref[...]).astype(o_ref.dtype)
        return carry

    lax.fori_loop(0, s_len // tq, q_tile, 0)


def _attn_call(body, name, q, k, v, g, tq):
    b, s, _ = q.shape
    nh = HEADS_PER_STEP
    spec = pl.BlockSpec((1, s, nh * LANES), lambda bi, hi: (bi, 0, hi))
    return pl.pallas_call(
        body,
        grid=(b, N_HEADS // nh),
        in_specs=[spec, spec, spec, pl.BlockSpec((1, LANES), lambda bi, hi: (0, 0))],
        out_specs=spec,
        out_shape=jax.ShapeDtypeStruct((b, s, HP), BF16),
        scratch_shapes=[pltpu.VMEM((nh, tq, LANES), F32), pltpu.VMEM((nh, tq, LANES), F32)],
        compiler_params=_cparams("arbitrary", "arbitrary"),
        name=name,
    )(q, k, v, g)


def _fox(q, k, v, g, tq, tk):
    return _attn_call(functools.partial(_fox_kernel, tq=tq, tk=tk), "fox", q, k, v, g, tq)


def _sb_kernel(q_ref, k_ref, v_ref, g_ref, o_ref, c_ref, acc_ref, *, tq, tk):
    s_len = q_ref.shape[1]
    n_diag = tq // tk
    jrow = lax.broadcasted_iota(jnp.int32, (tk, tk), 0)
    scol = lax.broadcasted_iota(jnp.int32, (tk, tk), 1)
    later = jnp.where(jrow > scol, 1.0, 0.0).astype(BF16)
    row = lax.broadcasted_iota(jnp.int32, (tq, tk), 0)
    col = lax.broadcasted_iota(jnp.int32, (tq, tk), 1)

    heads = _head_slices(q_ref)

    def q_tile(qi, carry):
        q0 = pl.multiple_of(qi * tq, tq)
        q = [q_ref[0, pl.ds(q0, tq), hl] for hl in heads]
        c_ref[...] = jnp.zeros_like(c_ref)
        acc_ref[...] = jnp.zeros_like(acc_ref)

        def step(start, mask, r0=0):
            for h, hl in enumerate(heads):
                k = k_ref[0, pl.ds(start, tk), hl]
                v = v_ref[0, pl.ds(start, tk), hl]
                z = _dot_nt(q[h][r0:], k)
                sp = _softplus2(z)
                spm = sp if mask is None else jnp.where(mask[r0:], sp, 0.0)
                c_prev = c_ref[h, r0:, :]
                rest = _dot(spm.astype(BF16), later) + jnp.concatenate([c_prev] * (tk // LANES), axis=1)
                a = jnp.exp2(z - sp - rest)
                if mask is not None:
                    a = jnp.where(mask[r0:], a, 0.0)
                acc_ref[h, r0:, :] += _dot(a.astype(BF16), v)
                c_ref[h, r0:, :] = c_prev + jnp.sum(spm, axis=-1, keepdims=True)

        def more_matters():
            least = None
            for h in range(len(heads)):
                ch = c_ref[h, :, 0:1]
                least = ch if least is None else jnp.minimum(least, ch)
            return jnp.min(least) < -EXP2_UNDERFLOW

        for d in reversed(range(n_diag)):
            step(pl.multiple_of(q0 + d * tk, tk), col + d * tk < row, d * tk)

        def cond(st):
            return st[1]

        def body(st):
            kb = st[0]
            step(pl.multiple_of(kb * tk, tk), None)
            nxt = kb - 1
            return nxt, (nxt >= 0) & more_matters()

        kb0 = qi * n_diag - 1
        lax.while_loop(cond, body, (kb0, (kb0 >= 0) & more_matters()))
        for h, hl in enumerate(heads):
            o_ref[0, pl.ds(q0, tq), hl] = _head_norm(acc_ref[h], g_ref[...]).astype(o_ref.dtype)
        return carry

    lax.fori_loop(0, s_len // tq, q_tile, 0)


def _sb(q, k, v, g, tq, tk):
    return _attn_call(functools.partial(_sb_kernel, tq=tq, tk=tk), "sb", q, k, v, g, tq)


def _outproj_kernel(x_ref, fo_ref, so_ref, mod_ref, g_ref, wo_ref, wc_ref,
                    x1_ref, h2t_ref, sc_ref):
    gt1 = mod_ref[0, 2:3, :]
    sh2 = mod_ref[0, 3:4, :]
    sc2 = mod_ref[0, 4:5, :]
    mixed = _dot(fo_ref[0], wo_ref[0:HP, :]) + _dot(so_ref[0], wo_ref[HP:2 * HP, :])
    x1 = x_ref[0] + gt1 * mixed
    x1_ref[0] = x1
    inv = lax.rsqrt(jnp.mean(x1 * x1, axis=-1, keepdims=True) + EPS)
    h2 = (x1 * inv) * g_ref[...] * (1.0 + sc2) + sh2
    th = h2.T.astype(BF16)
    h2t_ref[...] = th
    sc_ref[...] = _dot(wc_ref[...], th)


def _outproj(x, fo, so, mod3, g_ffn, w_out_p, wc, tm):
    b, s, d = x.shape
    t = b * s
    nsc = wc.shape[0]
    const = lambda shape: pl.BlockSpec(shape, lambda bi, si: (0,) * len(shape),
                                       pipeline_mode=pl.Buffered(1))
    spb = s // tm
    return pl.pallas_call(
        _outproj_kernel,
        grid=(b, spb),
        in_specs=[pl.BlockSpec((1, tm, d), lambda bi, si: (bi, si, 0)),
                  pl.BlockSpec((1, tm, HP), lambda bi, si: (bi, si, 0)),
                  pl.BlockSpec((1, tm, HP), lambda bi, si: (bi, si, 0)),
                  pl.BlockSpec((1, N_MOD, d), lambda bi, si: (bi, 0, 0)),
                  const((1, d)),
                  const((2 * HP, d)),
                  const((nsc, d))],
        out_specs=[pl.BlockSpec((1, tm, d), lambda bi, si: (bi, si, 0)),
                   pl.BlockSpec((d, tm), lambda bi, si: (0, bi * spb + si)),
                   pl.BlockSpec((nsc, tm), lambda bi, si: (0, bi * spb + si))],
        out_shape=[jax.ShapeDtypeStruct((b, s, d), F32),
                   jax.ShapeDtypeStruct((d, t), BF16),
                   jax.ShapeDtypeStruct((nsc, t), F32)],
        compiler_params=_cparams("arbitrary", "arbitrary"),
        name="outproj",
    )(x, fo, so, mod3, g_ffn, w_out_p, wc)


def _top16(s, exact):
    n, t = s.shape
    idx = lax.broadcasted_iota(jnp.int32, (n, t), 0).astype(F32)
    krow = lax.broadcasted_iota(jnp.int32, (PEER_TOPK, t), 0)
    rank = jnp.full((n, t), float(PEER_TOPK), F32)
    vals = jnp.zeros((PEER_TOPK, t), F32)
    for k in range(PEER_TOPK):
        m = jnp.max(s, axis=0, keepdims=True)
        hit = s == m
        if exact:
            hit = idx == jnp.min(jnp.where(hit, idx, float(n)), axis=0, keepdims=True)
        rank = jnp.where(hit, float(k), rank)
        s = jnp.where(hit, NEG_INF, s)
        vals = jnp.where(krow == k, m, vals)
    return vals, rank


SUBLANES = 8
N_CAND = PEER_TOPK + (SUBLANES - 1) * SUBLANES + SUBLANES


def _cand_flat_pos():
    r = lax.broadcasted_iota(jnp.int32, (N_CAND, LANES), 0)
    mid_lo, hi_lo = PEER_TOPK, N_CAND - SUBLANES
    sub_shift = SUBLANES.bit_length() - 1
    mid = (jnp.right_shift(r - mid_lo, sub_shift) + 1) * PEER_TOPK + jnp.bitwise_and(r - mid_lo, SUBLANES - 1)
    hi = (r - hi_lo + SUBLANES) * PEER_TOPK
    return jnp.where(r < mid_lo, r, jnp.where(r < hi_lo, mid, hi)).astype(F32)


def _pair_top16(v1, v2, r1, exact):
    cand = jnp.concatenate(
        [v1[0:1, :] + v2]
        + [v1[a:a + 1, :] + v2[0:SUBLANES, :] for a in range(1, SUBLANES)]
        + [v1[SUBLANES:, :] + v2[0:1, :]], axis=0)
    pos = _cand_flat_pos() if exact else None
    c0 = cand[0:1, :]
    zsum = jnp.zeros((1, LANES), F32)
    for k in range(PEER_TOPK):
        m = jnp.max(cand, axis=0, keepdims=True)
        hit = cand == m
        if exact:
            hit = pos == jnp.min(jnp.where(hit, pos, float(PEER_TOPK * PEER_TOPK)), axis=0, keepdims=True)
        cand = jnp.where(hit, NEG_INF, cand)
        zsum = zsum + jnp.exp(m - c0)
    taken = jnp.where(cand == NEG_INF, 1.0, 0.0)
    n_taken = jnp.sum(taken, axis=0, keepdims=True)
    lrow = jnp.zeros(r1.shape, F32)
    for a in range(PEER_TOPK):
        if a == 0:
            cnt_a = jnp.sum(taken[0:PEER_TOPK, :], axis=0, keepdims=True)
        elif a < SUBLANES:
            lo = PEER_TOPK + SUBLANES * (a - 1)
            cnt_a = jnp.sum(taken[lo:lo + SUBLANES, :], axis=0, keepdims=True)
        else:
            lo = N_CAND - SUBLANES + a - SUBLANES
            cnt_a = taken[lo:lo + 1, :]
        lrow = jnp.where(r1 == float(a), cnt_a, lrow)
    return lrow, zsum, n_taken


def _topk_kernel(sc_ref, r2_ref, e2_ref, l1_ref, e1_ref):
    n = PEER_N_KEYS
    tl = sc_ref.shape[1]
    ngrp = tl // LANES

    def run(item, exact):
        hd = lax.div(item, ngrp)
        gi = lax.rem(item, ngrp)
        lanes = pl.ds(pl.multiple_of(gi * LANES, LANES), LANES)
        s1 = sc_ref[pl.ds(pl.multiple_of(hd * 2 * n, n), n), lanes]
        s2 = sc_ref[pl.ds(pl.multiple_of(hd * 2 * n + n, n), n), lanes]
        v1, r1 = _top16(s1, exact)
        v2, r2 = _top16(s2, exact)
        lrow, zsum, n_taken = _pair_top16(v1, v2, r1, exact)
        r2_ref[hd, :, lanes] = pltpu.bitcast(r2.astype(BF16), jnp.uint32)
        e2_ref[hd, :, lanes] = pltpu.bitcast(jnp.exp(s2 - v2[0:1, :]).astype(BF16), jnp.uint32)
        l1_ref[hd, :, lanes] = _dup_bf16_words(lrow)
        e1_ref[hd, :, lanes] = _dup_bf16_words(jnp.exp(s1 - v1[0:1, :]) / zsum)
        k = float(PEER_TOPK)
        picked = lambda r: jnp.sum(jnp.where(r < k, 1.0, 0.0), axis=0, keepdims=True)
        return jnp.abs(picked(r1) - k) + jnp.abs(picked(r2) - k) + jnp.abs(n_taken - k)

    together = 2 if (PEER_HEADS * ngrp) % 2 == 0 else 1

    def body(it, carry):
        items = [it * together + u for u in range(together)]
        excess = [run(item, exact=False) for item in items]
        for item, ex in zip(items, excess):
            @pl.when(jnp.max(ex) > 0.0)
            def _():
                run(item, exact=True)

        return carry

    lax.fori_loop(0, PEER_HEADS * ngrp // together, body, 0)


def _topk(scores_t, tl):
    nsc, t = scores_t.shape
    spec = lambda rows: pl.BlockSpec((PEER_HEADS, rows, tl), lambda i: (0, 0, i))
    sds = lambda rows: jax.ShapeDtypeStruct((PEER_HEADS, rows, t), jnp.uint32)
    half, full = PEER_N_KEYS // 2, PEER_N_KEYS
    return pl.pallas_call(
        _topk_kernel,
        grid=(t // tl,),
        in_specs=[pl.BlockSpec((nsc, tl), lambda i: (0, i))],
        out_specs=[spec(half), spec(half), spec(full), spec(full)],
        out_shape=[sds(half), sds(half), sds(full), sds(full)],
        compiler_params=_cparams("arbitrary"),
        name="topk",
    )(scores_t)


def _gelu(x):
    return 0.5 * x * (1.0 + lax.erf(x * (2.0 ** -0.5)))


def _dup_bf16_words(x):
    bits = pltpu.bitcast(x.astype(BF16).astype(F32), jnp.uint32)
    return bits | (bits >> 16)


def _bf16_rows(words, n):
    return pltpu.bitcast(jnp.broadcast_to(words, (n // 2, words.shape[1])), BF16)


def _peer_kernel(h2t_ref, dn_ref, up_ref, r2_ref, e2_ref, l1_ref, e1_ref, x1_ref, mod_ref, gf_ref,
                 o_ref, acc_ref, st_ref, pw_ref, *, rows_per_chunk):
    j = pl.program_id(2)
    n = PEER_N_KEYS

    @pl.when(j == 0)
    def _():
        acc_ref[...] = jnp.zeros_like(acc_ref)

    st_ref[...] = _dot(dn_ref[...], h2t_ref[...])
    zero = jnp.zeros((n, LANES), BF16)
    pair = 2
    for g in range(h2t_ref.shape[1] // LANES):
        lanes = slice(g * LANES, (g + 1) * LANES)
        for i1 in range(0, rows_per_chunk, pair):
            w = [None] * pair
            for hd in range(PEER_HEADS):
                r2 = pltpu.bitcast(r2_ref[hd, :, lanes], BF16)
                e2 = pltpu.bitcast(e2_ref[hd, :, lanes], BF16)
                for u in range(pair):
                    cnt = _bf16_rows(l1_ref[hd, i1 + u:i1 + u + 1, lanes], n)
                    e1 = _bf16_rows(e1_ref[hd, i1 + u:i1 + u + 1, lanes], n)
                    wh = jnp.where(r2 < cnt, e2, zero) * e1
                    w[u] = wh if w[u] is None else w[u] + wh
            for u in range(pair):
                rows = slice((i1 + u) * n, (i1 + u + 1) * n)
                pw_ref[rows, lanes] = _gelu(st_ref[rows, lanes].astype(BF16)) * w[u]
    acc_ref[...] += _dot(up_ref[...], pw_ref[...])

    @pl.when(j == pl.num_programs(2) - 1)
    def _():
        gt2 = mod_ref[0, 5:6, :]
        x2 = x1_ref[0] + gt2 * acc_ref[...].T
        inv = lax.rsqrt(jnp.mean(x2 * x2, axis=-1, keepdims=True) + EPS)
        o_ref[0] = (x2 * inv) * gf_ref[...]


def _peer(h2t, down, up_t, r2, e2, l1, e1, x1, mod3, g_final, tt, ec):
    b, s, d = x1.shape
    ne = down.shape[0]
    rpc = ec // PEER_N_KEYS
    spb = s // tt
    tok = lambda bi, si, j: (0, 0, bi * spb + si)
    gate_tab = pl.BlockSpec((PEER_HEADS, rpc, tt), lambda bi, si, j: (0, j, bi * spb + si))
    return pl.pallas_call(
        functools.partial(_peer_kernel, rows_per_chunk=rpc),
        grid=(b, spb, ne // ec),
        in_specs=[pl.BlockSpec((d, tt), lambda bi, si, j: (0, bi * spb + si)),
                  pl.BlockSpec((ec, d), lambda bi, si, j: (j, 0)),
                  pl.BlockSpec((d, ec), lambda bi, si, j: (0, j)),
                  pl.BlockSpec((PEER_HEADS, PEER_N_KEYS // 2, tt), tok),
                  pl.BlockSpec((PEER_HEADS, PEER_N_KEYS // 2, tt), tok),
                  gate_tab, gate_tab,
                  pl.BlockSpec((1, tt, d), lambda bi, si, j: (bi, si, 0)),
                  pl.BlockSpec((1, N_MOD, d), lambda bi, si, j: (bi, 0, 0)),
                  pl.BlockSpec((1, d), lambda bi, si, j: (0, 0))],
        out_specs=pl.BlockSpec((1, tt, d), lambda bi, si, j: (bi, si, 0)),
        out_shape=jax.ShapeDtypeStruct((b, s, d), F32),
        scratch_shapes=[pltpu.VMEM((d, tt), F32), pltpu.VMEM((ec, tt), F32), pltpu.VMEM((ec, tt), BF16)],
        compiler_params=_cparams("arbitrary", "arbitrary", "arbitrary"),
        name="peer",
    )(h2t, down, up_t, r2, e2, l1, e1, x1, mod3, g_final)


def _pad_heads(w):
    d = w.shape[0]
    w = w.reshape(d, N_HEADS, HEAD_DIM)
    return jnp.pad(w, ((0, 0), (0, 0), (0, LANES - HEAD_DIM))).reshape(d, HP)


def _bias_constants():
    pq = np.zeros((3 * LANES, HP), np.float32)
    pk = np.zeros((3 * LANES, HP), np.float32)
    cq = np.zeros((1, HP), np.float32)
    ck = np.zeros((1, HP), np.float32)
    cv = np.zeros((1, HP), np.float32)
    for h in range(N_HEADS):
        for part in range(3):
            pq[part * LANES + h, h * LANES + LANE_F_Q + part] = 1.0
            pk[part * LANES + h, h * LANES + LANE_ONE_Q + part] = -1.0
            cq[0, h * LANES + LANE_ONE_Q + part] = 1.0
            ck[0, h * LANES + LANE_F_Q + part] = 1.0
        cv[0, h * LANES + LANE_ONE_V] = 1.0
    return (jnp.asarray(pq, BF16), jnp.asarray(pk, BF16), jnp.asarray(cq), jnp.asarray(ck), jnp.asarray(cv))


def _tile(n, pref):
    return pref if n % pref == 0 else n


def kernel(x, c, w_ada, b_ada, g_norm_mix, w_in, b_forget, g_out_fox, g_out_sb, w_out, g_norm_ffn,
           w_query, sub_keys, expert_down, expert_up, g_final):
    b, s, d = x.shape
    assert w_ada.shape[0] == 1, "single-layer block: the final RMSNorm is fused into the PEER kernel"
    fw = N_HEADS * HEAD_DIM
    scale = HEAD_DIM ** -0.5 * LOG2E
    pq, pk, cq, ck, cv = _bias_constants()
    tm = _tile(s, 512)
    c_pad = jnp.pad(c, ((0, -b % 8), (0, 0)))

    for l in range(1):
        mod = _ada(c_pad, w_ada[l], b_ada[l][None, :])[:b]
        mod3 = mod.reshape(b, N_MOD, d)

        wi = w_in[l]
        o1 = 3 * fw
        o2 = o1 + N_HEADS
        w_all = jnp.concatenate([
            _pad_heads(wi[:, 0:fw] * scale), _pad_heads(wi[:, fw:2 * fw]), _pad_heads(wi[:, 2 * fw:o1]),
            _pad_heads(wi[:, o2:o2 + fw] * scale), _pad_heads(wi[:, o2 + fw:o2 + 2 * fw]),
            _pad_heads(wi[:, o2 + 2 * fw:o2 + 3 * fw])], axis=1).astype(BF16)
        wf = jnp.pad(wi[:, o1:o2], ((0, 0), (0, LANES - N_HEADS)))
        wf_hi = wf.astype(BF16)
        wf_lo = (wf - wf_hi.astype(F32)).astype(BF16)
        b_f = jnp.pad(b_forget[l], (0, LANES - N_HEADS))[None, :]

        fq, fk, fv, sq, sk, sv = _inproj(x, mod3, g_norm_mix[l][None, :], w_all, wf_hi, wf_lo, b_f,
                                         pq, pk, cq, ck, cv, tm)

        g_fox = jnp.pad(g_out_fox[l], (0, LANES - HEAD_DIM))[None, :]
        g_sb = jnp.pad(g_out_sb[l], (0, LANES - HEAD_DIM))[None, :]
        fo = _fox(fq, fk, fv, g_fox, _tile(s, 512), _tile(s, 256))
        so = _sb(sq, sk, sv, g_sb, _tile(s, 512), _tile(s, 256))

        wo = w_out[l]
        w_out_p = jnp.pad(wo.reshape(2 * N_HEADS, HEAD_DIM, d),
                          ((0, 0), (0, LANES - HEAD_DIM), (0, 0))).reshape(2 * HP, d).astype(BF16)
        wc = _wc(sub_keys[l].reshape(PEER_HEADS * 2, PEER_N_KEYS, LANES), w_query[l])
        x1, h2t, scores_t = _outproj(x, fo, so, mod3, g_norm_ffn[l][None, :], w_out_p, wc.astype(BF16), tm)

        r2, e2, l1, e1 = _topk(scores_t, _tile(b * s, 512))
        x = _peer(h2t, expert_down[l].astype(BF16), expert_up[l].T.astype(BF16), r2, e2, l1, e1,
                  x1, mod3, g_final[None, :], _tile(s, 1024), 1024)
    return x
```

```python
import functools

import numpy as np
import jax
import jax.numpy as jnp
from jax import lax
from jax.experimental import pallas as pl
from jax.experimental.pallas import tpu as pltpu

F32 = jnp.float32
BF16 = jnp.bfloat16

HEAD_DIM = 64
N_HEADS = 8
PEER_HEADS = 8
PEER_N_KEYS = 128
PEER_TOPK = 16
N_MOD = 6
EPS = 1e-6

LANES = 128
VMEM_LIMIT = 56 * 1024 * 1024

HP = N_HEADS * LANES
SBW = N_HEADS * HEAD_DIM
LANE_ONE_Q = HEAD_DIM
LANE_F_Q = HEAD_DIM + 3
LANE_ONE_V = HEAD_DIM
NEG_INF = float("-inf")
LOG2E = 1.4426950408889634
EXP2_UNDERFLOW = -153.0


def _cparams(*sem, flags=None):
    return pltpu.CompilerParams(dimension_semantics=sem, vmem_limit_bytes=VMEM_LIMIT, flags=flags)


def _dot(a, b):
    return jnp.dot(a, b, preferred_element_type=F32)


def _dot_nt(a, b):
    return lax.dot_general(a, b, (((1,), (1,)), ((), ())), preferred_element_type=F32)


def _split2(x):
    hi = x.astype(BF16)
    lo = (x - hi.astype(F32)).astype(BF16)
    return hi, lo


def _split3(x):
    hi = x.astype(BF16)
    r = x - hi.astype(F32)
    mid = r.astype(BF16)
    lo = (r - mid.astype(F32)).astype(BF16)
    return hi, mid, lo


def _softplus2(z):
    return jnp.maximum(z, 0.0) + jnp.log2(1.0 + jnp.exp2(-jnp.abs(z)))


def _ada_kernel(c_ref, w_ref, b_ref, o_ref):
    c = c_ref[...]
    a = c / (1.0 + jnp.exp(-c))
    ah, al = _split2(a)
    wh, wl = _split2(w_ref[...])
    o_ref[...] = _dot(ah, wh) + _dot(ah, wl) + _dot(al, wh) + b_ref[...]


def _ada(c_pad, w_ada, b_ada):
    rows, d = c_pad.shape
    n = w_ada.shape[1]
    bn = 1024
    return pl.pallas_call(
        _ada_kernel,
        grid=(n // bn,),
        in_specs=[pl.BlockSpec((rows, d), lambda j: (0, 0)),
                  pl.BlockSpec((d, bn), lambda j: (0, j)),
                  pl.BlockSpec((1, bn), lambda j: (0, j))],
        out_specs=pl.BlockSpec((rows, bn), lambda j: (0, j)),
        out_shape=jax.ShapeDtypeStruct((rows, n), F32),
        compiler_params=_cparams("arbitrary"),
        name="ada",
    )(c_pad, w_ada, b_ada)


def _wc_kernel(sk_ref, wq_ref, o_ref):
    sh, sl = _split2(sk_ref[0])
    wh, wl = _split2(wq_ref[...])
    o_ref[...] = _dot_nt(sh, wh) + _dot_nt(sh, wl) + _dot_nt(sl, wh)


def _wc(sub_keys, w_query):
    d = w_query.shape[0]
    nhp = sub_keys.shape[0]
    return pl.pallas_call(
        _wc_kernel,
        grid=(nhp,),
        in_specs=[pl.BlockSpec((1, PEER_N_KEYS, LANES), lambda i: (i, 0, 0)),
                  pl.BlockSpec((d, LANES), lambda i: (0, i))],
        out_specs=pl.BlockSpec((PEER_N_KEYS, d), lambda i: (i, 0)),
        out_shape=jax.ShapeDtypeStruct((nhp * PEER_N_KEYS, d), F32),
        compiler_params=_cparams("arbitrary"),
        name="wc",
    )(sub_keys, w_query)


def _inproj_kernel(x_ref, mod_ref, g_ref, w_ref, wfh_ref, wfl_ref, bf_ref, pq_ref, pk_ref,
                   cq_ref, ck_ref, cv_ref,
                   fq_ref, fk_ref, fv_ref, sq_ref, sk_ref, sv_ref, carry_ref):
    tm = x_ref.shape[1]

    @pl.when(pl.program_id(1) == 0)
    def _():
        carry_ref[...] = jnp.zeros_like(carry_ref)

    x = x_ref[0]
    inv = lax.rsqrt(jnp.mean(x * x, axis=-1, keepdims=True) + EPS)
    sh1 = mod_ref[0, 0:1, :]
    sc1 = mod_ref[0, 1:2, :]
    h = (x * inv) * g_ref[...] * (1.0 + sc1) + sh1
    hh, hl = _split2(h)

    fl = _dot(hh, wfh_ref[...]) + _dot(hh, wfl_ref[...]) + _dot(hl, wfh_ref[...]) + bf_ref[...]
    lf = jnp.minimum(fl, 0.0) - jnp.log(1.0 + jnp.exp(-jnp.abs(fl)))
    row = lax.broadcasted_iota(jnp.int32, (tm, tm), 0)
    col = lax.broadcasted_iota(jnp.int32, (tm, tm), 1)
    tri = jnp.where(col <= row, 1.0, 0.0).astype(BF16)
    l0, l1, l2 = _split3(lf)
    fcum = _dot(tri, l0) + _dot(tri, l1) + _dot(tri, l2) + carry_ref[...]
    carry_ref[...] = fcum[tm - 1:tm, :]
    f0, f1, f2 = _split3(fcum * LOG2E)
    fcat = jnp.concatenate([f0, f1, f2], axis=1)

    fq_ref[0] = (_dot(hh, w_ref[:, 0 * HP:1 * HP]) + _dot(fcat, pq_ref[...]) + cq_ref[...]).astype(BF16)
    fk_ref[0] = (_dot(hh, w_ref[:, 1 * HP:2 * HP]) + _dot(fcat, pk_ref[...]) + ck_ref[...]).astype(BF16)
    fv_ref[0] = (_dot(hh, w_ref[:, 2 * HP:3 * HP]) + cv_ref[...]).astype(BF16)
    sq_ref[0] = _dot(hh, w_ref[:, 3 * HP + 0 * SBW:3 * HP + 1 * SBW]).astype(BF16)
    sk_ref[0] = _dot(hh, w_ref[:, 3 * HP + 1 * SBW:3 * HP + 2 * SBW]).astype(BF16)
    sv_ref[0] = _dot(hh, w_ref[:, 3 * HP + 2 * SBW:3 * HP + 3 * SBW]).astype(BF16)


def _inproj(x, mod3, g_mix, w_all, wf_hi, wf_lo, b_f, pq, pk, cq, ck, cv, tm):
    b, s, d = x.shape
    const = lambda shape: pl.BlockSpec(shape, lambda bi, si: (0,) * len(shape),
                                       pipeline_mode=pl.Buffered(1))
    out_spec = lambda w: pl.BlockSpec((1, tm, w), lambda bi, si: (bi, si, 0))
    out_sds = lambda w: jax.ShapeDtypeStruct((b, s, w), BF16)
    widths = [HP] * 3 + [SBW] * 3
    return pl.pallas_call(
        _inproj_kernel,
        grid=(b, s // tm),
        in_specs=[pl.BlockSpec((1, tm, d), lambda bi, si: (bi, si, 0)),
                  pl.BlockSpec((1, N_MOD, d), lambda bi, si: (bi, 0, 0)),
                  const((1, d)),
                  const((d, sum(widths))),
                  const((d, LANES)), const((d, LANES)), const((1, LANES)),
                  const((3 * LANES, HP)), const((3 * LANES, HP)),
                  const((1, HP)), const((1, HP)), const((1, HP))],
        out_specs=[out_spec(w) for w in widths],
        out_shape=[out_sds(w) for w in widths],
        scratch_shapes=[pltpu.VMEM((1, LANES), F32)],
        compiler_params=_cparams("arbitrary", "arbitrary"),
        name="inproj",
    )(x, mod3, g_mix, w_all, wf_hi, wf_lo, b_f, pq, pk, cq, ck, cv)


def _head_norm(o, g, lo=0):
    lane = lax.broadcasted_iota(jnp.int32, o.shape, 1)
    o = jnp.where((lane >= lo) & (lane < lo + HEAD_DIM), o, 0.0)
    ms = jnp.sum(o * o, axis=-1, keepdims=True) * (1.0 / HEAD_DIM)
    return o * lax.rsqrt(ms + EPS) * g


HEADS_PER_STEP = 2


def _head_slices(ref):
    return [slice(h * LANES, (h + 1) * LANES) for h in range(ref.shape[2] // LANES)]


def _fox_kernel(q_ref, k_ref, v_ref, g_ref, o_ref, m_ref, acc_ref, *, tq, tk):
    s_len = q_ref.shape[1]
    heads = _head_slices(q_ref)
    n_diag = tq // tk
    lane1 = lax.broadcasted_iota(jnp.int32, (1, LANES), 1)
    head_lanes = lane1 < HEAD_DIM
    row = lax.broadcasted_iota(jnp.int32, (tq, tk), 0)
    col = lax.broadcasted_iota(jnp.int32, (tq, tk), 1)

    def knorm(i, mx):
        out = []
        for hl, m in zip(heads, mx):
            kb = k_ref[0, pl.ds(pl.multiple_of(i * tk, tk), tk), hl].astype(F32)
            kb = jnp.where(head_lanes, kb, 0.0)
            n2 = jnp.sum(kb * kb, axis=-1, keepdims=True)
            out.append(jnp.maximum(m, jnp.max(n2, axis=0, keepdims=True)))
        return tuple(out)

    kmax = [jnp.sqrt(m) for m in
            lax.fori_loop(0, s_len // tk, knorm, tuple(jnp.zeros((1, 1), F32) for _ in heads))]

    def q_tile(qi, carry):
        q0 = pl.multiple_of(qi * tq, tq)
        q = [q_ref[0, pl.ds(q0, tq), hl] for hl in heads]
        reach = []
        for h, qh in enumerate(q):
            qf = qh.astype(F32)
            qn = jnp.sqrt(jnp.sum(jnp.where(head_lanes, qf * qf, 0.0), axis=-1, keepdims=True))
            f_t = jnp.sum(jnp.where((lane1 >= LANE_F_Q) & (lane1 < LANE_F_Q + 3), qf, 0.0),
                          axis=-1, keepdims=True)
            reach.append(qn * kmax[h] + f_t)
        m_ref[...] = jnp.full_like(m_ref, NEG_INF)
        acc_ref[...] = jnp.zeros_like(acc_ref)

        def step(start, mask, r0=0):
            for h, hl in enumerate(heads):
                k = k_ref[0, pl.ds(start, tk), hl]
                v = v_ref[0, pl.ds(start, tk), hl]
                s = _dot_nt(q[h][r0:], k)
                if mask is not None:
                    s = jnp.where(mask[r0:], s, NEG_INF)
                m_prev = m_ref[h, r0:, :]
                m_new = jnp.maximum(m_prev, jnp.max(s, axis=-1, keepdims=True))
                p = jnp.exp2(s - jnp.concatenate([m_new] * (tk // LANES), axis=1))
                acc_ref[h, r0:, :] = jnp.exp2(m_prev - m_new) * acc_ref[h, r0:, :] + _dot(p.astype(BF16), v)
                m_ref[h, r0:, :] = m_new

        def block_matters(kb):
            worst = None
            for h, hl in enumerate(heads):
                last = k_ref[0, pl.ds(pl.multiple_of(kb * tk + tk - 16, 16), 16), hl][15:16, :].astype(F32)
                neg_f_last = jnp.sum(jnp.where((lane1 >= LANE_ONE_Q) & (lane1 < LANE_ONE_Q + 3), last, 0.0),
                                     axis=-1, keepdims=True)
                bound = reach[h] + neg_f_last - m_ref[h, :, 0:1]
                worst = bound if worst is None else jnp.maximum(worst, bound)
            return jnp.max(worst) > EXP2_UNDERFLOW

        for d in range(n_diag):
            step(pl.multiple_of(q0 + d * tk, tk), col + d * tk <= row, d * tk)

        def cond(st):
            return st[1]

        def body(st):
            kb = st[0]
            step(pl.multiple_of(kb * tk, tk), None)
            nxt = kb - 1
            return nxt, (nxt >= 0) & block_matters(jnp.maximum(nxt, 0))

        kb0 = qi * n_diag - 1
        lax.while_loop(cond, body, (kb0, (kb0 >= 0) & block_matters(jnp.maximum(kb0, 0))))

        for h, hl in enumerate(heads):
            acc = acc_ref[h]
            lane = lax.broadcasted_iota(jnp.int32, acc.shape, 1)
            l = jnp.sum(jnp.where(lane == LANE_ONE_V, acc, 0.0), axis=-1, keepdims=True)
            o_ref[0, pl.ds(q0, tq), hl] = _head_norm(acc / l, g_ref[...]).astype(o_ref.dtype)
        return carry

    lax.fori_loop(0, s_len // tq, q_tile, 0)


def _attn_call(body, name, q, k, v, g, tq, width, nh):
    b, s, total = q.shape
    spec = pl.BlockSpec((1, s, width), lambda bi, hi: (bi, 0, hi))
    return pl.pallas_call(
        body,
        grid=(b, total // width),
        in_specs=[spec, spec, spec, pl.BlockSpec((1, LANES), lambda bi, hi: (0, 0))],
        out_specs=spec,
        out_shape=jax.ShapeDtypeStruct((b, s, total), BF16),
        scratch_shapes=[pltpu.VMEM((nh, tq, LANES), F32), pltpu.VMEM((nh, tq, LANES), F32)],
        compiler_params=_cparams("arbitrary", "arbitrary"),
        name=name,
    )(q, k, v, g)


def _fox(q, k, v, g, tq, tk):
    return _attn_call(functools.partial(_fox_kernel, tq=tq, tk=tk), "fox", q, k, v, g, tq,
                      HEADS_PER_STEP * LANES, HEADS_PER_STEP)


def _sb_kernel(q_ref, k_ref, v_ref, g_ref, o_ref, c_ref, acc_ref, *, tq, tk):
    s_len = q_ref.shape[1]
    n_diag = tq // tk
    jrow = lax.broadcasted_iota(jnp.int32, (tk, tk), 0)
    scol = lax.broadcasted_iota(jnp.int32, (tk, tk), 1)
    later = jnp.where(jrow > scol, 1.0, 0.0).astype(BF16)
    row = lax.broadcasted_iota(jnp.int32, (tq, tk), 0)
    col = lax.broadcasted_iota(jnp.int32, (tq, tk), 1)

    n_heads = LANES // HEAD_DIM
    lane_q = lax.broadcasted_iota(jnp.int32, (tq, LANES), 1)
    own = [(lane_q >= h * HEAD_DIM) & (lane_q < (h + 1) * HEAD_DIM) for h in range(n_heads)]

    def q_tile(qi, carry):
        q0 = pl.multiple_of(qi * tq, tq)
        q_all = q_ref[0, pl.ds(q0, tq), :]
        q = [jnp.where(own[h], q_all, jnp.zeros_like(q_all)) for h in range(n_heads)]
        c_ref[...] = jnp.zeros_like(c_ref)
        acc_ref[...] = jnp.zeros_like(acc_ref)

        def step(start, mask, r0=0):
            k = k_ref[0, pl.ds(start, tk), :]
            v = v_ref[0, pl.ds(start, tk), :]
            for h in range(n_heads):
                z = _dot_nt(q[h][r0:], k)
                sp = _softplus2(z)
                spm = sp if mask is None else jnp.where(mask[r0:], sp, 0.0)
                c_prev = c_ref[h, r0:, :]
                rest = _dot(spm.astype(BF16), later) + jnp.concatenate([c_prev] * (tk // LANES), axis=1)
                a = jnp.exp2(z - sp - rest)
                if mask is not None:
                    a = jnp.where(mask[r0:], a, 0.0)
                acc_ref[h, r0:, :] += _dot(a.astype(BF16), v)
                c_ref[h, r0:, :] = c_prev + jnp.sum(spm, axis=-1, keepdims=True)

        def more_matters():
            least = None
            for h in range(n_heads):
                ch = c_ref[h, :, 0:1]
                least = ch if least is None else jnp.minimum(least, ch)
            return jnp.min(least) < -EXP2_UNDERFLOW

        for d in reversed(range(n_diag)):
            step(pl.multiple_of(q0 + d * tk, tk), col + d * tk < row, d * tk)

        def cond(st):
            return st[1]

        def body(st):
            kb = st[0]
            step(pl.multiple_of(kb * tk, tk), None)
            nxt = kb - 1
            return nxt, (nxt >= 0) & more_matters()

        kb0 = qi * n_diag - 1
        lax.while_loop(cond, body, (kb0, (kb0 >= 0) & more_matters()))
        out = None
        for h in range(n_heads):
            oh = _head_norm(acc_ref[h], g_ref[...], h * HEAD_DIM)
            out = oh if out is None else out + oh
        o_ref[0, pl.ds(q0, tq), :] = out.astype(o_ref.dtype)
        return carry

    lax.fori_loop(0, s_len // tq, q_tile, 0)


def _sb(q, k, v, g, tq, tk):
    return _attn_call(functools.partial(_sb_kernel, tq=tq, tk=tk), "sb", q, k, v, g, tq, LANES,
                      LANES // HEAD_DIM)


def _outproj_kernel(x_ref, fo_ref, so_ref, mod_ref, g_ref, wo_ref, wc_ref,
                    x1_ref, h2t_ref, sc_ref):
    gt1 = mod_ref[0, 2:3, :]
    sh2 = mod_ref[0, 3:4, :]
    sc2 = mod_ref[0, 4:5, :]
    mixed = _dot(fo_ref[0], wo_ref[0:HP, :]) + _dot(so_ref[0], wo_ref[HP:HP + SBW, :])
    x1 = x_ref[0] + gt1 * mixed
    x1_ref[0] = x1
    inv = lax.rsqrt(jnp.mean(x1 * x1, axis=-1, keepdims=True) + EPS)
    h2 = (x1 * inv) * g_ref[...] * (1.0 + sc2) + sh2
    th = h2.T.astype(BF16)
    h2t_ref[...] = th
    sc_ref[...] = _dot(wc_ref[...], th)


def _outproj(x, fo, so, mod3, g_ffn, w_out_p, wc, tm):
    b, s, d = x.shape
    t = b * s
    nsc = wc.shape[0]
    const = lambda shape: pl.BlockSpec(shape, lambda bi, si: (0,) * len(shape),
                                       pipeline_mode=pl.Buffered(1))
    spb = s // tm
    return pl.pallas_call(
        _outproj_kernel,
        grid=(b, spb),
        in_specs=[pl.BlockSpec((1, tm, d), lambda bi, si: (bi, si, 0)),
                  pl.BlockSpec((1, tm, HP), lambda bi, si: (bi, si, 0)),
                  pl.BlockSpec((1, tm, SBW), lambda bi, si: (bi, si, 0)),
                  pl.BlockSpec((1, N_MOD, d), lambda bi, si: (bi, 0, 0)),
                  const((1, d)),
                  const((HP + SBW, d)),
                  const((nsc, d))],
        out_specs=[pl.BlockSpec((1, tm, d), lambda bi, si: (bi, si, 0)),
                   pl.BlockSpec((d, tm), lambda bi, si: (0, bi * spb + si)),
                   pl.BlockSpec((nsc, tm), lambda bi, si: (0, bi * spb + si))],
        out_shape=[jax.ShapeDtypeStruct((b, s, d), F32),
                   jax.ShapeDtypeStruct((d, t), BF16),
                   jax.ShapeDtypeStruct((nsc, t), F32)],
        compiler_params=_cparams("arbitrary", "arbitrary"),
        name="outproj",
    )(x, fo, so, mod3, g_ffn, w_out_p, wc)


TAKEN = -(2.0 ** 126)
TAKEN_STEP = 2.0 ** -10


def _top16(s, exact):
    n, t = s.shape
    idx = lax.broadcasted_iota(jnp.int32, (n, t), 0).astype(F32)
    krow = lax.broadcasted_iota(jnp.int32, (PEER_TOPK, t), 0)
    vals = jnp.zeros((PEER_TOPK, t), F32)
    for k in range(PEER_TOPK):
        m = jnp.max(s, axis=0, keepdims=True)
        hit = s == m
        if exact:
            hit = idx == jnp.min(jnp.where(hit, idx, float(n)), axis=0, keepdims=True)
        s = jnp.where(hit, TAKEN * (1.0 + k * TAKEN_STEP), s)
        vals = jnp.where(krow == k, m, vals)
    rank = jnp.where(s <= TAKEN, (s * (1.0 / TAKEN) - 1.0) * (1.0 / TAKEN_STEP), float(PEER_TOPK))
    return vals, rank


SUBLANES = 8
N_CAND = PEER_TOPK + (SUBLANES - 1) * SUBLANES + SUBLANES


def _cand_flat_pos():
    r = lax.broadcasted_iota(jnp.int32, (N_CAND, LANES), 0)
    mid_lo, hi_lo = PEER_TOPK, N_CAND - SUBLANES
    sub_shift = SUBLANES.bit_length() - 1
    mid = (jnp.right_shift(r - mid_lo, sub_shift) + 1) * PEER_TOPK + jnp.bitwise_and(r - mid_lo, SUBLANES - 1)
    hi = (r - hi_lo + SUBLANES) * PEER_TOPK
    return jnp.where(r < mid_lo, r, jnp.where(r < hi_lo, mid, hi)).astype(F32)


def _pair_top16(v1, v2, r1, exact):
    cand = jnp.concatenate(
        [v1[0:1, :] + v2]
        + [v1[a:a + 1, :] + v2[0:SUBLANES, :] for a in range(1, SUBLANES)]
        + [v1[SUBLANES:, :] + v2[0:1, :]], axis=0)
    pos = _cand_flat_pos() if exact else None
    c0 = cand[0:1, :]
    zsum = jnp.zeros((1, LANES), F32)
    for k in range(PEER_TOPK):
        m = jnp.max(cand, axis=0, keepdims=True)
        hit = cand == m
        if exact:
            hit = pos == jnp.min(jnp.where(hit, pos, float(PEER_TOPK * PEER_TOPK)), axis=0, keepdims=True)
        cand = jnp.where(hit, NEG_INF, cand)
        zsum = zsum + jnp.exp(m - c0)
    taken = jnp.where(cand == NEG_INF, 1.0, 0.0)
    n_taken = jnp.sum(taken, axis=0, keepdims=True)
    lrow = jnp.zeros(r1.shape, F32)
    for a in range(PEER_TOPK):
        if a == 0:
            cnt_a = jnp.sum(taken[0:PEER_TOPK, :], axis=0, keepdims=True)
        elif a < SUBLANES:
            lo = PEER_TOPK + SUBLANES * (a - 1)
            cnt_a = jnp.sum(taken[lo:lo + SUBLANES, :], axis=0, keepdims=True)
        else:
            lo = N_CAND - SUBLANES + a - SUBLANES
            cnt_a = taken[lo:lo + 1, :]
        lrow = jnp.where(r1 == float(a), cnt_a, lrow)
    return lrow, zsum, n_taken


def _topk_kernel(sc_ref, r2_ref, e2_ref, l1_ref, e1_ref):
    n = PEER_N_KEYS
    tl = sc_ref.shape[1]
    ngrp = tl // LANES

    def run(item, exact):
        hd = lax.div(item, ngrp)
        gi = lax.rem(item, ngrp)
        lanes = pl.ds(pl.multiple_of(gi * LANES, LANES), LANES)
        s1 = sc_ref[pl.ds(pl.multiple_of(hd * 2 * n, n), n), lanes]
        s2 = sc_ref[pl.ds(pl.multiple_of(hd * 2 * n + n, n), n), lanes]
        v1, r1 = _top16(s1, exact)
        v2, r2 = _top16(s2, exact)
        lrow, zsum, n_taken = _pair_top16(v1, v2, r1, exact)
        r2_ref[hd, :, lanes] = pltpu.bitcast(r2.astype(BF16), jnp.uint32)
        e2_ref[hd, :, lanes] = pltpu.bitcast(jnp.exp(s2 - v2[0:1, :]).astype(BF16), jnp.uint32)
        l1_ref[hd, :, lanes] = _dup_bf16_words(lrow)
        e1_ref[hd, :, lanes] = _dup_bf16_words(jnp.exp(s1 - v1[0:1, :]) / zsum)
        k = float(PEER_TOPK)
        picked = lambda r: jnp.sum(jnp.where(r < k, 1.0, 0.0), axis=0, keepdims=True)
        return jnp.abs(picked(r1) - k) + jnp.abs(picked(r2) - k) + jnp.abs(n_taken - k)

    together = 2 if (PEER_HEADS * ngrp) % 2 == 0 else 1

    def body(it, carry):
        items = [it * together + u for u in range(together)]
        excess = [run(item, exact=False) for item in items]
        for item, ex in zip(items, excess):
            @pl.when(jnp.max(ex) > 0.0)
            def _():
                run(item, exact=True)

        return carry

    lax.fori_loop(0, PEER_HEADS * ngrp // together, body, 0)


def _topk(scores_t, tl):
    nsc, t = scores_t.shape
    spec = lambda rows: pl.BlockSpec((PEER_HEADS, rows, tl), lambda i: (0, 0, i))
    sds = lambda rows: jax.ShapeDtypeStruct((PEER_HEADS, rows, t), jnp.uint32)
    half, full = PEER_N_KEYS // 2, PEER_N_KEYS
    return pl.pallas_call(
        _topk_kernel,
        grid=(t // tl,),
        in_specs=[pl.BlockSpec((nsc, tl), lambda i: (0, i))],
        out_specs=[spec(half), spec(half), spec(full), spec(full)],
        out_shape=[sds(half), sds(half), sds(full), sds(full)],
        compiler_params=_cparams("arbitrary"),
        name="topk",
    )(scores_t)


def _gelu(x):
    return 0.5 * x * (1.0 + lax.erf(x * (2.0 ** -0.5)))


def _dup_bf16_words(x):
    bits = pltpu.bitcast(x.astype(BF16).astype(F32), jnp.uint32)
    return bits | (bits >> 16)


def _bf16_rows(words, n):
    return pltpu.bitcast(jnp.broadcast_to(words, (n // 2, words.shape[1])), BF16)


def _peer_kernel(h2t_ref, dn_ref, up_ref, r2_ref, e2_ref, l1_ref, e1_ref, x1_ref, mod_ref, gf_ref,
                 o_ref, acc_ref, st_ref, pw_ref, *, rows_per_chunk):
    j = pl.program_id(2)
    n = PEER_N_KEYS

    @pl.when(j == 0)
    def _():
        acc_ref[...] = jnp.zeros_like(acc_ref)

    st_ref[...] = _dot(dn_ref[...], h2t_ref[...])
    zero = jnp.zeros((n, LANES), BF16)
    pair = 2
    for g in range(h2t_ref.shape[1] // LANES):
        lanes = slice(g * LANES, (g + 1) * LANES)
        for i1 in range(0, rows_per_chunk, pair):
            w = [None] * pair
            for hd in range(PEER_HEADS):
                r2 = pltpu.bitcast(r2_ref[hd, :, lanes], BF16)
                e2 = pltpu.bitcast(e2_ref[hd, :, lanes], BF16)
                for u in range(pair):
                    cnt = _bf16_rows(l1_ref[hd, i1 + u:i1 + u + 1, lanes], n)
                    e1 = _bf16_rows(e1_ref[hd, i1 + u:i1 + u + 1, lanes], n)
                    wh = jnp.where(r2 < cnt, e2, zero) * e1
                    w[u] = wh if w[u] is None else w[u] + wh
            for u in range(pair):
                rows = slice((i1 + u) * n, (i1 + u + 1) * n)
                pw_ref[rows, lanes] = _gelu(st_ref[rows, lanes].astype(BF16)) * w[u]
    acc_ref[...] += _dot(up_ref[...], pw_ref[...])

    @pl.when(j == pl.num_programs(2) - 1)
    def _():
        gt2 = mod_ref[0, 5:6, :]
        x2 = x1_ref[0] + gt2 * acc_ref[...].T
        inv = lax.rsqrt(jnp.mean(x2 * x2, axis=-1, keepdims=True) + EPS)
        o_ref[0] = (x2 * inv) * gf_ref[...]


def _peer(h2t, down, up_t, r2, e2, l1, e1, x1, mod3, g_final, tt, ec):
    b, s, d = x1.shape
    ne = down.shape[0]
    rpc = ec // PEER_N_KEYS
    spb = s // tt
    tok = lambda bi, si, j: (0, 0, bi * spb + si)
    gate_tab = pl.BlockSpec((PEER_HEADS, rpc, tt), lambda bi, si, j: (0, j, bi * spb + si))
    return pl.pallas_call(
        functools.partial(_peer_kernel, rows_per_chunk=rpc),
        grid=(b, spb, ne // ec),
        in_specs=[pl.BlockSpec((d, tt), lambda bi, si, j: (0, bi * spb + si)),
                  pl.BlockSpec((ec, d), lambda bi, si, j: (j, 0)),
                  pl.BlockSpec((d, ec), lambda bi, si, j: (0, j)),
                  pl.BlockSpec((PEER_HEADS, PEER_N_KEYS // 2, tt), tok),
                  pl.BlockSpec((PEER_HEADS, PEER_N_KEYS // 2, tt), tok),
                  gate_tab, gate_tab,
                  pl.BlockSpec((1, tt, d), lambda bi, si, j: (bi, si, 0)),
                  pl.BlockSpec((1, N_MOD, d), lambda bi, si, j: (bi, 0, 0)),
                  pl.BlockSpec((1, d), lambda bi, si, j: (0, 0))],
        out_specs=pl.BlockSpec((1, tt, d), lambda bi, si, j: (bi, si, 0)),
        out_shape=jax.ShapeDtypeStruct((b, s, d), F32),
        scratch_shapes=[pltpu.VMEM((d, tt), F32), pltpu.VMEM((ec, tt), F32), pltpu.VMEM((ec, tt), BF16)],
        compiler_params=_cparams("arbitrary", "arbitrary", "arbitrary"),
        name="peer",
    )(h2t, down, up_t, r2, e2, l1, e1, x1, mod3, g_final)


def _pad_heads(w):
    d = w.shape[0]
    w = w.reshape(d, N_HEADS, HEAD_DIM)
    return jnp.pad(w, ((0, 0), (0, 0), (0, LANES - HEAD_DIM))).reshape(d, HP)


def _bias_constants():
    pq = np.zeros((3 * LANES, HP), np.float32)
    pk = np.zeros((3 * LANES, HP), np.float32)
    cq = np.zeros((1, HP), np.float32)
    ck = np.zeros((1, HP), np.float32)
    cv = np.zeros((1, HP), np.float32)
    for h in range(N_HEADS):
        for part in range(3):
            pq[part * LANES + h, h * LANES + LANE_F_Q + part] = 1.0
            pk[part * LANES + h, h * LANES + LANE_ONE_Q + part] = -1.0
            cq[0, h * LANES + LANE_ONE_Q + part] = 1.0
            ck[0, h * LANES + LANE_F_Q + part] = 1.0
        cv[0, h * LANES + LANE_ONE_V] = 1.0
    return (jnp.asarray(pq, BF16), jnp.asarray(pk, BF16), jnp.asarray(cq), jnp.asarray(ck), jnp.asarray(cv))


def _tile(n, pref):
    return pref if n % pref == 0 else n


def kernel(x, c, w_ada, b_ada, g_norm_mix, w_in, b_forget, g_out_fox, g_out_sb, w_out, g_norm_ffn,
           w_query, sub_keys, expert_down, expert_up, g_final):
    b, s, d = x.shape
    assert w_ada.shape[0] == 1, "single-layer block: the final RMSNorm is fused into the PEER kernel"
    fw = N_HEADS * HEAD_DIM
    scale = HEAD_DIM ** -0.5 * LOG2E
    pq, pk, cq, ck, cv = _bias_constants()
    tm = _tile(s, 512)
    c_pad = jnp.pad(c, ((0, -b % 8), (0, 0)))

    for l in range(1):
        mod = _ada(c_pad, w_ada[l], b_ada[l][None, :])[:b]
        mod3 = mod.reshape(b, N_MOD, d)

        wi = w_in[l]
        o1 = 3 * fw
        o2 = o1 + N_HEADS
        w_all = jnp.concatenate([
            _pad_heads(wi[:, 0:fw] * scale), _pad_heads(wi[:, fw:2 * fw]), _pad_heads(wi[:, 2 * fw:o1]),
            wi[:, o2:o2 + fw] * scale, wi[:, o2 + fw:o2 + 3 * fw]], axis=1).astype(BF16)
        wf = jnp.pad(wi[:, o1:o2], ((0, 0), (0, LANES - N_HEADS)))
        wf_hi = wf.astype(BF16)
        wf_lo = (wf - wf_hi.astype(F32)).astype(BF16)
        b_f = jnp.pad(b_forget[l], (0, LANES - N_HEADS))[None, :]

        fq, fk, fv, sq, sk, sv = _inproj(x, mod3, g_norm_mix[l][None, :], w_all, wf_hi, wf_lo, b_f,
                                         pq, pk, cq, ck, cv, tm)

        g_fox = jnp.pad(g_out_fox[l], (0, LANES - HEAD_DIM))[None, :]
        g_sb = jnp.tile(g_out_sb[l], LANES // HEAD_DIM)[None, :]
        fo = _fox(fq, fk, fv, g_fox, _tile(s, 512), _tile(s, 256))
        so = _sb(sq, sk, sv, g_sb, _tile(s, 512), _tile(s, 256))

        wo = w_out[l]
        w_out_fox = jnp.pad(wo[:fw].reshape(N_HEADS, HEAD_DIM, d),
                            ((0, 0), (0, LANES - HEAD_DIM), (0, 0))).reshape(HP, d)
        w_out_p = jnp.concatenate([w_out_fox, wo[fw:]], axis=0).astype(BF16)
        wc = _wc(sub_keys[l].reshape(PEER_HEADS * 2, PEER_N_KEYS, LANES), w_query[l])
        x1, h2t, scores_t = _outproj(x, fo, so, mod3, g_norm_ffn[l][None, :], w_out_p, wc.astype(BF16), tm)

        r2, e2, l1, e1 = _topk(scores_t, _tile(b * s, 512))
        x = _peer(h2t, expert_down[l].astype(BF16), expert_up[l].T.astype(BF16), r2, e2, l1, e1,
                  x1, mod3, g_final[None, :], _tile(s, 1024), 1024)
    return x
```

```python
import functools

import numpy as np
import jax
import jax.numpy as jnp
from jax import lax
from jax.experimental import pallas as pl
from jax.experimental.pallas import tpu as pltpu

F32 = jnp.float32
BF16 = jnp.bfloat16

HEAD_DIM = 64
N_HEADS = 8
PEER_HEADS = 8
PEER_N_KEYS = 128
PEER_TOPK = 16
N_MOD = 6
EPS = 1e-6

LANES = 128
VMEM_LIMIT = 56 * 1024 * 1024

HP = N_HEADS * LANES
SBW = N_HEADS * HEAD_DIM
LANE_ONE_Q = HEAD_DIM
LANE_F_Q = HEAD_DIM + 3
LANE_ONE_V = HEAD_DIM
NEG_INF = float("-inf")
LOG2E = 1.4426950408889634
EXP2_UNDERFLOW = -153.0


def _cparams(*sem):
    return pltpu.CompilerParams(dimension_semantics=sem, vmem_limit_bytes=VMEM_LIMIT)


def _dot(a, b):
    return jnp.dot(a, b, preferred_element_type=F32)


def _dot_nt(a, b):
    return lax.dot_general(a, b, (((1,), (1,)), ((), ())), preferred_element_type=F32)


def _split2(x):
    hi = x.astype(BF16)
    lo = (x - hi.astype(F32)).astype(BF16)
    return hi, lo


def _split3(x):
    hi = x.astype(BF16)
    r = x - hi.astype(F32)
    mid = r.astype(BF16)
    lo = (r - mid.astype(F32)).astype(BF16)
    return hi, mid, lo


def _softplus2(z):
    return jnp.maximum(z, 0.0) + jnp.log2(1.0 + jnp.exp2(-jnp.abs(z)))


def _ada_kernel(c_ref, w_ref, b_ref, o_ref):
    c = c_ref[...]
    a = c / (1.0 + jnp.exp(-c))
    ah, al = _split2(a)
    wh, wl = _split2(w_ref[...])
    o_ref[...] = _dot(ah, wh) + _dot(ah, wl) + _dot(al, wh) + b_ref[...]


def _ada(c_pad, w_ada, b_ada):
    rows, d = c_pad.shape
    n = w_ada.shape[1]
    bn = 1024
    return pl.pallas_call(
        _ada_kernel,
        grid=(n // bn,),
        in_specs=[pl.BlockSpec((rows, d), lambda j: (0, 0)),
                  pl.BlockSpec((d, bn), lambda j: (0, j)),
                  pl.BlockSpec((1, bn), lambda j: (0, j))],
        out_specs=pl.BlockSpec((rows, bn), lambda j: (0, j)),
        out_shape=jax.ShapeDtypeStruct((rows, n), F32),
        compiler_params=_cparams("arbitrary"),
        name="ada",
    )(c_pad, w_ada, b_ada)


def _wc_kernel(sk_ref, wq_ref, o_ref):
    sh, sl = _split2(sk_ref[0])
    wh, wl = _split2(wq_ref[...])
    o_ref[...] = _dot_nt(sh, wh) + _dot_nt(sh, wl) + _dot_nt(sl, wh)


def _wc(sub_keys, w_query):
    d = w_query.shape[0]
    nhp = sub_keys.shape[0]
    return pl.pallas_call(
        _wc_kernel,
        grid=(nhp,),
        in_specs=[pl.BlockSpec((1, PEER_N_KEYS, LANES), lambda i: (i, 0, 0)),
                  pl.BlockSpec((d, LANES), lambda i: (0, i))],
        out_specs=pl.BlockSpec((PEER_N_KEYS, d), lambda i: (i, 0)),
        out_shape=jax.ShapeDtypeStruct((nhp * PEER_N_KEYS, d), F32),
        compiler_params=_cparams("arbitrary"),
        name="wc",
    )(sub_keys, w_query)


def _inproj_kernel(x_ref, mod_ref, g_ref, w_ref, wfh_ref, wfl_ref, bf_ref, pq_ref, pk_ref,
                   cq_ref, ck_ref, cv_ref,
                   fq_ref, fk_ref, fv_ref, sq_ref, sk_ref, sv_ref, carry_ref):
    tm = x_ref.shape[1]

    @pl.when(pl.program_id(1) == 0)
    def _():
        carry_ref[...] = jnp.zeros_like(carry_ref)

    x = x_ref[0]
    inv = lax.rsqrt(jnp.mean(x * x, axis=-1, keepdims=True) + EPS)
    sh1 = mod_ref[0, 0:1, :]
    sc1 = mod_ref[0, 1:2, :]
    h = (x * inv) * g_ref[...] * (1.0 + sc1) + sh1
    hh, hl = _split2(h)

    fl = _dot(hh, wfh_ref[...]) + _dot(hh, wfl_ref[...]) + _dot(hl, wfh_ref[...]) + bf_ref[...]
    lf = jnp.minimum(fl, 0.0) - jnp.log(1.0 + jnp.exp(-jnp.abs(fl)))
    row = lax.broadcasted_iota(jnp.int32, (tm, tm), 0)
    col = lax.broadcasted_iota(jnp.int32, (tm, tm), 1)
    tri = jnp.where(col <= row, 1.0, 0.0).astype(BF16)
    l0, l1, l2 = _split3(lf)
    fcum = _dot(tri, l0) + _dot(tri, l1) + _dot(tri, l2) + carry_ref[...]
    carry_ref[...] = fcum[tm - 1:tm, :]
    f0, f1, f2 = _split3(fcum * LOG2E)
    fcat = jnp.concatenate([f0, f1, f2], axis=1)

    fq_ref[0] = (_dot(hh, w_ref[:, 0 * HP:1 * HP]) + _dot(fcat, pq_ref[...]) + cq_ref[...]).astype(BF16)
    fk_ref[0] = (_dot(hh, w_ref[:, 1 * HP:2 * HP]) + _dot(fcat, pk_ref[...]) + ck_ref[...]).astype(BF16)
    fv_ref[0] = (_dot(hh, w_ref[:, 2 * HP:3 * HP]) + cv_ref[...]).astype(BF16)
    sq_ref[0] = _dot(hh, w_ref[:, 3 * HP + 0 * SBW:3 * HP + 1 * SBW]).astype(BF16)
    sk_ref[0] = _dot(hh, w_ref[:, 3 * HP + 1 * SBW:3 * HP + 2 * SBW]).astype(BF16)
    sv_ref[0] = _dot(hh, w_ref[:, 3 * HP + 2 * SBW:3 * HP + 3 * SBW]).astype(BF16)


def _inproj(x, mod3, g_mix, w_all, wf_hi, wf_lo, b_f, pq, pk, cq, ck, cv, tm):
    b, s, d = x.shape
    const = lambda shape: pl.BlockSpec(shape, lambda bi, si: (0,) * len(shape),
                                       pipeline_mode=pl.Buffered(1))
    out_spec = lambda w: pl.BlockSpec((1, tm, w), lambda bi, si: (bi, si, 0))
    out_sds = lambda w: jax.ShapeDtypeStruct((b, s, w), BF16)
    widths = [HP] * 3 + [SBW] * 3
    return pl.pallas_call(
        _inproj_kernel,
        grid=(b, s // tm),
        in_specs=[pl.BlockSpec((1, tm, d), lambda bi, si: (bi, si, 0)),
                  pl.BlockSpec((1, N_MOD, d), lambda bi, si: (bi, 0, 0)),
                  const((1, d)),
                  const((d, sum(widths))),
                  const((d, LANES)), const((d, LANES)), const((1, LANES)),
                  const((3 * LANES, HP)), const((3 * LANES, HP)),
                  const((1, HP)), const((1, HP)), const((1, HP))],
        out_specs=[out_spec(w) for w in widths],
        out_shape=[out_sds(w) for w in widths],
        scratch_shapes=[pltpu.VMEM((1, LANES), F32)],
        compiler_params=_cparams("arbitrary", "arbitrary"),
        name="inproj",
    )(x, mod3, g_mix, w_all, wf_hi, wf_lo, b_f, pq, pk, cq, ck, cv)


def _head_norm(o, g, lo=0):
    lane = lax.broadcasted_iota(jnp.int32, o.shape, 1)
    o = jnp.where((lane >= lo) & (lane < lo + HEAD_DIM), o, 0.0)
    ms = jnp.sum(o * o, axis=-1, keepdims=True) * (1.0 / HEAD_DIM)
    return o * lax.rsqrt(ms + EPS) * g


HEADS_PER_STEP = 2


def _head_slices(ref):
    return [slice(h * LANES, (h + 1) * LANES) for h in range(ref.shape[2] // LANES)]


def _fox_kernel(q_ref, k_ref, v_ref, g_ref, o_ref, m_ref, acc_ref, *, tq, tk):
    s_len = q_ref.shape[1]
    heads = _head_slices(q_ref)
    n_diag = tq // tk
    lane1 = lax.broadcasted_iota(jnp.int32, (1, LANES), 1)
    head_lanes = lane1 < HEAD_DIM
    row = lax.broadcasted_iota(jnp.int32, (tq, tk), 0)
    col = lax.broadcasted_iota(jnp.int32, (tq, tk), 1)

    def knorm(i, mx):
        out = []
        for hl, m in zip(heads, mx):
            kb = k_ref[0, pl.ds(pl.multiple_of(i * tk, tk), tk), hl].astype(F32)
            kb = jnp.where(head_lanes, kb, 0.0)
            n2 = jnp.sum(kb * kb, axis=-1, keepdims=True)
            out.append(jnp.maximum(m, jnp.max(n2, axis=0, keepdims=True)))
        return tuple(out)

    kmax = [jnp.sqrt(m) for m in
            lax.fori_loop(0, s_len // tk, knorm, tuple(jnp.zeros((1, 1), F32) for _ in heads))]

    def q_tile(qi, carry):
        q0 = pl.multiple_of(qi * tq, tq)
        q = [q_ref[0, pl.ds(q0, tq), hl] for hl in heads]
        reach = []
        for h, qh in enumerate(q):
            qf = qh.astype(F32)
            qn = jnp.sqrt(jnp.sum(jnp.where(head_lanes, qf * qf, 0.0), axis=-1, keepdims=True))
            f_t = jnp.sum(jnp.where((lane1 >= LANE_F_Q) & (lane1 < LANE_F_Q + 3), qf, 0.0),
                          axis=-1, keepdims=True)
            reach.append(qn * kmax[h] + f_t)
        m_ref[...] = jnp.full_like(m_ref, NEG_INF)
        acc_ref[...] = jnp.zeros_like(acc_ref)

        def step(start, mask, r0=0):
            for h, hl in enumerate(heads):
                k = k_ref[0, pl.ds(start, tk), hl]
                v = v_ref[0, pl.ds(start, tk), hl]
                s = _dot_nt(q[h][r0:], k)
                if mask is not None:
                    s = jnp.where(mask[r0:], s, NEG_INF)
                m_prev = m_ref[h, r0:, :]
                m_new = jnp.maximum(m_prev, jnp.max(s, axis=-1, keepdims=True))
                p = jnp.exp2(s - jnp.concatenate([m_new] * (tk // LANES), axis=1))
                acc_ref[h, r0:, :] = jnp.exp2(m_prev - m_new) * acc_ref[h, r0:, :] + _dot(p.astype(BF16), v)
                m_ref[h, r0:, :] = m_new

        def block_matters(kb):
            worst = None
            for h, hl in enumerate(heads):
                last = k_ref[0, pl.ds(pl.multiple_of(kb * tk + tk - 16, 16), 16), hl][15:16, :].astype(F32)
                neg_f_last = jnp.sum(jnp.where((lane1 >= LANE_ONE_Q) & (lane1 < LANE_ONE_Q + 3), last, 0.0),
                                     axis=-1, keepdims=True)
                bound = reach[h] + neg_f_last - m_ref[h, :, 0:1]
                worst = bound if worst is None else jnp.maximum(worst, bound)
            return jnp.max(worst) > EXP2_UNDERFLOW

        for d in range(n_diag):
            step(pl.multiple_of(q0 + d * tk, tk), col + d * tk <= row, d * tk)

        def cond(st):
            return st[1]

        def body(st):
            kb = st[0]
            step(pl.multiple_of(kb * tk, tk), None)
            nxt = kb - 1
            return nxt, (nxt >= 0) & block_matters(jnp.maximum(nxt, 0))

        kb0 = qi * n_diag - 1
        lax.while_loop(cond, body, (kb0, (kb0 >= 0) & block_matters(jnp.maximum(kb0, 0))))

        for h, hl in enumerate(heads):
            acc = acc_ref[h]
            lane = lax.broadcasted_iota(jnp.int32, acc.shape, 1)
            l = jnp.sum(jnp.where(lane == LANE_ONE_V, acc, 0.0), axis=-1, keepdims=True)
            o_ref[0, pl.ds(q0, tq), hl] = _head_norm(acc / l, g_ref[...]).astype(o_ref.dtype)
        return carry

    lax.fori_loop(0, s_len // tq, q_tile, 0)


def _attn_call(body, name, q, k, v, g, tq, width, nh):
    b, s, total = q.shape
    spec = pl.BlockSpec((1, s, width), lambda bi, hi: (bi, 0, hi))
    return pl.pallas_call(
        body,
        grid=(b, total // width),
        in_specs=[spec, spec, spec, pl.BlockSpec((1, LANES), lambda bi, hi: (0, 0))],
        out_specs=spec,
        out_shape=jax.ShapeDtypeStruct((b, s, total), BF16),
        scratch_shapes=[pltpu.VMEM((nh, tq, LANES), F32), pltpu.VMEM((nh, tq, LANES), F32)],
        compiler_params=_cparams("arbitrary", "arbitrary"),
        name=name,
    )(q, k, v, g)


def _fox(q, k, v, g, tq, tk):
    return _attn_call(functools.partial(_fox_kernel, tq=tq, tk=tk), "fox", q, k, v, g, tq,
                      HEADS_PER_STEP * LANES, HEADS_PER_STEP)


def _sb_kernel(q_ref, k_ref, v_ref, g_ref, o_ref, c_ref, acc_ref, *, tq, tk):
    s_len = q_ref.shape[1]
    n_diag = tq // tk
    jrow = lax.broadcasted_iota(jnp.int32, (tk, tk), 0)
    scol = lax.broadcasted_iota(jnp.int32, (tk, tk), 1)
    later = jnp.where(jrow > scol, 1.0, 0.0).astype(BF16)
    row = lax.broadcasted_iota(jnp.int32, (tq, tk), 0)
    col = lax.broadcasted_iota(jnp.int32, (tq, tk), 1)

    n_heads = LANES // HEAD_DIM
    lane_q = lax.broadcasted_iota(jnp.int32, (tq, LANES), 1)
    own = [(lane_q >= h * HEAD_DIM) & (lane_q < (h + 1) * HEAD_DIM) for h in range(n_heads)]

    def q_tile(qi, carry):
        q0 = pl.multiple_of(qi * tq, tq)
        q_all = q_ref[0, pl.ds(q0, tq), :]
        q = [jnp.where(own[h], q_all, jnp.zeros_like(q_all)) for h in range(n_heads)]
        c_ref[...] = jnp.zeros_like(c_ref)
        acc_ref[...] = jnp.zeros_like(acc_ref)

        def step(start, mask, r0=0):
            k = k_ref[0, pl.ds(start, tk), :]
            v = v_ref[0, pl.ds(start, tk), :]
            for h in range(n_heads):
                z = _dot_nt(q[h][r0:], k)
                sp = _softplus2(z)
                spm = sp if mask is None else jnp.where(mask[r0:], sp, 0.0)
                c_prev = c_ref[h, r0:, :]
                rest = _dot(spm.astype(BF16), later) + jnp.concatenate([c_prev] * (tk // LANES), axis=1)
                a = jnp.exp2(z - sp - rest)
                if mask is not None:
                    a = jnp.where(mask[r0:], a, 0.0)
                acc_ref[h, r0:, :] += _dot(a.astype(BF16), v)
                c_ref[h, r0:, :] = c_prev + jnp.sum(spm, axis=-1, keepdims=True)

        def more_matters():
            least = None
            for h in range(n_heads):
                ch = c_ref[h, :, 0:1]
                least = ch if least is None else jnp.minimum(least, ch)
            return jnp.min(least) < -EXP2_UNDERFLOW

        for d in reversed(range(n_diag)):
            step(pl.multiple_of(q0 + d * tk, tk), col + d * tk < row, d * tk)

        def cond(st):
            return st[1]

        def body(st):
            kb = st[0]
            step(pl.multiple_of(kb * tk, tk), None)
            nxt = kb - 1
            return nxt, (nxt >= 0) & more_matters()

        kb0 = qi * n_diag - 1
        lax.while_loop(cond, body, (kb0, (kb0 >= 0) & more_matters()))
        out = None
        for h in range(n_heads):
            oh = _head_norm(acc_ref[h], g_ref[...], h * HEAD_DIM)
            out = oh if out is None else out + oh
        o_ref[0, pl.ds(q0, tq), :] = out.astype(o_ref.dtype)
        return carry

    lax.fori_loop(0, s_len // tq, q_tile, 0)


def _sb(q, k, v, g, tq, tk):
    return _attn_call(functools.partial(_sb_kernel, tq=tq, tk=tk), "sb", q, k, v, g, tq, LANES,
                      LANES // HEAD_DIM)


def _outproj_kernel(x_ref, fo_ref, so_ref, mod_ref, g_ref, wo_ref, wc_ref,
                    x1_ref, h2t_ref, sc_ref):
    gt1 = mod_ref[0, 2:3, :]
    sh2 = mod_ref[0, 3:4, :]
    sc2 = mod_ref[0, 4:5, :]
    mixed = _dot(fo_ref[0], wo_ref[0:HP, :]) + _dot(so_ref[0], wo_ref[HP:HP + SBW, :])
    x1 = x_ref[0] + gt1 * mixed
    x1_ref[0] = x1
    inv = lax.rsqrt(jnp.mean(x1 * x1, axis=-1, keepdims=True) + EPS)
    h2 = (x1 * inv) * g_ref[...] * (1.0 + sc2) + sh2
    th = h2.T.astype(BF16)
    h2t_ref[...] = th
    sc_ref[...] = _dot(wc_ref[...], th)


def _outproj(x, fo, so, mod3, g_ffn, w_out_p, wc, tm):
    b, s, d = x.shape
    t = b * s
    nsc = wc.shape[0]
    const = lambda shape: pl.BlockSpec(shape, lambda bi, si: (0,) * len(shape),
                                       pipeline_mode=pl.Buffered(1))
    spb = s // tm
    return pl.pallas_call(
        _outproj_kernel,
        grid=(b, spb),
        in_specs=[pl.BlockSpec((1, tm, d), lambda bi, si: (bi, si, 0)),
                  pl.BlockSpec((1, tm, HP), lambda bi, si: (bi, si, 0)),
                  pl.BlockSpec((1, tm, SBW), lambda bi, si: (bi, si, 0)),
                  pl.BlockSpec((1, N_MOD, d), lambda bi, si: (bi, 0, 0)),
                  const((1, d)),
                  const((HP + SBW, d)),
                  const((nsc, d))],
        out_specs=[pl.BlockSpec((1, tm, d), lambda bi, si: (bi, si, 0)),
                   pl.BlockSpec((d, tm), lambda bi, si: (0, bi * spb + si)),
                   pl.BlockSpec((nsc, tm), lambda bi, si: (0, bi * spb + si))],
        out_shape=[jax.ShapeDtypeStruct((b, s, d), F32),
                   jax.ShapeDtypeStruct((d, t), BF16),
                   jax.ShapeDtypeStruct((nsc, t), F32)],
        compiler_params=_cparams("arbitrary", "arbitrary"),
        name="outproj",
    )(x, fo, so, mod3, g_ffn, w_out_p, wc)


TAKEN = -(2.0 ** 126)
TAKEN_STEP = 2.0 ** -10


def _top16(s, exact):
    n, t = s.shape
    idx = lax.broadcasted_iota(jnp.int32, (n, t), 0).astype(F32)
    krow = lax.broadcasted_iota(jnp.int32, (PEER_TOPK, t), 0)
    vals = jnp.zeros((PEER_TOPK, t), F32)
    for k in range(PEER_TOPK):
        m = jnp.max(s, axis=0, keepdims=True)
        hit = s == m
        if exact:
            hit = idx == jnp.min(jnp.where(hit, idx, float(n)), axis=0, keepdims=True)
        s = jnp.where(hit, TAKEN * (1.0 + k * TAKEN_STEP), s)
        vals = jnp.where(krow == k, m, vals)
    rank = jnp.where(s <= TAKEN, (s * (1.0 / TAKEN) - 1.0) * (1.0 / TAKEN_STEP), float(PEER_TOPK))
    return vals, rank


SUBLANES = 8
N_CAND = PEER_TOPK + (SUBLANES - 1) * SUBLANES + SUBLANES


def _cand_flat_pos():
    r = lax.broadcasted_iota(jnp.int32, (N_CAND, LANES), 0)
    mid_lo, hi_lo = PEER_TOPK, N_CAND - SUBLANES
    sub_shift = SUBLANES.bit_length() - 1
    mid = (jnp.right_shift(r - mid_lo, sub_shift) + 1) * PEER_TOPK + jnp.bitwise_and(r - mid_lo, SUBLANES - 1)
    hi = (r - hi_lo + SUBLANES) * PEER_TOPK
    return jnp.where(r < mid_lo, r, jnp.where(r < hi_lo, mid, hi)).astype(F32)


def _pair_top16(v1, v2, r1, exact):
    cand = jnp.concatenate(
        [v1[0:1, :] + v2]
        + [v1[a:a + 1, :] + v2[0:SUBLANES, :] for a in range(1, SUBLANES)]
        + [v1[SUBLANES:, :] + v2[0:1, :]], axis=0)
    pos = _cand_flat_pos() if exact else None
    c0 = cand[0:1, :]
    zsum = jnp.zeros((1, LANES), F32)
    for k in range(PEER_TOPK):
        m = jnp.max(cand, axis=0, keepdims=True)
        hit = cand == m
        if exact:
            hit = pos == jnp.min(jnp.where(hit, pos, float(PEER_TOPK * PEER_TOPK)), axis=0, keepdims=True)
        cand = jnp.where(hit, NEG_INF, cand)
        zsum = zsum + jnp.exp(m - c0)
    taken = jnp.where(cand == NEG_INF, 1.0, 0.0)
    n_taken = jnp.sum(taken, axis=0, keepdims=True)
    lrow = jnp.zeros(r1.shape, F32)
    for a in range(PEER_TOPK):
        if a == 0:
            cnt_a = jnp.sum(taken[0:PEER_TOPK, :], axis=0, keepdims=True)
        elif a < SUBLANES:
            lo = PEER_TOPK + SUBLANES * (a - 1)
            cnt_a = jnp.sum(taken[lo:lo + SUBLANES, :], axis=0, keepdims=True)
        else:
            lo = N_CAND - SUBLANES + a - SUBLANES
            cnt_a = taken[lo:lo + 1, :]
        lrow = jnp.where(r1 == float(a), cnt_a, lrow)
    return lrow, zsum, n_taken


def _topk_kernel(sc_ref, r2_ref, e2_ref, l1_ref, e1_ref):
    n = PEER_N_KEYS
    tl = sc_ref.shape[1]
    ngrp = tl // LANES

    def run(item, exact):
        hd = lax.div(item, ngrp)
        gi = lax.rem(item, ngrp)
        lanes = pl.ds(pl.multiple_of(gi * LANES, LANES), LANES)
        s1 = sc_ref[pl.ds(pl.multiple_of(hd * 2 * n, n), n), lanes]
        s2 = sc_ref[pl.ds(pl.multiple_of(hd * 2 * n + n, n), n), lanes]
        v1, r1 = _top16(s1, exact)
        v2, r2 = _top16(s2, exact)
        lrow, zsum, n_taken = _pair_top16(v1, v2, r1, exact)
        r2_ref[hd, :, lanes] = pltpu.bitcast(r2.astype(BF16), jnp.uint32)
        e2_ref[hd, :, lanes] = pltpu.bitcast(jnp.exp(s2 - v2[0:1, :]).astype(BF16), jnp.uint32)
        l1_ref[hd, :, lanes] = _dup_bf16_words(lrow)
        e1_ref[hd, :, lanes] = _dup_bf16_words(jnp.exp(s1 - v1[0:1, :]) / zsum)
        k = float(PEER_TOPK)
        picked = lambda r: jnp.sum(jnp.where(r < k, 1.0, 0.0), axis=0, keepdims=True)
        return jnp.abs(picked(r1) - k) + jnp.abs(picked(r2) - k) + jnp.abs(n_taken - k)

    together = 2 if (PEER_HEADS * ngrp) % 2 == 0 else 1

    def body(it, carry):
        items = [it * together + u for u in range(together)]
        excess = [run(item, exact=False) for item in items]
        for item, ex in zip(items, excess):
            @pl.when(jnp.max(ex) > 0.0)
            def _():
                run(item, exact=True)

        return carry

    lax.fori_loop(0, PEER_HEADS * ngrp // together, body, 0)


def _topk(scores_t, tl):
    nsc, t = scores_t.shape
    spec = lambda rows: pl.BlockSpec((PEER_HEADS, rows, tl), lambda i: (0, 0, i))
    sds = lambda rows: jax.ShapeDtypeStruct((PEER_HEADS, rows, t), jnp.uint32)
    half, full = PEER_N_KEYS // 2, PEER_N_KEYS
    return pl.pallas_call(
        _topk_kernel,
        grid=(t // tl,),
        in_specs=[pl.BlockSpec((nsc, tl), lambda i: (0, i))],
        out_specs=[spec(half), spec(half), spec(full), spec(full)],
        out_shape=[sds(half), sds(half), sds(full), sds(full)],
        compiler_params=_cparams("arbitrary"),
        name="topk",
    )(scores_t)


def _gelu(x):
    return 0.5 * x * (1.0 + lax.erf(x * (2.0 ** -0.5)))


def _dup_bf16_words(x):
    bits = pltpu.bitcast(x.astype(BF16).astype(F32), jnp.uint32)
    return bits | (bits >> 16)


def _bf16_rows(words, n):
    return pltpu.bitcast(jnp.broadcast_to(words, (n // 2, words.shape[1])), BF16)


def _peer_kernel(h2t_ref, dn_ref, up_ref, r2_ref, e2_ref, l1_ref, e1_ref, x1_ref, mod_ref, gf_ref,
                 o_ref, acc_ref, st_ref, pw_ref, *, rows_per_chunk):
    j = pl.program_id(2)
    n = PEER_N_KEYS

    @pl.when(j == 0)
    def _():
        acc_ref[...] = jnp.zeros_like(acc_ref)

    st_ref[...] = _dot(dn_ref[...], h2t_ref[...])
    zero = jnp.zeros((n, LANES), BF16)
    pair = 2
    for g in range(h2t_ref.shape[1] // LANES):
        lanes = slice(g * LANES, (g + 1) * LANES)
        for i1 in range(0, rows_per_chunk, pair):
            w = [None] * pair
            for hd in range(PEER_HEADS):
                r2 = pltpu.bitcast(r2_ref[hd, :, lanes], BF16)
                e2 = pltpu.bitcast(e2_ref[hd, :, lanes], BF16)
                for u in range(pair):
                    cnt = _bf16_rows(l1_ref[hd, i1 + u:i1 + u + 1, lanes], n)
                    e1 = _bf16_rows(e1_ref[hd, i1 + u:i1 + u + 1, lanes], n)
                    wh = jnp.where(r2 < cnt, e2, zero) * e1
                    w[u] = wh if w[u] is None else w[u] + wh
            for u in range(pair):
                rows = slice((i1 + u) * n, (i1 + u + 1) * n)
                pw_ref[rows, lanes] = _gelu(st_ref[rows, lanes].astype(BF16)) * w[u]
    acc_ref[...] += _dot(up_ref[...], pw_ref[...])

    @pl.when(j == pl.num_programs(2) - 1)
    def _():
        gt2 = mod_ref[0, 5:6, :]
        x2 = x1_ref[0] + gt2 * acc_ref[...].T
        inv = lax.rsqrt(jnp.mean(x2 * x2, axis=-1, keepdims=True) + EPS)
        o_ref[0] = (x2 * inv) * gf_ref[...]


def _peer(h2t, down, up_t, r2, e2, l1, e1, x1, mod3, g_final, tt, ec):
    b, s, d = x1.shape
    ne = down.shape[0]
    rpc = ec // PEER_N_KEYS
    spb = s // tt
    tok = lambda bi, si, j: (0, 0, bi * spb + si)
    gate_tab = pl.BlockSpec((PEER_HEADS, rpc, tt), lambda bi, si, j: (0, j, bi * spb + si))
    return pl.pallas_call(
        functools.partial(_peer_kernel, rows_per_chunk=rpc),
        grid=(b, spb, ne // ec),
        in_specs=[pl.BlockSpec((d, tt), lambda bi, si, j: (0, bi * spb + si)),
                  pl.BlockSpec((ec, d), lambda bi, si, j: (j, 0)),
                  pl.BlockSpec((d, ec), lambda bi, si, j: (0, j)),
                  pl.BlockSpec((PEER_HEADS, PEER_N_KEYS // 2, tt), tok),
                  pl.BlockSpec((PEER_HEADS, PEER_N_KEYS // 2, tt), tok),
                  gate_tab, gate_tab,
                  pl.BlockSpec((1, tt, d), lambda bi, si, j: (bi, si, 0)),
                  pl.BlockSpec((1, N_MOD, d), lambda bi, si, j: (bi, 0, 0)),
                  pl.BlockSpec((1, d), lambda bi, si, j: (0, 0))],
        out_specs=pl.BlockSpec((1, tt, d), lambda bi, si, j: (bi, si, 0)),
        out_shape=jax.ShapeDtypeStruct((b, s, d), F32),
        scratch_shapes=[pltpu.VMEM((d, tt), F32), pltpu.VMEM((ec, tt), F32), pltpu.VMEM((ec, tt), BF16)],
        compiler_params=_cparams("arbitrary", "arbitrary", "arbitrary"),
        name="peer",
    )(h2t, down, up_t, r2, e2, l1, e1, x1, mod3, g_final)


def _pad_heads(w):
    d = w.shape[0]
    w = w.reshape(d, N_HEADS, HEAD_DIM)
    return jnp.pad(w, ((0, 0), (0, 0), (0, LANES - HEAD_DIM))).reshape(d, HP)


def _bias_constants():
    pq = np.zeros((3 * LANES, HP), np.float32)
    pk = np.zeros((3 * LANES, HP), np.float32)
    cq = np.zeros((1, HP), np.float32)
    ck = np.zeros((1, HP), np.float32)
    cv = np.zeros((1, HP), np.float32)
    for h in range(N_HEADS):
        for part in range(3):
            pq[part * LANES + h, h * LANES + LANE_F_Q + part] = 1.0
            pk[part * LANES + h, h * LANES + LANE_ONE_Q + part] = -1.0
            cq[0, h * LANES + LANE_ONE_Q + part] = 1.0
            ck[0, h * LANES + LANE_F_Q + part] = 1.0
        cv[0, h * LANES + LANE_ONE_V] = 1.0
    return (jnp.asarray(pq, BF16), jnp.asarray(pk, BF16), jnp.asarray(cq), jnp.asarray(ck), jnp.asarray(cv))


PROJ_TOKENS = 512
ATTN_Q_TOKENS = 512
ATTN_K_TOKENS = 256
TOPK_TOKENS = 512
PEER_TOKENS = 1024
PEER_EXPERTS = 1024


def _tile(n, pref):
    return pref if n % pref == 0 else n


def kernel(x, c, w_ada, b_ada, g_norm_mix, w_in, b_forget, g_out_fox, g_out_sb, w_out, g_norm_ffn,
           w_query, sub_keys, expert_down, expert_up, g_final):
    b, s, d = x.shape
    assert w_ada.shape[0] == 1, "single-layer block: the final RMSNorm is fused into the PEER kernel"
    fw = N_HEADS * HEAD_DIM
    scale = HEAD_DIM ** -0.5 * LOG2E
    pq, pk, cq, ck, cv = _bias_constants()
    tm = _tile(s, PROJ_TOKENS)
    tq, tk = _tile(s, ATTN_Q_TOKENS), _tile(s, ATTN_K_TOKENS)

    mod = _ada(jnp.pad(c, ((0, -b % 8), (0, 0))), w_ada[0], b_ada[0][None, :])[:b]
    mod3 = mod.reshape(b, N_MOD, d)

    wi = w_in[0]
    o1 = 3 * fw
    o2 = o1 + N_HEADS
    w_all = jnp.concatenate([
        _pad_heads(wi[:, 0:fw] * scale), _pad_heads(wi[:, fw:2 * fw]), _pad_heads(wi[:, 2 * fw:o1]),
        wi[:, o2:o2 + fw] * scale, wi[:, o2 + fw:o2 + 3 * fw]], axis=1).astype(BF16)
    wf = jnp.pad(wi[:, o1:o2], ((0, 0), (0, LANES - N_HEADS)))
    wf_hi = wf.astype(BF16)
    wf_lo = (wf - wf_hi.astype(F32)).astype(BF16)
    b_f = jnp.pad(b_forget[0], (0, LANES - N_HEADS))[None, :]
    fq, fk, fv, sq, sk, sv = _inproj(x, mod3, g_norm_mix[0][None, :], w_all, wf_hi, wf_lo, b_f,
                                     pq, pk, cq, ck, cv, tm)

    g_fox = jnp.pad(g_out_fox[0], (0, LANES - HEAD_DIM))[None, :]
    g_sb = jnp.tile(g_out_sb[0], LANES // HEAD_DIM)[None, :]
    fo = _fox(fq, fk, fv, g_fox, tq, tk)
    so = _sb(sq, sk, sv, g_sb, tq, tk)

    wo = w_out[0]
    w_out_fox = jnp.pad(wo[:fw].reshape(N_HEADS, HEAD_DIM, d),
                        ((0, 0), (0, LANES - HEAD_DIM), (0, 0))).reshape(HP, d)
    w_out_p = jnp.concatenate([w_out_fox, wo[fw:]], axis=0).astype(BF16)
    wc = _wc(sub_keys[0].reshape(PEER_HEADS * 2, PEER_N_KEYS, LANES), w_query[0])
    x1, h2t, scores_t = _outproj(x, fo, so, mod3, g_norm_ffn[0][None, :], w_out_p, wc.astype(BF16), tm)

    r2, e2, l1, e1 = _topk(scores_t, _tile(b * s, TOPK_TOKENS))
    return _peer(h2t, expert_down[0].astype(BF16), expert_up[0].T.astype(BF16), r2, e2, l1, e1,
                 x1, mod3, g_final[None, :], _tile(s, PEER_TOKENS), PEER_EXPERTS)
```

```python
import functools

import numpy as np
import jax
import jax.numpy as jnp
from jax import lax
from jax.experimental import pallas as pl
from jax.experimental.pallas import tpu as pltpu

F32 = jnp.float32
BF16 = jnp.bfloat16

HEAD_DIM = 64
N_HEADS = 8
PEER_HEADS = 8
PEER_N_KEYS = 128
PEER_TOPK = 16
N_MOD = 6
EPS = 1e-6

LANES = 128
VMEM_LIMIT = 56 * 1024 * 1024

HP = N_HEADS * LANES
SBW = N_HEADS * HEAD_DIM
LANE_ONE_Q = HEAD_DIM
LANE_F_Q = HEAD_DIM + 3
LANE_ONE_V = HEAD_DIM
NEG_INF = float("-inf")
LOG2E = 1.4426950408889634
EXP2_UNDERFLOW = -153.0


def _cparams(*sem):
    return pltpu.CompilerParams(dimension_semantics=sem, vmem_limit_bytes=VMEM_LIMIT)


def _dot(a, b):
    return jnp.dot(a, b, preferred_element_type=F32)


def _dot_nt(a, b):
    return lax.dot_general(a, b, (((1,), (1,)), ((), ())), preferred_element_type=F32)


def _split2(x):
    hi = x.astype(BF16)
    lo = (x - hi.astype(F32)).astype(BF16)
    return hi, lo


def _split3(x):
    hi = x.astype(BF16)
    r = x - hi.astype(F32)
    mid = r.astype(BF16)
    lo = (r - mid.astype(F32)).astype(BF16)
    return hi, mid, lo


def _softplus2(z):
    return jnp.maximum(z, 0.0) + jnp.log2(1.0 + jnp.exp2(-jnp.abs(z)))


def _ada_kernel(c_ref, w_ref, b_ref, o_ref):
    c = c_ref[...]
    a = c / (1.0 + jnp.exp(-c))
    ah, al = _split2(a)
    wh, wl = _split2(w_ref[...])
    o_ref[...] = _dot(ah, wh) + _dot(ah, wl) + _dot(al, wh) + b_ref[...]


def _ada(c_pad, w_ada, b_ada):
    rows, d = c_pad.shape
    n = w_ada.shape[1]
    bn = 1024
    return pl.pallas_call(
        _ada_kernel,
        grid=(n // bn,),
        in_specs=[pl.BlockSpec((rows, d), lambda j: (0, 0)),
                  pl.BlockSpec((d, bn), lambda j: (0, j)),
                  pl.BlockSpec((1, bn), lambda j: (0, j))],
        out_specs=pl.BlockSpec((rows, bn), lambda j: (0, j)),
        out_shape=jax.ShapeDtypeStruct((rows, n), F32),
        compiler_params=_cparams("arbitrary"),
        name="ada",
    )(c_pad, w_ada, b_ada)


def _wc_kernel(sk_ref, wq_ref, o_ref):
    sh, sl = _split2(sk_ref[0])
    wh, wl = _split2(wq_ref[...])
    o_ref[...] = _dot_nt(sh, wh) + _dot_nt(sh, wl) + _dot_nt(sl, wh)


def _wc(sub_keys, w_query):
    d = w_query.shape[0]
    nhp = sub_keys.shape[0]
    return pl.pallas_call(
        _wc_kernel,
        grid=(nhp,),
        in_specs=[pl.BlockSpec((1, PEER_N_KEYS, LANES), lambda i: (i, 0, 0)),
                  pl.BlockSpec((d, LANES), lambda i: (0, i))],
        out_specs=pl.BlockSpec((PEER_N_KEYS, d), lambda i: (i, 0)),
        out_shape=jax.ShapeDtypeStruct((nhp * PEER_N_KEYS, d), F32),
        compiler_params=_cparams("arbitrary"),
        name="wc",
    )(sub_keys, w_query)


def _inproj_kernel(x_ref, mod_ref, g_ref, w_ref, wfh_ref, wfl_ref, bf_ref, pq_ref, pk_ref,
                   cq_ref, ck_ref, cv_ref,
                   fq_ref, fk_ref, fv_ref, sq_ref, sk_ref, sv_ref, carry_ref):
    tm = x_ref.shape[1]

    @pl.when(pl.program_id(1) == 0)
    def _():
        carry_ref[...] = jnp.zeros_like(carry_ref)

    x = x_ref[0]
    inv = lax.rsqrt(jnp.mean(x * x, axis=-1, keepdims=True) + EPS)
    sh1 = mod_ref[0, 0:1, :]
    sc1 = mod_ref[0, 1:2, :]
    h = (x * inv) * g_ref[...] * (1.0 + sc1) + sh1
    hh, hl = _split2(h)

    fl = _dot(hh, wfh_ref[...]) + _dot(hh, wfl_ref[...]) + _dot(hl, wfh_ref[...]) + bf_ref[...]
    lf = jnp.minimum(fl, 0.0) - jnp.log(1.0 + jnp.exp(-jnp.abs(fl)))
    row = lax.broadcasted_iota(jnp.int32, (tm, tm), 0)
    col = lax.broadcasted_iota(jnp.int32, (tm, tm), 1)
    tri = jnp.where(col <= row, 1.0, 0.0).astype(BF16)
    l0, l1, l2 = _split3(lf)
    fcum = _dot(tri, l0) + _dot(tri, l1) + _dot(tri, l2) + carry_ref[...]
    carry_ref[...] = fcum[tm - 1:tm, :]
    f0, f1, f2 = _split3(fcum * LOG2E)
    fcat = jnp.concatenate([f0, f1, f2], axis=1)

    fq_ref[0] = (_dot(hh, w_ref[:, 0 * HP:1 * HP]) + _dot(fcat, pq_ref[...]) + cq_ref[...]).astype(BF16)
    fk_ref[0] = (_dot(hh, w_ref[:, 1 * HP:2 * HP]) + _dot(fcat, pk_ref[...]) + ck_ref[...]).astype(BF16)
    fv_ref[0] = (_dot(hh, w_ref[:, 2 * HP:3 * HP]) + cv_ref[...]).astype(BF16)
    sq_ref[0] = _dot(hh, w_ref[:, 3 * HP + 0 * SBW:3 * HP + 1 * SBW]).astype(BF16)
    sk_ref[0] = _dot(hh, w_ref[:, 3 * HP + 1 * SBW:3 * HP + 2 * SBW]).astype(BF16)
    sv_ref[0] = _dot(hh, w_ref[:, 3 * HP + 2 * SBW:3 * HP + 3 * SBW]).astype(BF16)


def _inproj(x, mod3, g_mix, w_all, wf_hi, wf_lo, b_f, pq, pk, cq, ck, cv, tm):
    b, s, d = x.shape
    const = lambda shape: pl.BlockSpec(shape, lambda bi, si: (0,) * len(shape),
                                       pipeline_mode=pl.Buffered(1))
    out_spec = lambda w: pl.BlockSpec((1, tm, w), lambda bi, si: (bi, si, 0))
    out_sds = lambda w: jax.ShapeDtypeStruct((b, s, w), BF16)
    widths = [HP] * 3 + [SBW] * 3
    return pl.pallas_call(
        _inproj_kernel,
        grid=(b, s // tm),
        in_specs=[pl.BlockSpec((1, tm, d), lambda bi, si: (bi, si, 0)),
                  pl.BlockSpec((1, N_MOD, d), lambda bi, si: (bi, 0, 0)),
                  const((1, d)),
                  const((d, sum(widths))),
                  const((d, LANES)), const((d, LANES)), const((1, LANES)),
                  const((3 * LANES, HP)), const((3 * LANES, HP)),
                  const((1, HP)), const((1, HP)), const((1, HP))],
        out_specs=[out_spec(w) for w in widths],
        out_shape=[out_sds(w) for w in widths],
        scratch_shapes=[pltpu.VMEM((1, LANES), F32)],
        compiler_params=_cparams("arbitrary", "arbitrary"),
        name="inproj",
    )(x, mod3, g_mix, w_all, wf_hi, wf_lo, b_f, pq, pk, cq, ck, cv)


def _head_norm(o, g, lo=0):
    lane = lax.broadcasted_iota(jnp.int32, o.shape, 1)
    o = jnp.where((lane >= lo) & (lane < lo + HEAD_DIM), o, 0.0)
    ms = jnp.sum(o * o, axis=-1, keepdims=True) * (1.0 / HEAD_DIM)
    return o * lax.rsqrt(ms + EPS) * g


HEADS_PER_STEP = 2


def _head_slices(ref):
    return [slice(h * LANES, (h + 1) * LANES) for h in range(ref.shape[2] // LANES)]


def _fox_kernel(q_ref, k_ref, v_ref, g_ref, o_ref, m_ref, acc_ref, *, tq, tk):
    s_len = q_ref.shape[1]
    heads = _head_slices(q_ref)
    n_diag = tq // tk
    lane1 = lax.broadcasted_iota(jnp.int32, (1, LANES), 1)
    head_lanes = lane1 < HEAD_DIM
    row = lax.broadcasted_iota(jnp.int32, (tq, tk), 0)
    col = lax.broadcasted_iota(jnp.int32, (tq, tk), 1)

    def knorm(i, mx):
        out = []
        for hl, m in zip(heads, mx):
            kb = k_ref[0, pl.ds(pl.multiple_of(i * tk, tk), tk), hl].astype(F32)
            kb = jnp.where(head_lanes, kb, 0.0)
            n2 = jnp.sum(kb * kb, axis=-1, keepdims=True)
            out.append(jnp.maximum(m, jnp.max(n2, axis=0, keepdims=True)))
        return tuple(out)

    kmax = [jnp.sqrt(m) for m in
            lax.fori_loop(0, s_len // tk, knorm, tuple(jnp.zeros((1, 1), F32) for _ in heads))]

    def q_tile(qi, carry):
        q0 = pl.multiple_of(qi * tq, tq)
        q = [q_ref[0, pl.ds(q0, tq), hl] for hl in heads]
        reach = []
        for h, qh in enumerate(q):
            qf = qh.astype(F32)
            qn = jnp.sqrt(jnp.sum(jnp.where(head_lanes, qf * qf, 0.0), axis=-1, keepdims=True))
            f_t = jnp.sum(jnp.where((lane1 >= LANE_F_Q) & (lane1 < LANE_F_Q + 3), qf, 0.0),
                          axis=-1, keepdims=True)
            reach.append(qn * kmax[h] + f_t)
        m_ref[...] = jnp.full_like(m_ref, NEG_INF)
        acc_ref[...] = jnp.zeros_like(acc_ref)

        def step(start, mask, r0=0):
            for h, hl in enumerate(heads):
                k = k_ref[0, pl.ds(start, tk), hl]
                v = v_ref[0, pl.ds(start, tk), hl]
                s = _dot_nt(q[h][r0:], k)
                if mask is not None:
                    s = jnp.where(mask[r0:], s, NEG_INF)
                m_prev = m_ref[h, r0:, :]
                m_new = jnp.maximum(m_prev, jnp.max(s, axis=-1, keepdims=True))
                p = jnp.exp2(s - jnp.concatenate([m_new] * (tk // LANES), axis=1))
                acc_ref[h, r0:, :] = jnp.exp2(m_prev - m_new) * acc_ref[h, r0:, :] + _dot(p.astype(BF16), v)
                m_ref[h, r0:, :] = m_new

        def block_matters(kb):
            worst = None
            for h, hl in enumerate(heads):
                last = k_ref[0, pl.ds(pl.multiple_of(kb * tk + tk - 16, 16), 16), hl][15:16, :].astype(F32)
                neg_f_last = jnp.sum(jnp.where((lane1 >= LANE_ONE_Q) & (lane1 < LANE_ONE_Q + 3), last, 0.0),
                                     axis=-1, keepdims=True)
                bound = reach[h] + neg_f_last - m_ref[h, :, 0:1]
                worst = bound if worst is None else jnp.maximum(worst, bound)
            return jnp.max(worst) > EXP2_UNDERFLOW

        for d in range(n_diag):
            step(pl.multiple_of(q0 + d * tk, tk), col + d * tk <= row, d * tk)

        def cond(st):
            return st[1]

        def body(st):
            kb = st[0]
            step(pl.multiple_of(kb * tk, tk), None)
            nxt = kb - 1
            return nxt, (nxt >= 0) & block_matters(jnp.maximum(nxt, 0))

        kb0 = qi * n_diag - 1
        lax.while_loop(cond, body, (kb0, (kb0 >= 0) & block_matters(jnp.maximum(kb0, 0))))

        for h, hl in enumerate(heads):
            acc = acc_ref[h]
            lane = lax.broadcasted_iota(jnp.int32, acc.shape, 1)
            l = jnp.sum(jnp.where(lane == LANE_ONE_V, acc, 0.0), axis=-1, keepdims=True)
            o_ref[0, pl.ds(q0, tq), hl] = _head_norm(acc / l, g_ref[...]).astype(o_ref.dtype)
        return carry

    lax.fori_loop(0, s_len // tq, q_tile, 0)


def _attn_call(body, name, q, k, v, g, tq, width, nh):
    b, s, total = q.shape
    spec = pl.BlockSpec((1, s, width), lambda bi, hi: (bi, 0, hi))
    return pl.pallas_call(
        body,
        grid=(b, total // width),
        in_specs=[spec, spec, spec, pl.BlockSpec((1, LANES), lambda bi, hi: (0, 0))],
        out_specs=spec,
        out_shape=jax.ShapeDtypeStruct((b, s, total), BF16),
        scratch_shapes=[pltpu.VMEM((nh, tq, LANES), F32), pltpu.VMEM((nh, tq, LANES), F32)],
        compiler_params=_cparams("arbitrary", "arbitrary"),
        name=name,
    )(q, k, v, g)


def _fox(q, k, v, g, tq, tk):
    return _attn_call(functools.partial(_fox_kernel, tq=tq, tk=tk), "fox", q, k, v, g, tq,
                      HEADS_PER_STEP * LANES, HEADS_PER_STEP)


def _sb_kernel(q_ref, k_ref, v_ref, g_ref, o_ref, c_ref, acc_ref, *, tq, tk):
    s_len = q_ref.shape[1]
    n_diag = tq // tk
    jrow = lax.broadcasted_iota(jnp.int32, (tk, tk), 0)
    scol = lax.broadcasted_iota(jnp.int32, (tk, tk), 1)
    later = jnp.where(jrow > scol, 1.0, 0.0).astype(BF16)
    row = lax.broadcasted_iota(jnp.int32, (tq, tk), 0)
    col = lax.broadcasted_iota(jnp.int32, (tq, tk), 1)

    n_heads = LANES // HEAD_DIM
    lane_q = lax.broadcasted_iota(jnp.int32, (tq, LANES), 1)
    own = [(lane_q >= h * HEAD_DIM) & (lane_q < (h + 1) * HEAD_DIM) for h in range(n_heads)]

    def q_tile(qi, carry):
        q0 = pl.multiple_of(qi * tq, tq)
        q_all = q_ref[0, pl.ds(q0, tq), :]
        q = [jnp.where(own[h], q_all, jnp.zeros_like(q_all)) for h in range(n_heads)]
        c_ref[...] = jnp.zeros_like(c_ref)
        acc_ref[...] = jnp.zeros_like(acc_ref)

        def step(start, mask, r0=0):
            k = k_ref[0, pl.ds(start, tk), :]
            v = v_ref[0, pl.ds(start, tk), :]
            for h in range(n_heads):
                z = _dot_nt(q[h][r0:], k)
                sp = _softplus2(z)
                spm = sp if mask is None else jnp.where(mask[r0:], sp, 0.0)
                c_prev = c_ref[h, r0:, :]
                rest = _dot(spm.astype(BF16), later) + jnp.concatenate([c_prev] * (tk // LANES), axis=1)
                a = jnp.exp2(z - sp - rest)
                if mask is not None:
                    a = jnp.where(mask[r0:], a, 0.0)
                acc_ref[h, r0:, :] += _dot(a.astype(BF16), v)
                c_ref[h, r0:, :] = c_prev + jnp.sum(spm, axis=-1, keepdims=True)

        def more_matters():
            least = None
            for h in range(n_heads):
                ch = c_ref[h, :, 0:1]
                least = ch if least is None else jnp.minimum(least, ch)
            return jnp.min(least) < -EXP2_UNDERFLOW

        for d in reversed(range(n_diag)):
            step(pl.multiple_of(q0 + d * tk, tk), col + d * tk < row, d * tk)

        def cond(st):
            return st[1]

        def body(st):
            kb = st[0]
            step(pl.multiple_of(kb * tk, tk), None)
            nxt = kb - 1
            return nxt, (nxt >= 0) & more_matters()

        kb0 = qi * n_diag - 1
        lax.while_loop(cond, body, (kb0, (kb0 >= 0) & more_matters()))
        out = None
        for h in range(n_heads):
            oh = _head_norm(acc_ref[h], g_ref[...], h * HEAD_DIM)
            out = oh if out is None else out + oh
        o_ref[0, pl.ds(q0, tq), :] = out.astype(o_ref.dtype)
        return carry

    lax.fori_loop(0, s_len // tq, q_tile, 0)


def _sb(q, k, v, g, tq, tk):
    return _attn_call(functools.partial(_sb_kernel, tq=tq, tk=tk), "sb", q, k, v, g, tq, LANES,
                      LANES // HEAD_DIM)


def _outproj_kernel(x_ref, fo_ref, so_ref, mod_ref, g_ref, wo_ref, wc_ref,
                    x1_ref, h2t_ref, sc_ref):
    gt1 = mod_ref[0, 2:3, :]
    sh2 = mod_ref[0, 3:4, :]
    sc2 = mod_ref[0, 4:5, :]
    mixed = _dot(fo_ref[0], wo_ref[0:HP, :]) + _dot(so_ref[0], wo_ref[HP:HP + SBW, :])
    x1 = x_ref[0] + gt1 * mixed
    x1_ref[0] = x1
    inv = lax.rsqrt(jnp.mean(x1 * x1, axis=-1, keepdims=True) + EPS)
    h2 = (x1 * inv) * g_ref[...] * (1.0 + sc2) + sh2
    th = h2.T.astype(BF16)
    h2t_ref[...] = th
    sc_ref[...] = _dot(wc_ref[...], th)


def _outproj(x, fo, so, mod3, g_ffn, w_out_p, wc, tm):
    b, s, d = x.shape
    t = b * s
    nsc = wc.shape[0]
    const = lambda shape: pl.BlockSpec(shape, lambda bi, si: (0,) * len(shape),
                                       pipeline_mode=pl.Buffered(1))
    spb = s // tm
    return pl.pallas_call(
        _outproj_kernel,
        grid=(b, spb),
        in_specs=[pl.BlockSpec((1, tm, d), lambda bi, si: (bi, si, 0)),
                  pl.BlockSpec((1, tm, HP), lambda bi, si: (bi, si, 0)),
                  pl.BlockSpec((1, tm, SBW), lambda bi, si: (bi, si, 0)),
                  pl.BlockSpec((1, N_MOD, d), lambda bi, si: (bi, 0, 0)),
                  const((1, d)),
                  const((HP + SBW, d)),
                  const((nsc, d))],
        out_specs=[pl.BlockSpec((1, tm, d), lambda bi, si: (bi, si, 0)),
                   pl.BlockSpec((d, tm), lambda bi, si: (0, bi * spb + si)),
                   pl.BlockSpec((nsc, tm), lambda bi, si: (0, bi * spb + si))],
        out_shape=[jax.ShapeDtypeStruct((b, s, d), F32),
                   jax.ShapeDtypeStruct((d, t), BF16),
                   jax.ShapeDtypeStruct((nsc, t), F32)],
        compiler_params=_cparams("arbitrary", "arbitrary"),
        name="outproj",
    )(x, fo, so, mod3, g_ffn, w_out_p, wc)


TAKEN = -(2.0 ** 126)
TAKEN_STEP = 2.0 ** -10


def _top16(s, exact):
    n, t = s.shape
    idx = lax.broadcasted_iota(jnp.int32, (n, t), 0).astype(F32)
    krow = lax.broadcasted_iota(jnp.int32, (PEER_TOPK, t), 0)
    vals = jnp.zeros((PEER_TOPK, t), F32)
    for k in range(PEER_TOPK):
        m = jnp.max(s, axis=0, keepdims=True)
        hit = s == m
        if exact:
            hit = idx == jnp.min(jnp.where(hit, idx, float(n)), axis=0, keepdims=True)
        s = jnp.where(hit, TAKEN * (1.0 + k * TAKEN_STEP), s)
        vals = jnp.where(krow == k, m, vals)
    rank = jnp.where(s <= TAKEN, (s * (1.0 / TAKEN) - 1.0) * (1.0 / TAKEN_STEP), float(PEER_TOPK))
    return vals, rank


SUBLANES = 8
N_CAND = PEER_TOPK + (SUBLANES - 1) * SUBLANES + SUBLANES


def _cand_flat_pos():
    r = lax.broadcasted_iota(jnp.int32, (N_CAND, LANES), 0)
    mid_lo, hi_lo = PEER_TOPK, N_CAND - SUBLANES
    sub_shift = SUBLANES.bit_length() - 1
    mid = (jnp.right_shift(r - mid_lo, sub_shift) + 1) * PEER_TOPK + jnp.bitwise_and(r - mid_lo, SUBLANES - 1)
    hi = (r - hi_lo + SUBLANES) * PEER_TOPK
    return jnp.where(r < mid_lo, r, jnp.where(r < hi_lo, mid, hi)).astype(F32)


def _pair_top16(v1, v2, r1, exact):
    cand = jnp.concatenate(
        [v1[0:1, :] + v2]
        + [v1[a:a + 1, :] + v2[0:SUBLANES, :] for a in range(1, SUBLANES)]
        + [v1[SUBLANES:, :] + v2[0:1, :]], axis=0)
    pos = _cand_flat_pos() if exact else None
    c0 = cand[0:1, :]
    zsum = jnp.zeros((1, LANES), F32)
    for k in range(PEER_TOPK):
        m = jnp.max(cand, axis=0, keepdims=True)
        hit = cand == m
        if exact:
            hit = pos == jnp.min(jnp.where(hit, pos, float(PEER_TOPK * PEER_TOPK)), axis=0, keepdims=True)
        cand = jnp.where(hit, NEG_INF, cand)
        zsum = zsum + jnp.exp(m - c0)
    taken = jnp.where(cand == NEG_INF, 1.0, 0.0)
    n_taken = jnp.sum(taken, axis=0, keepdims=True)
    lrow = jnp.zeros(r1.shape, F32)
    for a in range(PEER_TOPK):
        if a == 0:
            cnt_a = jnp.sum(taken[0:PEER_TOPK, :], axis=0, keepdims=True)
        elif a < SUBLANES:
            lo = PEER_TOPK + SUBLANES * (a - 1)
            cnt_a = jnp.sum(taken[lo:lo + SUBLANES, :], axis=0, keepdims=True)
        else:
            lo = N_CAND - SUBLANES + a - SUBLANES
            cnt_a = taken[lo:lo + 1, :]
        lrow = jnp.where(r1 == float(a), cnt_a, lrow)
    return lrow, zsum, n_taken


def _topk_kernel(sc_ref, r2_ref, e2_ref, l1_ref, e1_ref):
    n = PEER_N_KEYS
    tl = sc_ref.shape[1]
    ngrp = tl // LANES

    def run(item, exact):
        hd = lax.div(item, ngrp)
        gi = lax.rem(item, ngrp)
        lanes = pl.ds(pl.multiple_of(gi * LANES, LANES), LANES)
        s1 = sc_ref[pl.ds(pl.multiple_of(hd * 2 * n, n), n), lanes]
        s2 = sc_ref[pl.ds(pl.multiple_of(hd * 2 * n + n, n), n), lanes]
        v1, r1 = _top16(s1, exact)
        v2, r2 = _top16(s2, exact)
        lrow, zsum, n_taken = _pair_top16(v1, v2, r1, exact)
        r2_ref[hd, :, lanes] = pltpu.bitcast(r2.astype(BF16), jnp.uint32)
        e2_ref[hd, :, lanes] = pltpu.bitcast(jnp.exp(s2 - v2[0:1, :]).astype(BF16), jnp.uint32)
        l1_ref[hd, :, lanes] = _dup_bf16_words(lrow)
        e1_ref[hd, :, lanes] = _dup_bf16_words(jnp.exp(s1 - v1[0:1, :]) / zsum)
        k = float(PEER_TOPK)
        picked = lambda r: jnp.sum(jnp.where(r < k, 1.0, 0.0), axis=0, keepdims=True)
        return jnp.abs(picked(r1) - k) + jnp.abs(picked(r2) - k) + jnp.abs(n_taken - k)

    together = 2 if (PEER_HEADS * ngrp) % 2 == 0 else 1

    def body(it, carry):
        items = [it * together + u for u in range(together)]
        excess = [run(item, exact=False) for item in items]
        for item, ex in zip(items, excess):
            @pl.when(jnp.max(ex) > 0.0)
            def _():
                run(item, exact=True)

        return carry

    lax.fori_loop(0, PEER_HEADS * ngrp // together, body, 0)


def _topk(scores_t, tl):
    nsc, t = scores_t.shape
    spec = lambda rows: pl.BlockSpec((PEER_HEADS, rows, tl), lambda i: (0, 0, i))
    sds = lambda rows: jax.ShapeDtypeStruct((PEER_HEADS, rows, t), jnp.uint32)
    half, full = PEER_N_KEYS // 2, PEER_N_KEYS
    return pl.pallas_call(
        _topk_kernel,
        grid=(t // tl,),
        in_specs=[pl.BlockSpec((nsc, tl), lambda i: (0, i))],
        out_specs=[spec(half), spec(half), spec(full), spec(full)],
        out_shape=[sds(half), sds(half), sds(full), sds(full)],
        compiler_params=_cparams("arbitrary"),
        name="topk",
    )(scores_t)


def _gelu(x):
    return 0.5 * x * (1.0 + lax.erf(x * (2.0 ** -0.5)))


def _dup_bf16_words(x):
    bits = pltpu.bitcast(x.astype(BF16).astype(F32), jnp.uint32)
    return bits | (bits >> 16)


def _bf16_rows(words, n):
    return pltpu.bitcast(jnp.broadcast_to(words, (n // 2, words.shape[1])), BF16)


def _peer_kernel(h2t_ref, dn_ref, up_ref, r2_ref, e2_ref, l1_ref, e1_ref, x1_ref, mod_ref, gf_ref,
                 o_ref, acc_ref, st_ref, pw_ref, *, rows_per_chunk):
    j = pl.program_id(2)
    n = PEER_N_KEYS

    @pl.when(j == 0)
    def _():
        acc_ref[...] = jnp.zeros_like(acc_ref)

    st_ref[...] = _dot(dn_ref[...], h2t_ref[...])
    zero = jnp.zeros((n, LANES), BF16)
    pair = 2
    for g in range(h2t_ref.shape[1] // LANES):
        lanes = slice(g * LANES, (g + 1) * LANES)
        for i1 in range(0, rows_per_chunk, pair):
            w = [None] * pair
            for hd in range(PEER_HEADS):
                r2 = pltpu.bitcast(r2_ref[hd, :, lanes], BF16)
                e2 = pltpu.bitcast(e2_ref[hd, :, lanes], BF16)
                for u in range(pair):
                    cnt = _bf16_rows(l1_ref[hd, i1 + u:i1 + u + 1, lanes], n)
                    e1 = _bf16_rows(e1_ref[hd, i1 + u:i1 + u + 1, lanes], n)
                    wh = jnp.where(r2 < cnt, e2, zero) * e1
                    w[u] = wh if w[u] is None else w[u] + wh
            for u in range(pair):
                rows = slice((i1 + u) * n, (i1 + u + 1) * n)
                pw_ref[rows, lanes] = _gelu(st_ref[rows, lanes].astype(BF16)) * w[u]
    acc_ref[...] += _dot(up_ref[...], pw_ref[...])

    @pl.when(j == pl.num_programs(2) - 1)
    def _():
        gt2 = mod_ref[0, 5:6, :]
        x2 = x1_ref[0] + gt2 * acc_ref[...].T
        inv = lax.rsqrt(jnp.mean(x2 * x2, axis=-1, keepdims=True) + EPS)
        o_ref[0] = (x2 * inv) * gf_ref[...]


def _peer(h2t, down, up_t, r2, e2, l1, e1, x1, mod3, g_final, tt, ec):
    b, s, d = x1.shape
    ne = down.shape[0]
    rpc = ec // PEER_N_KEYS
    spb = s // tt
    tok = lambda bi, si, j: (0, 0, bi * spb + si)
    gate_tab = pl.BlockSpec((PEER_HEADS, rpc, tt), lambda bi, si, j: (0, j, bi * spb + si))
    return pl.pallas_call(
        functools.partial(_peer_kernel, rows_per_chunk=rpc),
        grid=(b, spb, ne // ec),
        in_specs=[pl.BlockSpec((d, tt), lambda bi, si, j: (0, bi * spb + si)),
                  pl.BlockSpec((ec, d), lambda bi, si, j: (j, 0)),
                  pl.BlockSpec((d, ec), lambda bi, si, j: (0, j)),
                  pl.BlockSpec((PEER_HEADS, PEER_N_KEYS // 2, tt), tok),
                  pl.BlockSpec((PEER_HEADS, PEER_N_KEYS // 2, tt), tok),
                  gate_tab, gate_tab,
                  pl.BlockSpec((1, tt, d), lambda bi, si, j: (bi, si, 0)),
                  pl.BlockSpec((1, N_MOD, d), lambda bi, si, j: (bi, 0, 0)),
                  pl.BlockSpec((1, d), lambda bi, si, j: (0, 0))],
        out_specs=pl.BlockSpec((1, tt, d), lambda bi, si, j: (bi, si, 0)),
        out_shape=jax.ShapeDtypeStruct((b, s, d), F32),
        scratch_shapes=[pltpu.VMEM((d, tt), F32), pltpu.VMEM((ec, tt), F32), pltpu.VMEM((ec, tt), BF16)],
        compiler_params=_cparams("arbitrary", "arbitrary", "arbitrary"),
        name="peer",
    )(h2t, down, up_t, r2, e2, l1, e1, x1, mod3, g_final)


def _pad_heads(w):
    d = w.shape[0]
    w = w.reshape(d, N_HEADS, HEAD_DIM)
    return jnp.pad(w, ((0, 0), (0, 0), (0, LANES - HEAD_DIM))).reshape(d, HP)


def _bias_constants():
    pq = np.zeros((3 * LANES, HP), np.float32)
    pk = np.zeros((3 * LANES, HP), np.float32)
    cq = np.zeros((1, HP), np.float32)
    ck = np.zeros((1, HP), np.float32)
    cv = np.zeros((1, HP), np.float32)
    for h in range(N_HEADS):
        for part in range(3):
            pq[part * LANES + h, h * LANES + LANE_F_Q + part] = 1.0
            pk[part * LANES + h, h * LANES + LANE_ONE_Q + part] = -1.0
            cq[0, h * LANES + LANE_ONE_Q + part] = 1.0
            ck[0, h * LANES + LANE_F_Q + part] = 1.0
        cv[0, h * LANES + LANE_ONE_V] = 1.0
    return (jnp.asarray(pq, BF16), jnp.asarray(pk, BF16), jnp.asarray(cq), jnp.asarray(ck), jnp.asarray(cv))


PROJ_TOKENS = 512
ATTN_Q_TOKENS = 1024
ATTN_K_TOKENS = 256
TOPK_TOKENS = 512
PEER_TOKENS = 1024
PEER_EXPERTS = 1024


def _tile(n, pref):
    return pref if n % pref == 0 else n


def kernel(x, c, w_ada, b_ada, g_norm_mix, w_in, b_forget, g_out_fox, g_out_sb, w_out, g_norm_ffn,
           w_query, sub_keys, expert_down, expert_up, g_final):
    b, s, d = x.shape
    assert w_ada.shape[0] == 1, "single-layer block: the final RMSNorm is fused into the PEER kernel"
    fw = N_HEADS * HEAD_DIM
    scale = HEAD_DIM ** -0.5 * LOG2E
    pq, pk, cq, ck, cv = _bias_constants()
    tm = _tile(s, PROJ_TOKENS)
    tq, tk = _tile(s, ATTN_Q_TOKENS), _tile(s, ATTN_K_TOKENS)

    mod = _ada(jnp.pad(c, ((0, -b % 8), (0, 0))), w_ada[0], b_ada[0][None, :])[:b]
    mod3 = mod.reshape(b, N_MOD, d)

    wi = w_in[0]
    o1 = 3 * fw
    o2 = o1 + N_HEADS
    w_all = jnp.concatenate([
        _pad_heads(wi[:, 0:fw] * scale), _pad_heads(wi[:, fw:2 * fw]), _pad_heads(wi[:, 2 * fw:o1]),
        wi[:, o2:o2 + fw] * scale, wi[:, o2 + fw:o2 + 3 * fw]], axis=1).astype(BF16)
    wf = jnp.pad(wi[:, o1:o2], ((0, 0), (0, LANES - N_HEADS)))
    wf_hi = wf.astype(BF16)
    wf_lo = (wf - wf_hi.astype(F32)).astype(BF16)
    b_f = jnp.pad(b_forget[0], (0, LANES - N_HEADS))[None, :]
    fq, fk, fv, sq, sk, sv = _inproj(x, mod3, g_norm_mix[0][None, :], w_all, wf_hi, wf_lo, b_f,
                                     pq, pk, cq, ck, cv, tm)

    g_fox = jnp.pad(g_out_fox[0], (0, LANES - HEAD_DIM))[None, :]
    g_sb = jnp.tile(g_out_sb[0], LANES // HEAD_DIM)[None, :]
    fo = _fox(fq, fk, fv, g_fox, tq, tk)
    so = _sb(sq, sk, sv, g_sb, tq, tk)

    wo = w_out[0]
    w_out_fox = jnp.pad(wo[:fw].reshape(N_HEADS, HEAD_DIM, d),
                        ((0, 0), (0, LANES - HEAD_DIM), (0, 0))).reshape(HP, d)
    w_out_p = jnp.concatenate([w_out_fox, wo[fw:]], axis=0).astype(BF16)
    wc = _wc(sub_keys[0].reshape(PEER_HEADS * 2, PEER_N_KEYS, LANES), w_query[0])
    x1, h2t, scores_t = _outproj(x, fo, so, mod3, g_norm_ffn[0][None, :], w_out_p, wc.astype(BF16), tm)

    r2, e2, l1, e1 = _topk(scores_t, _tile(b * s, TOPK_TOKENS))
    return _peer(h2t, expert_down[0].astype(BF16), expert_up[0].T.astype(BF16), r2, e2, l1, e1,
                 x1, mod3, g_final[None, :], _tile(s, PEER_TOKENS), PEER_EXPERTS)
```

```python
import functools

import numpy as np
import jax
import jax.numpy as jnp
from jax import lax
from jax.experimental import pallas as pl
from jax.experimental.pallas import tpu as pltpu

F32 = jnp.float32
BF16 = jnp.bfloat16

HEAD_DIM = 64
N_HEADS = 8
PEER_HEADS = 8
PEER_N_KEYS = 128
PEER_TOPK = 16
N_MOD = 6
EPS = 1e-6

LANES = 128
VMEM_LIMIT = 56 * 1024 * 1024

HP = N_HEADS * LANES
SBW = N_HEADS * HEAD_DIM
LANE_ONE_Q = HEAD_DIM
LANE_F_Q = HEAD_DIM + 3
LANE_ONE_V = HEAD_DIM
NEG_INF = float("-inf")
LOG2E = 1.4426950408889634
EXP2_UNDERFLOW = -153.0


def _cparams(*sem):
    return pltpu.CompilerParams(dimension_semantics=sem, vmem_limit_bytes=VMEM_LIMIT)


def _dot(a, b):
    return jnp.dot(a, b, preferred_element_type=F32)


def _dot_nt(a, b):
    return lax.dot_general(a, b, (((1,), (1,)), ((), ())), preferred_element_type=F32)


def _split2(x):
    hi = x.astype(BF16)
    lo = (x - hi.astype(F32)).astype(BF16)
    return hi, lo


def _split3(x):
    hi = x.astype(BF16)
    r = x - hi.astype(F32)
    mid = r.astype(BF16)
    lo = (r - mid.astype(F32)).astype(BF16)
    return hi, mid, lo


def _softplus2(z):
    return jnp.maximum(z, 0.0) + jnp.log2(1.0 + jnp.exp2(-jnp.abs(z)))


def _ada_kernel(c_ref, w_ref, b_ref, o_ref):
    c = c_ref[...]
    a = c / (1.0 + jnp.exp(-c))
    ah, al = _split2(a)
    wh, wl = _split2(w_ref[...])
    o_ref[...] = _dot(ah, wh) + _dot(ah, wl) + _dot(al, wh) + b_ref[...]


def _ada(c_pad, w_ada, b_ada):
    rows, d = c_pad.shape
    n = w_ada.shape[1]
    bn = 1024
    return pl.pallas_call(
        _ada_kernel,
        grid=(n // bn,),
        in_specs=[pl.BlockSpec((rows, d), lambda j: (0, 0)),
                  pl.BlockSpec((d, bn), lambda j: (0, j)),
                  pl.BlockSpec((1, bn), lambda j: (0, j))],
        out_specs=pl.BlockSpec((rows, bn), lambda j: (0, j)),
        out_shape=jax.ShapeDtypeStruct((rows, n), F32),
        compiler_params=_cparams("arbitrary"),
        name="ada",
    )(c_pad, w_ada, b_ada)


def _wc_kernel(sk_ref, wq_ref, o_ref):
    sh, sl = _split2(sk_ref[0])
    wh, wl = _split2(wq_ref[...])
    o_ref[...] = _dot_nt(sh, wh) + _dot_nt(sh, wl) + _dot_nt(sl, wh)


def _wc(sub_keys, w_query):
    d = w_query.shape[0]
    nhp = sub_keys.shape[0]
    return pl.pallas_call(
        _wc_kernel,
        grid=(nhp,),
        in_specs=[pl.BlockSpec((1, PEER_N_KEYS, LANES), lambda i: (i, 0, 0)),
                  pl.BlockSpec((d, LANES), lambda i: (0, i))],
        out_specs=pl.BlockSpec((PEER_N_KEYS, d), lambda i: (i, 0)),
        out_shape=jax.ShapeDtypeStruct((nhp * PEER_N_KEYS, d), F32),
        compiler_params=_cparams("arbitrary"),
        name="wc",
    )(sub_keys, w_query)


def _inproj_kernel(x_ref, mod_ref, g_ref, w_ref, wfh_ref, wfl_ref, bf_ref, pq_ref, pk_ref,
                   cq_ref, ck_ref, cv_ref,
                   fq_ref, fk_ref, fv_ref, sq_ref, sk_ref, sv_ref, carry_ref):
    tm = x_ref.shape[1]

    @pl.when(pl.program_id(1) == 0)
    def _():
        carry_ref[...] = jnp.zeros_like(carry_ref)

    x = x_ref[0]
    inv = lax.rsqrt(jnp.mean(x * x, axis=-1, keepdims=True) + EPS)
    sh1 = mod_ref[0, 0:1, :]
    sc1 = mod_ref[0, 1:2, :]
    h = (x * inv) * g_ref[...] * (1.0 + sc1) + sh1
    hh, hl = _split2(h)

    fl = _dot(hh, wfh_ref[...]) + _dot(hh, wfl_ref[...]) + _dot(hl, wfh_ref[...]) + bf_ref[...]
    lf = jnp.minimum(fl, 0.0) - jnp.log(1.0 + jnp.exp(-jnp.abs(fl)))
    row = lax.broadcasted_iota(jnp.int32, (tm, tm), 0)
    col = lax.broadcasted_iota(jnp.int32, (tm, tm), 1)
    tri = jnp.where(col <= row, 1.0, 0.0).astype(BF16)
    l0, l1, l2 = _split3(lf)
    fcum = _dot(tri, l0) + _dot(tri, l1) + _dot(tri, l2) + carry_ref[...]
    carry_ref[...] = fcum[tm - 1:tm, :]
    f0, f1, f2 = _split3(fcum * LOG2E)
    fcat = jnp.concatenate([f0, f1, f2], axis=1)

    fq_ref[0] = (_dot(hh, w_ref[:, 0 * HP:1 * HP]) + _dot(fcat, pq_ref[...]) + cq_ref[...]).astype(BF16)
    fk_ref[0] = (_dot(hh, w_ref[:, 1 * HP:2 * HP]) + _dot(fcat, pk_ref[...]) + ck_ref[...]).astype(BF16)
    fv_ref[0] = (_dot(hh, w_ref[:, 2 * HP:3 * HP]) + cv_ref[...]).astype(BF16)
    sq_ref[0] = _dot(hh, w_ref[:, 3 * HP + 0 * SBW:3 * HP + 1 * SBW]).astype(BF16)
    sk_ref[0] = _dot(hh, w_ref[:, 3 * HP + 1 * SBW:3 * HP + 2 * SBW]).astype(BF16)
    sv_ref[0] = _dot(hh, w_ref[:, 3 * HP + 2 * SBW:3 * HP + 3 * SBW]).astype(BF16)


def _inproj(x, mod3, g_mix, w_all, wf_hi, wf_lo, b_f, pq, pk, cq, ck, cv, tm):
    b, s, d = x.shape
    const = lambda shape: pl.BlockSpec(shape, lambda bi, si: (0,) * len(shape),
                                       pipeline_mode=pl.Buffered(1))
    out_spec = lambda w: pl.BlockSpec((1, tm, w), lambda bi, si: (bi, si, 0))
    out_sds = lambda w: jax.ShapeDtypeStruct((b, s, w), BF16)
    widths = [HP] * 3 + [SBW] * 3
    return pl.pallas_call(
        _inproj_kernel,
        grid=(b, s // tm),
        in_specs=[pl.BlockSpec((1, tm, d), lambda bi, si: (bi, si, 0)),
                  pl.BlockSpec((1, N_MOD, d), lambda bi, si: (bi, 0, 0)),
                  const((1, d)),
                  const((d, sum(widths))),
                  const((d, LANES)), const((d, LANES)), const((1, LANES)),
                  const((3 * LANES, HP)), const((3 * LANES, HP)),
                  const((1, HP)), const((1, HP)), const((1, HP))],
        out_specs=[out_spec(w) for w in widths],
        out_shape=[out_sds(w) for w in widths],
        scratch_shapes=[pltpu.VMEM((1, LANES), F32)],
        compiler_params=_cparams("arbitrary", "arbitrary"),
        name="inproj",
    )(x, mod3, g_mix, w_all, wf_hi, wf_lo, b_f, pq, pk, cq, ck, cv)


def _head_norm(o, g, lo=0):
    lane = lax.broadcasted_iota(jnp.int32, o.shape, 1)
    o = jnp.where((lane >= lo) & (lane < lo + HEAD_DIM), o, 0.0)
    ms = jnp.sum(o * o, axis=-1, keepdims=True) * (1.0 / HEAD_DIM)
    return o * lax.rsqrt(ms + EPS) * g


HEADS_PER_STEP = 2


def _head_slices(ref):
    return [slice(h * LANES, (h + 1) * LANES) for h in range(ref.shape[2] // LANES)]


def _fox_kernel(q_ref, k_ref, v_ref, g_ref, o_ref, m_ref, acc_ref, *, tq, tk):
    s_len = q_ref.shape[1]
    heads = _head_slices(q_ref)
    n_diag = tq // tk
    lane1 = lax.broadcasted_iota(jnp.int32, (1, LANES), 1)
    head_lanes = lane1 < HEAD_DIM
    row = lax.broadcasted_iota(jnp.int32, (tq, tk), 0)
    col = lax.broadcasted_iota(jnp.int32, (tq, tk), 1)

    def knorm(i, mx):
        out = []
        for hl, m in zip(heads, mx):
            kb = k_ref[0, pl.ds(pl.multiple_of(i * tk, tk), tk), hl].astype(F32)
            kb = jnp.where(head_lanes, kb, 0.0)
            n2 = jnp.sum(kb * kb, axis=-1, keepdims=True)
            out.append(jnp.maximum(m, jnp.max(n2, axis=0, keepdims=True)))
        return tuple(out)

    kmax = [jnp.sqrt(m) for m in
            lax.fori_loop(0, s_len // tk, knorm, tuple(jnp.zeros((1, 1), F32) for _ in heads))]

    def q_tile(qi, carry):
        q0 = pl.multiple_of(qi * tq, tq)
        q = [q_ref[0, pl.ds(q0, tq), hl] for hl in heads]
        reach = []
        for h, qh in enumerate(q):
            qf = qh.astype(F32)
            qn = jnp.sqrt(jnp.sum(jnp.where(head_lanes, qf * qf, 0.0), axis=-1, keepdims=True))
            f_t = jnp.sum(jnp.where((lane1 >= LANE_F_Q) & (lane1 < LANE_F_Q + 3), qf, 0.0),
                          axis=-1, keepdims=True)
            reach.append(qn * kmax[h] + f_t)
        m_ref[...] = jnp.full_like(m_ref, NEG_INF)
        acc_ref[...] = jnp.zeros_like(acc_ref)

        def step(start, mask, r0=0):
            for h, hl in enumerate(heads):
                k = k_ref[0, pl.ds(start, tk), hl]
                v = v_ref[0, pl.ds(start, tk), hl]
                s = _dot_nt(q[h][r0:], k)
                if mask is not None:
                    s = jnp.where(mask[r0:], s, NEG_INF)
                m_prev = m_ref[h, r0:, :]
                m_new = jnp.maximum(m_prev, jnp.max(s, axis=-1, keepdims=True))
                p = jnp.exp2(s - jnp.concatenate([m_new] * (tk // LANES), axis=1))
                acc_ref[h, r0:, :] = jnp.exp2(m_prev - m_new) * acc_ref[h, r0:, :] + _dot(p.astype(BF16), v)
                m_ref[h, r0:, :] = m_new

        def block_matters(kb):
            worst = None
            for h, hl in enumerate(heads):
                last = k_ref[0, pl.ds(pl.multiple_of(kb * tk + tk - 16, 16), 16), hl][15:16, :].astype(F32)
                neg_f_last = jnp.sum(jnp.where((lane1 >= LANE_ONE_Q) & (lane1 < LANE_ONE_Q + 3), last, 0.0),
                                     axis=-1, keepdims=True)
                bound = reach[h] + neg_f_last - m_ref[h, :, 0:1]
                worst = bound if worst is None else jnp.maximum(worst, bound)
            return jnp.max(worst) > EXP2_UNDERFLOW

        for d in range(n_diag):
            step(pl.multiple_of(q0 + d * tk, tk), col + d * tk <= row, d * tk)

        def cond(st):
            return st[1]

        def body(st):
            kb = st[0]
            step(pl.multiple_of(kb * tk, tk), None)
            nxt = kb - 1
            return nxt, (nxt >= 0) & block_matters(jnp.maximum(nxt, 0))

        kb0 = qi * n_diag - 1
        lax.while_loop(cond, body, (kb0, (kb0 >= 0) & block_matters(jnp.maximum(kb0, 0))))

        for h, hl in enumerate(heads):
            acc = acc_ref[h]
            lane = lax.broadcasted_iota(jnp.int32, acc.shape, 1)
            l = jnp.sum(jnp.where(lane == LANE_ONE_V, acc, 0.0), axis=-1, keepdims=True)
            o_ref[0, pl.ds(q0, tq), hl] = _head_norm(acc / l, g_ref[...]).astype(o_ref.dtype)
        return carry

    lax.fori_loop(0, s_len // tq, q_tile, 0)


def _attn_call(body, name, q, k, v, g, tq, width, nh):
    b, s, total = q.shape
    spec = pl.BlockSpec((1, s, width), lambda bi, hi: (bi, 0, hi))
    return pl.pallas_call(
        body,
        grid=(b, total // width),
        in_specs=[spec, spec, spec, pl.BlockSpec((1, LANES), lambda bi, hi: (0, 0))],
        out_specs=spec,
        out_shape=jax.ShapeDtypeStruct((b, s, total), BF16),
        scratch_shapes=[pltpu.VMEM((nh, tq, LANES), F32), pltpu.VMEM((nh, tq, LANES), F32)],
        compiler_params=_cparams("arbitrary", "arbitrary"),
        name=name,
    )(q, k, v, g)


def _fox(q, k, v, g, tq, tk):
    return _attn_call(functools.partial(_fox_kernel, tq=tq, tk=tk), "fox", q, k, v, g, tq,
                      HEADS_PER_STEP * LANES, HEADS_PER_STEP)


def _sb_kernel(q_ref, k_ref, v_ref, g_ref, o_ref, c_ref, acc_ref, *, tq, tk):
    s_len = q_ref.shape[1]
    n_diag = tq // tk
    jrow = lax.broadcasted_iota(jnp.int32, (tk, tk), 0)
    scol = lax.broadcasted_iota(jnp.int32, (tk, tk), 1)
    later = jnp.where(jrow > scol, 1.0, 0.0).astype(BF16)
    row = lax.broadcasted_iota(jnp.int32, (tq, tk), 0)
    col = lax.broadcasted_iota(jnp.int32, (tq, tk), 1)

    n_heads = LANES // HEAD_DIM
    lane_q = lax.broadcasted_iota(jnp.int32, (tq, LANES), 1)
    own = [(lane_q >= h * HEAD_DIM) & (lane_q < (h + 1) * HEAD_DIM) for h in range(n_heads)]

    def q_tile(qi, carry):
        q0 = pl.multiple_of(qi * tq, tq)
        q_all = q_ref[0, pl.ds(q0, tq), :]
        q = [jnp.where(own[h], q_all, jnp.zeros_like(q_all)) for h in range(n_heads)]
        c_ref[...] = jnp.zeros_like(c_ref)
        acc_ref[...] = jnp.zeros_like(acc_ref)

        def step(start, mask, r0=0):
            k = k_ref[0, pl.ds(start, tk), :]
            v = v_ref[0, pl.ds(start, tk), :]
            for h in range(n_heads):
                z = _dot_nt(q[h][r0:], k)
                sp = _softplus2(z)
                spm = sp if mask is None else jnp.where(mask[r0:], sp, 0.0)
                c_prev = c_ref[h, r0:, :]
                rest = _dot(spm.astype(BF16), later) + jnp.concatenate([c_prev] * (tk // LANES), axis=1)
                a = jnp.exp2(z - sp - rest)
                if mask is not None:
                    a = jnp.where(mask[r0:], a, 0.0)
                acc_ref[h, r0:, :] += _dot(a.astype(BF16), v)
                c_ref[h, r0:, :] = c_prev + jnp.sum(spm, axis=-1, keepdims=True)

        def more_matters():
            least = None
            for h in range(n_heads):
                ch = c_ref[h, :, 0:1]
                least = ch if least is None else jnp.minimum(least, ch)
            return jnp.min(least) < -EXP2_UNDERFLOW

        for d in reversed(range(n_diag)):
            step(pl.multiple_of(q0 + d * tk, tk), col + d * tk < row, d * tk)

        def cond(st):
            return st[1]

        def body(st):
            kb = st[0]
            step(pl.multiple_of(kb * tk, tk), None)
            nxt = kb - 1
            return nxt, (nxt >= 0) & more_matters()

        kb0 = qi * n_diag - 1
        lax.while_loop(cond, body, (kb0, (kb0 >= 0) & more_matters()))
        out = None
        for h in range(n_heads):
            oh = _head_norm(acc_ref[h], g_ref[...], h * HEAD_DIM)
            out = oh if out is None else out + oh
        o_ref[0, pl.ds(q0, tq), :] = out.astype(o_ref.dtype)
        return carry

    lax.fori_loop(0, s_len // tq, q_tile, 0)


def _sb(q, k, v, g, tq, tk):
    return _attn_call(functools.partial(_sb_kernel, tq=tq, tk=tk), "sb", q, k, v, g, tq, LANES,
                      LANES // HEAD_DIM)


def _outproj_kernel(x_ref, fo_ref, so_ref, mod_ref, g_ref, wo_ref, wc_ref,
                    x1_ref, h2t_ref, sc_ref):
    gt1 = mod_ref[0, 2:3, :]
    sh2 = mod_ref[0, 3:4, :]
    sc2 = mod_ref[0, 4:5, :]
    mixed = _dot(fo_ref[0], wo_ref[0:HP, :]) + _dot(so_ref[0], wo_ref[HP:HP + SBW, :])
    x1 = x_ref[0] + gt1 * mixed
    x1_ref[0] = x1
    inv = lax.rsqrt(jnp.mean(x1 * x1, axis=-1, keepdims=True) + EPS)
    h2 = (x1 * inv) * g_ref[...] * (1.0 + sc2) + sh2
    th = h2.T.astype(BF16)
    h2t_ref[...] = th
    sc_ref[...] = _dot(wc_ref[...], th)


def _outproj(x, fo, so, mod3, g_ffn, w_out_p, wc, tm):
    b, s, d = x.shape
    t = b * s
    nsc = wc.shape[0]
    const = lambda shape: pl.BlockSpec(shape, lambda bi, si: (0,) * len(shape),
                                       pipeline_mode=pl.Buffered(1))
    spb = s // tm
    return pl.pallas_call(
        _outproj_kernel,
        grid=(b, spb),
        in_specs=[pl.BlockSpec((1, tm, d), lambda bi, si: (bi, si, 0)),
                  pl.BlockSpec((1, tm, HP), lambda bi, si: (bi, si, 0)),
                  pl.BlockSpec((1, tm, SBW), lambda bi, si: (bi, si, 0)),
                  pl.BlockSpec((1, N_MOD, d), lambda bi, si: (bi, 0, 0)),
                  const((1, d)),
                  const((HP + SBW, d)),
                  const((nsc, d))],
        out_specs=[pl.BlockSpec((1, tm, d), lambda bi, si: (bi, si, 0)),
                   pl.BlockSpec((d, tm), lambda bi, si: (0, bi * spb + si)),
                   pl.BlockSpec((nsc, tm), lambda bi, si: (0, bi * spb + si))],
        out_shape=[jax.ShapeDtypeStruct((b, s, d), F32),
                   jax.ShapeDtypeStruct((d, t), BF16),
                   jax.ShapeDtypeStruct((nsc, t), F32)],
        compiler_params=_cparams("arbitrary", "arbitrary"),
        name="outproj",
    )(x, fo, so, mod3, g_ffn, w_out_p, wc)


TAKEN = -(2.0 ** 126)
TAKEN_STEP = 2.0 ** -10


def _top16(s, exact):
    n, t = s.shape
    idx = lax.broadcasted_iota(jnp.int32, (n, t), 0).astype(F32)
    krow = lax.broadcasted_iota(jnp.int32, (PEER_TOPK, t), 0)
    vals = jnp.zeros((PEER_TOPK, t), F32)
    for k in range(PEER_TOPK):
        m = jnp.max(s, axis=0, keepdims=True)
        hit = s == m
        if exact:
            hit = idx == jnp.min(jnp.where(hit, idx, float(n)), axis=0, keepdims=True)
        s = jnp.where(hit, TAKEN * (1.0 + k * TAKEN_STEP), s)
        vals = jnp.where(krow == k, m, vals)
    rank = jnp.where(s <= TAKEN, (s * (1.0 / TAKEN) - 1.0) * (1.0 / TAKEN_STEP), float(PEER_TOPK))
    return vals, rank


SUBLANES = 8
N_CAND = PEER_TOPK + (SUBLANES - 1) * SUBLANES + SUBLANES


def _cand_flat_pos():
    r = lax.broadcasted_iota(jnp.int32, (N_CAND, LANES), 0)
    mid_lo, hi_lo = PEER_TOPK, N_CAND - SUBLANES
    sub_shift = SUBLANES.bit_length() - 1
    mid = (jnp.right_shift(r - mid_lo, sub_shift) + 1) * PEER_TOPK + jnp.bitwise_and(r - mid_lo, SUBLANES - 1)
    hi = (r - hi_lo + SUBLANES) * PEER_TOPK
    return jnp.where(r < mid_lo, r, jnp.where(r < hi_lo, mid, hi)).astype(F32)


def _pair_top16(v1, v2, r1, exact):
    cand = jnp.concatenate(
        [v1[0:1, :] + v2]
        + [v1[a:a + 1, :] + v2[0:SUBLANES, :] for a in range(1, SUBLANES)]
        + [v1[SUBLANES:, :] + v2[0:1, :]], axis=0)
    pos = _cand_flat_pos() if exact else None
    c0 = cand[0:1, :]
    zsum = jnp.zeros((1, LANES), F32)
    for k in range(PEER_TOPK):
        m = jnp.max(cand, axis=0, keepdims=True)
        hit = cand == m
        if exact:
            hit = pos == jnp.min(jnp.where(hit, pos, float(PEER_TOPK * PEER_TOPK)), axis=0, keepdims=True)
        cand = jnp.where(hit, NEG_INF, cand)
        zsum = zsum + jnp.exp(m - c0)
    taken = jnp.where(cand == NEG_INF, 1.0, 0.0)
    n_taken = jnp.sum(taken, axis=0, keepdims=True)
    lrow = jnp.zeros(r1.shape, F32)
    for a in range(PEER_TOPK):
        if a == 0:
            cnt_a = jnp.sum(taken[0:PEER_TOPK, :], axis=0, keepdims=True)
        elif a < SUBLANES:
            lo = PEER_TOPK + SUBLANES * (a - 1)
            cnt_a = jnp.sum(taken[lo:lo + SUBLANES, :], axis=0, keepdims=True)
        else:
            lo = N_CAND - SUBLANES + a - SUBLANES
            cnt_a = taken[lo:lo + 1, :]
        lrow = jnp.where(r1 == float(a), cnt_a, lrow)
    return lrow, zsum, n_taken


def _topk_kernel(sc_ref, r2_ref, e2_ref, l1_ref, e1_ref):
    n = PEER_N_KEYS
    tl = sc_ref.shape[1]
    ngrp = tl // LANES

    def run(item, exact):
        hd = lax.div(item, ngrp)
        gi = lax.rem(item, ngrp)
        lanes = pl.ds(pl.multiple_of(gi * LANES, LANES), LANES)
        s1 = sc_ref[pl.ds(pl.multiple_of(hd * 2 * n, n), n), lanes]
        s2 = sc_ref[pl.ds(pl.multiple_of(hd * 2 * n + n, n), n), lanes]
        v1, r1 = _top16(s1, exact)
        v2, r2 = _top16(s2, exact)
        lrow, zsum, n_taken = _pair_top16(v1, v2, r1, exact)
        r2_ref[hd, :, lanes] = pltpu.bitcast(r2.astype(BF16), jnp.uint32)
        e2_ref[hd, :, lanes] = pltpu.bitcast(jnp.exp(s2 - v2[0:1, :]).astype(BF16), jnp.uint32)
        l1_ref[hd, :, lanes] = _dup_bf16_words(lrow)
        e1_ref[hd, :, lanes] = _dup_bf16_words(jnp.exp(s1 - v1[0:1, :]) * (0.5 / zsum))
        k = float(PEER_TOPK)
        picked = lambda r: jnp.sum(jnp.where(r < k, 1.0, 0.0), axis=0, keepdims=True)
        return jnp.abs(picked(r1) - k) + jnp.abs(picked(r2) - k) + jnp.abs(n_taken - k)

    together = 2 if (PEER_HEADS * ngrp) % 2 == 0 else 1

    def body(it, carry):
        items = [it * together + u for u in range(together)]
        excess = [run(item, exact=False) for item in items]
        for item, ex in zip(items, excess):
            @pl.when(jnp.max(ex) > 0.0)
            def _():
                run(item, exact=True)

        return carry

    lax.fori_loop(0, PEER_HEADS * ngrp // together, body, 0)


def _topk(scores_t, tl):
    nsc, t = scores_t.shape
    spec = lambda rows: pl.BlockSpec((PEER_HEADS, rows, tl), lambda i: (0, 0, i))
    sds = lambda rows: jax.ShapeDtypeStruct((PEER_HEADS, rows, t), jnp.uint32)
    half, full = PEER_N_KEYS // 2, PEER_N_KEYS
    return pl.pallas_call(
        _topk_kernel,
        grid=(t // tl,),
        in_specs=[pl.BlockSpec((nsc, tl), lambda i: (0, i))],
        out_specs=[spec(half), spec(half), spec(full), spec(full)],
        out_shape=[sds(half), sds(half), sds(full), sds(full)],
        compiler_params=_cparams("arbitrary"),
        name="topk",
    )(scores_t)


def _gelu_x2(x):
    return x * (1.0 + lax.erf(x * (2.0 ** -0.5)))


def _dup_bf16_words(x):
    bits = pltpu.bitcast(x.astype(BF16).astype(F32), jnp.uint32)
    return bits | (bits >> 16)


def _bf16_rows(words, n):
    return pltpu.bitcast(jnp.broadcast_to(words, (n // 2, words.shape[1])), BF16)


def _peer_kernel(h2t_ref, dn_ref, up_ref, r2_ref, e2_ref, l1_ref, e1_ref, x1_ref, mod_ref, gf_ref,
                 o_ref, acc_ref, st_ref, pw_ref, *, rows_per_chunk):
    j = pl.program_id(2)
    n = PEER_N_KEYS

    @pl.when(j == 0)
    def _():
        acc_ref[...] = jnp.zeros_like(acc_ref)

    st_ref[...] = _dot(dn_ref[...], h2t_ref[...]).astype(BF16)
    zero = jnp.zeros((n, LANES), BF16)
    pair = 2
    for g in range(h2t_ref.shape[1] // LANES):
        lanes = slice(g * LANES, (g + 1) * LANES)
        for i1 in range(0, rows_per_chunk, pair):
            w = [None] * pair
            for hd in range(PEER_HEADS):
                r2 = pltpu.bitcast(r2_ref[hd, :, lanes], BF16)
                e2 = pltpu.bitcast(e2_ref[hd, :, lanes], BF16)
                for u in range(pair):
                    cnt = _bf16_rows(l1_ref[hd, i1 + u:i1 + u + 1, lanes], n)
                    e1 = _bf16_rows(e1_ref[hd, i1 + u:i1 + u + 1, lanes], n)
                    wh = jnp.where(r2 < cnt, e2, zero) * e1
                    w[u] = wh if w[u] is None else w[u] + wh
            for u in range(pair):
                rows = slice((i1 + u) * n, (i1 + u + 1) * n)
                pw_ref[rows, lanes] = _gelu_x2(st_ref[rows, lanes]) * w[u]
    acc_ref[...] += _dot(up_ref[...], pw_ref[...])

    @pl.when(j == pl.num_programs(2) - 1)
    def _():
        gt2 = mod_ref[0, 5:6, :]
        x2 = x1_ref[0] + gt2 * acc_ref[...].T
        inv = lax.rsqrt(jnp.mean(x2 * x2, axis=-1, keepdims=True) + EPS)
        o_ref[0] = (x2 * inv) * gf_ref[...]


def _peer(h2t, down, up_t, r2, e2, l1, e1, x1, mod3, g_final, tt, ec):
    b, s, d = x1.shape
    ne = down.shape[0]
    rpc = ec // PEER_N_KEYS
    spb = s // tt
    tok = lambda bi, si, j: (0, 0, bi * spb + si)
    gate_tab = pl.BlockSpec((PEER_HEADS, rpc, tt), lambda bi, si, j: (0, j, bi * spb + si))
    return pl.pallas_call(
        functools.partial(_peer_kernel, rows_per_chunk=rpc),
        grid=(b, spb, ne // ec),
        in_specs=[pl.BlockSpec((d, tt), lambda bi, si, j: (0, bi * spb + si)),
                  pl.BlockSpec((ec, d), lambda bi, si, j: (j, 0)),
                  pl.BlockSpec((d, ec), lambda bi, si, j: (0, j)),
                  pl.BlockSpec((PEER_HEADS, PEER_N_KEYS // 2, tt), tok),
                  pl.BlockSpec((PEER_HEADS, PEER_N_KEYS // 2, tt), tok),
                  gate_tab, gate_tab,
                  pl.BlockSpec((1, tt, d), lambda bi, si, j: (bi, si, 0)),
                  pl.BlockSpec((1, N_MOD, d), lambda bi, si, j: (bi, 0, 0)),
                  pl.BlockSpec((1, d), lambda bi, si, j: (0, 0))],
        out_specs=pl.BlockSpec((1, tt, d), lambda bi, si, j: (bi, si, 0)),
        out_shape=jax.ShapeDtypeStruct((b, s, d), F32),
        scratch_shapes=[pltpu.VMEM((d, tt), F32), pltpu.VMEM((ec, tt), BF16), pltpu.VMEM((ec, tt), BF16)],
        compiler_params=_cparams("arbitrary", "arbitrary", "arbitrary"),
        name="peer",
    )(h2t, down, up_t, r2, e2, l1, e1, x1, mod3, g_final)


def _pad_heads(w):
    d = w.shape[0]
    w = w.reshape(d, N_HEADS, HEAD_DIM)
    return jnp.pad(w, ((0, 0), (0, 0), (0, LANES - HEAD_DIM))).reshape(d, HP)


def _bias_constants():
    pq = np.zeros((3 * LANES, HP), np.float32)
    pk = np.zeros((3 * LANES, HP), np.float32)
    cq = np.zeros((1, HP), np.float32)
    ck = np.zeros((1, HP), np.float32)
    cv = np.zeros((1, HP), np.float32)
    for h in range(N_HEADS):
        for part in range(3):
            pq[part * LANES + h, h * LANES + LANE_F_Q + part] = 1.0
            pk[part * LANES + h, h * LANES + LANE_ONE_Q + part] = -1.0
            cq[0, h * LANES + LANE_ONE_Q + part] = 1.0
            ck[0, h * LANES + LANE_F_Q + part] = 1.0
        cv[0, h * LANES + LANE_ONE_V] = 1.0
    return (jnp.asarray(pq, BF16), jnp.asarray(pk, BF16), jnp.asarray(cq), jnp.asarray(ck), jnp.asarray(cv))


PROJ_TOKENS = 512
ATTN_Q_TOKENS = 1024
FOX_K_TOKENS = 512
SB_K_TOKENS = 256
TOPK_TOKENS = 512
PEER_TOKENS = 1024
PEER_EXPERTS = 1024


def _tile(n, pref):
    return pref if n % pref == 0 else n


def kernel(x, c, w_ada, b_ada, g_norm_mix, w_in, b_forget, g_out_fox, g_out_sb, w_out, g_norm_ffn,
           w_query, sub_keys, expert_down, expert_up, g_final):
    b, s, d = x.shape
    assert w_ada.shape[0] == 1, "single-layer block: the final RMSNorm is fused into the PEER kernel"
    fw = N_HEADS * HEAD_DIM
    scale = HEAD_DIM ** -0.5 * LOG2E
    pq, pk, cq, ck, cv = _bias_constants()
    tm = _tile(s, PROJ_TOKENS)
    tq = _tile(s, ATTN_Q_TOKENS)

    mod = _ada(jnp.pad(c, ((0, -b % 8), (0, 0))), w_ada[0], b_ada[0][None, :])[:b]
    mod3 = mod.reshape(b, N_MOD, d)

    wi = w_in[0]
    o1 = 3 * fw
    o2 = o1 + N_HEADS
    w_all = jnp.concatenate([
        _pad_heads(wi[:, 0:fw] * scale), _pad_heads(wi[:, fw:2 * fw]), _pad_heads(wi[:, 2 * fw:o1]),
        wi[:, o2:o2 + fw] * scale, wi[:, o2 + fw:o2 + 3 * fw]], axis=1).astype(BF16)
    wf = jnp.pad(wi[:, o1:o2], ((0, 0), (0, LANES - N_HEADS)))
    wf_hi = wf.astype(BF16)
    wf_lo = (wf - wf_hi.astype(F32)).astype(BF16)
    b_f = jnp.pad(b_forget[0], (0, LANES - N_HEADS))[None, :]
    fq, fk, fv, sq, sk, sv = _inproj(x, mod3, g_norm_mix[0][None, :], w_all, wf_hi, wf_lo, b_f,
                                     pq, pk, cq, ck, cv, tm)

    g_fox = jnp.pad(g_out_fox[0], (0, LANES - HEAD_DIM))[None, :]
    g_sb = jnp.tile(g_out_sb[0], LANES // HEAD_DIM)[None, :]
    fo = _fox(fq, fk, fv, g_fox, tq, _tile(s, FOX_K_TOKENS))
    so = _sb(sq, sk, sv, g_sb, tq, _tile(s, SB_K_TOKENS))

    wo = w_out[0]
    w_out_fox = jnp.pad(wo[:fw].reshape(N_HEADS, HEAD_DIM, d),
                        ((0, 0), (0, LANES - HEAD_DIM), (0, 0))).reshape(HP, d)
    w_out_p = jnp.concatenate([w_out_fox, wo[fw:]], axis=0).astype(BF16)
    wc = _wc(sub_keys[0].reshape(PEER_HEADS * 2, PEER_N_KEYS, LANES), w_query[0])
    x1, h2t, scores_t = _outproj(x, fo, so, mod3, g_norm_ffn[0][None, :], w_out_p, wc.astype(BF16), tm)

    r2, e2, l1, e1 = _topk(scores_t, _tile(b * s, TOPK_TOKENS))
    return _peer(h2t, expert_down[0].astype(BF16), expert_up[0].astype(BF16).T, r2, e2, l1, e1,
                 x1, mod3, g_final[None, :], _tile(s, PEER_TOKENS), PEER_EXPERTS)
```

```python
import functools

import numpy as np
import jax
import jax.numpy as jnp
from jax import lax
from jax.experimental import pallas as pl
from jax.experimental.pallas import tpu as pltpu

F32 = jnp.float32
BF16 = jnp.bfloat16

HEAD_DIM = 64
N_HEADS = 8
PEER_HEADS = 8
PEER_N_KEYS = 128
PEER_TOPK = 16
N_MOD = 6
EPS = 1e-6

LANES = 128
VMEM_LIMIT = 56 * 1024 * 1024

HP = N_HEADS * LANES
SBW = N_HEADS * HEAD_DIM
LANE_ONE_Q = HEAD_DIM
LANE_F_Q = HEAD_DIM + 3
LANE_ONE_V = HEAD_DIM
NEG_INF = float("-inf")
LOG2E = 1.4426950408889634
EXP2_UNDERFLOW = -153.0


def _cparams(*sem):
    return pltpu.CompilerParams(dimension_semantics=sem, vmem_limit_bytes=VMEM_LIMIT)


def _dot(a, b):
    return jnp.dot(a, b, preferred_element_type=F32)


def _dot_nt(a, b):
    return lax.dot_general(a, b, (((1,), (1,)), ((), ())), preferred_element_type=F32)


def _split2(x):
    hi = x.astype(BF16)
    lo = (x - hi.astype(F32)).astype(BF16)
    return hi, lo


def _split3(x):
    hi = x.astype(BF16)
    r = x - hi.astype(F32)
    mid = r.astype(BF16)
    lo = (r - mid.astype(F32)).astype(BF16)
    return hi, mid, lo


def _softplus2(z):
    return jnp.maximum(z, 0.0) + jnp.log2(1.0 + jnp.exp2(-jnp.abs(z)))


def _ada_kernel(c_ref, w_ref, b_ref, o_ref):
    c = c_ref[...]
    a = c / (1.0 + jnp.exp(-c))
    ah, al = _split2(a)
    wh, wl = _split2(w_ref[...])
    o_ref[...] = _dot(ah, wh) + _dot(ah, wl) + _dot(al, wh) + b_ref[...]


def _ada(c_pad, w_ada, b_ada):
    rows, d = c_pad.shape
    n = w_ada.shape[1]
    bn = 1024
    return pl.pallas_call(
        _ada_kernel,
        grid=(n // bn,),
        in_specs=[pl.BlockSpec((rows, d), lambda j: (0, 0)),
                  pl.BlockSpec((d, bn), lambda j: (0, j)),
                  pl.BlockSpec((1, bn), lambda j: (0, j))],
        out_specs=pl.BlockSpec((rows, bn), lambda j: (0, j)),
        out_shape=jax.ShapeDtypeStruct((rows, n), F32),
        compiler_params=_cparams("arbitrary"),
        name="ada",
    )(c_pad, w_ada, b_ada)


def _wc_kernel(sk_ref, wq_ref, o_ref):
    sh, sl = _split2(sk_ref[0])
    wh, wl = _split2(wq_ref[...])
    o_ref[...] = _dot_nt(sh, wh) + _dot_nt(sh, wl) + _dot_nt(sl, wh)


def _wc(sub_keys, w_query):
    d = w_query.shape[0]
    nhp = sub_keys.shape[0]
    return pl.pallas_call(
        _wc_kernel,
        grid=(nhp,),
        in_specs=[pl.BlockSpec((1, PEER_N_KEYS, LANES), lambda i: (i, 0, 0)),
                  pl.BlockSpec((d, LANES), lambda i: (0, i))],
        out_specs=pl.BlockSpec((PEER_N_KEYS, d), lambda i: (i, 0)),
        out_shape=jax.ShapeDtypeStruct((nhp * PEER_N_KEYS, d), F32),
        compiler_params=_cparams("arbitrary"),
        name="wc",
    )(sub_keys, w_query)


def _inproj_kernel(x_ref, mod_ref, g_ref, w_ref, wfh_ref, wfl_ref, bf_ref, pq_ref, pk_ref,
                   cq_ref, ck_ref, cv_ref,
                   fq_ref, fk_ref, fv_ref, sq_ref, sk_ref, sv_ref, carry_ref):
    tm = x_ref.shape[1]

    @pl.when(pl.program_id(1) == 0)
    def _():
        carry_ref[...] = jnp.zeros_like(carry_ref)

    x = x_ref[0]
    inv = lax.rsqrt(jnp.mean(x * x, axis=-1, keepdims=True) + EPS)
    sh1 = mod_ref[0, 0:1, :]
    sc1 = mod_ref[0, 1:2, :]
    h = (x * inv) * g_ref[...] * (1.0 + sc1) + sh1
    hh, hl = _split2(h)

    fl = _dot(hh, wfh_ref[...]) + _dot(hh, wfl_ref[...]) + _dot(hl, wfh_ref[...]) + bf_ref[...]
    lf = jnp.minimum(fl, 0.0) - jnp.log(1.0 + jnp.exp(-jnp.abs(fl)))
    row = lax.broadcasted_iota(jnp.int32, (tm, tm), 0)
    col = lax.broadcasted_iota(jnp.int32, (tm, tm), 1)
    tri = jnp.where(col <= row, 1.0, 0.0).astype(BF16)
    l0, l1, l2 = _split3(lf)
    fcum = _dot(tri, l0) + _dot(tri, l1) + _dot(tri, l2) + carry_ref[...]
    carry_ref[...] = fcum[tm - 1:tm, :]
    f0, f1, f2 = _split3(fcum * LOG2E)
    fcat = jnp.concatenate([f0, f1, f2], axis=1)

    fq_ref[0] = (_dot(hh, w_ref[:, 0 * HP:1 * HP]) + _dot(fcat, pq_ref[...]) + cq_ref[...]).astype(BF16)
    fk_ref[0] = (_dot(hh, w_ref[:, 1 * HP:2 * HP]) + _dot(fcat, pk_ref[...]) + ck_ref[...]).astype(BF16)
    fv_ref[0] = (_dot(hh, w_ref[:, 2 * HP:3 * HP]) + cv_ref[...]).astype(BF16)
    sq_ref[0] = _dot(hh, w_ref[:, 3 * HP + 0 * SBW:3 * HP + 1 * SBW]).astype(BF16)
    sk_ref[0] = _dot(hh, w_ref[:, 3 * HP + 1 * SBW:3 * HP + 2 * SBW]).astype(BF16)
    sv_ref[0] = _dot(hh, w_ref[:, 3 * HP + 2 * SBW:3 * HP + 3 * SBW]).astype(BF16)


def _inproj(x, mod3, g_mix, w_all, wf_hi, wf_lo, b_f, pq, pk, cq, ck, cv, tm):
    b, s, d = x.shape
    const = lambda shape: pl.BlockSpec(shape, lambda bi, si: (0,) * len(shape),
                                       pipeline_mode=pl.Buffered(1))
    out_spec = lambda w: pl.BlockSpec((1, tm, w), lambda bi, si: (bi, si, 0))
    out_sds = lambda w: jax.ShapeDtypeStruct((b, s, w), BF16)
    widths = [HP] * 3 + [SBW] * 3
    return pl.pallas_call(
        _inproj_kernel,
        grid=(b, s // tm),
        in_specs=[pl.BlockSpec((1, tm, d), lambda bi, si: (bi, si, 0)),
                  pl.BlockSpec((1, N_MOD, d), lambda bi, si: (bi, 0, 0)),
                  const((1, d)),
                  const((d, sum(widths))),
                  const((d, LANES)), const((d, LANES)), const((1, LANES)),
                  const((3 * LANES, HP)), const((3 * LANES, HP)),
                  const((1, HP)), const((1, HP)), const((1, HP))],
        out_specs=[out_spec(w) for w in widths],
        out_shape=[out_sds(w) for w in widths],
        scratch_shapes=[pltpu.VMEM((1, LANES), F32)],
        compiler_params=_cparams("arbitrary", "arbitrary"),
        name="inproj",
    )(x, mod3, g_mix, w_all, wf_hi, wf_lo, b_f, pq, pk, cq, ck, cv)


def _head_norm(o, g, lo=0):
    lane = lax.broadcasted_iota(jnp.int32, o.shape, 1)
    o = jnp.where((lane >= lo) & (lane < lo + HEAD_DIM), o, 0.0)
    ms = jnp.sum(o * o, axis=-1, keepdims=True) * (1.0 / HEAD_DIM)
    return o * lax.rsqrt(ms + EPS) * g


HEADS_PER_STEP = 2


def _head_slices(ref):
    return [slice(h * LANES, (h + 1) * LANES) for h in range(ref.shape[2] // LANES)]


def _fox_kernel(q_ref, k_ref, v_ref, g_ref, o_ref, m_ref, acc_ref, *, tq, tk):
    s_len = q_ref.shape[1]
    heads = _head_slices(q_ref)
    n_diag = tq // tk
    lane1 = lax.broadcasted_iota(jnp.int32, (1, LANES), 1)
    head_lanes = lane1 < HEAD_DIM
    row = lax.broadcasted_iota(jnp.int32, (tq, tk), 0)
    col = lax.broadcasted_iota(jnp.int32, (tq, tk), 1)

    def knorm(i, mx):
        out = []
        for hl, m in zip(heads, mx):
            kb = k_ref[0, pl.ds(pl.multiple_of(i * tk, tk), tk), hl].astype(F32)
            kb = jnp.where(head_lanes, kb, 0.0)
            n2 = jnp.sum(kb * kb, axis=-1, keepdims=True)
            out.append(jnp.maximum(m, jnp.max(n2, axis=0, keepdims=True)))
        return tuple(out)

    kmax = [jnp.sqrt(m) for m in
            lax.fori_loop(0, s_len // tk, knorm, tuple(jnp.zeros((1, 1), F32) for _ in heads))]

    def q_tile(qi, carry):
        q0 = pl.multiple_of(qi * tq, tq)
        q = [q_ref[0, pl.ds(q0, tq), hl] for hl in heads]
        reach = []
        for h, qh in enumerate(q):
            qf = qh.astype(F32)
            qn = jnp.sqrt(jnp.sum(jnp.where(head_lanes, qf * qf, 0.0), axis=-1, keepdims=True))
            f_t = jnp.sum(jnp.where((lane1 >= LANE_F_Q) & (lane1 < LANE_F_Q + 3), qf, 0.0),
                          axis=-1, keepdims=True)
            reach.append(qn * kmax[h] + f_t)
        m_ref[...] = jnp.full_like(m_ref, NEG_INF)
        acc_ref[...] = jnp.zeros_like(acc_ref)

        def step(start, mask, r0=0):
            for h, hl in enumerate(heads):
                k = k_ref[0, pl.ds(start, tk), hl]
                v = v_ref[0, pl.ds(start, tk), hl]
                s = _dot_nt(q[h][r0:], k)
                if mask is not None:
                    s = jnp.where(mask[r0:], s, NEG_INF)
                m_prev = m_ref[h, r0:, :]
                m_new = jnp.maximum(m_prev, jnp.max(s, axis=-1, keepdims=True))
                p = jnp.exp2(s - jnp.concatenate([m_new] * (tk // LANES), axis=1))
                acc_ref[h, r0:, :] = jnp.exp2(m_prev - m_new) * acc_ref[h, r0:, :] + _dot(p.astype(BF16), v)
                m_ref[h, r0:, :] = m_new

        def block_matters(kb):
            worst = None
            for h, hl in enumerate(heads):
                last = k_ref[0, pl.ds(pl.multiple_of(kb * tk + tk - 16, 16), 16), hl][15:16, :].astype(F32)
                neg_f_last = jnp.sum(jnp.where((lane1 >= LANE_ONE_Q) & (lane1 < LANE_ONE_Q + 3), last, 0.0),
                                     axis=-1, keepdims=True)
                bound = reach[h] + neg_f_last - m_ref[h, :, 0:1]
                worst = bound if worst is None else jnp.maximum(worst, bound)
            return jnp.max(worst) > EXP2_UNDERFLOW

        for d in range(n_diag):
            step(pl.multiple_of(q0 + d * tk, tk), col + d * tk <= row, d * tk)

        def cond(st):
            return st[1]

        def body(st):
            kb = st[0]
            step(pl.multiple_of(kb * tk, tk), None)
            nxt = kb - 1
            return nxt, (nxt >= 0) & block_matters(jnp.maximum(nxt, 0))

        kb0 = qi * n_diag - 1
        lax.while_loop(cond, body, (kb0, (kb0 >= 0) & block_matters(jnp.maximum(kb0, 0))))

        for h, hl in enumerate(heads):
            acc = acc_ref[h]
            lane = lax.broadcasted_iota(jnp.int32, acc.shape, 1)
            l = jnp.sum(jnp.where(lane == LANE_ONE_V, acc, 0.0), axis=-1, keepdims=True)
            o_ref[0, pl.ds(q0, tq), hl] = _head_norm(acc / l, g_ref[...]).astype(o_ref.dtype)
        return carry

    lax.fori_loop(0, s_len // tq, q_tile, 0)


def _attn_call(body, name, q, k, v, g, tq, width, nh):
    b, s, total = q.shape
    spec = pl.BlockSpec((1, s, width), lambda bi, hi: (bi, 0, hi))
    return pl.pallas_call(
        body,
        grid=(b, total // width),
        in_specs=[spec, spec, spec, pl.BlockSpec((1, LANES), lambda bi, hi: (0, 0))],
        out_specs=spec,
        out_shape=jax.ShapeDtypeStruct((b, s, total), BF16),
        scratch_shapes=[pltpu.VMEM((nh, tq, LANES), F32), pltpu.VMEM((nh, tq, LANES), F32)],
        compiler_params=_cparams("arbitrary", "arbitrary"),
        name=name,
    )(q, k, v, g)


def _fox(q, k, v, g, tq, tk):
    return _attn_call(functools.partial(_fox_kernel, tq=tq, tk=tk), "fox", q, k, v, g, tq,
                      HEADS_PER_STEP * LANES, HEADS_PER_STEP)


def _sb_kernel(q_ref, k_ref, v_ref, g_ref, o_ref, c_ref, acc_ref, *, tq, tk):
    s_len = q_ref.shape[1]
    n_diag = tq // tk
    jrow = lax.broadcasted_iota(jnp.int32, (tk, tk), 0)
    scol = lax.broadcasted_iota(jnp.int32, (tk, tk), 1)
    later = jnp.where(jrow > scol, 1.0, 0.0).astype(BF16)
    row = lax.broadcasted_iota(jnp.int32, (tq, tk), 0)
    col = lax.broadcasted_iota(jnp.int32, (tq, tk), 1)

    n_heads = LANES // HEAD_DIM
    lane_q = lax.broadcasted_iota(jnp.int32, (tq, LANES), 1)
    own = [(lane_q >= h * HEAD_DIM) & (lane_q < (h + 1) * HEAD_DIM) for h in range(n_heads)]

    def q_tile(qi, carry):
        q0 = pl.multiple_of(qi * tq, tq)
        q_all = q_ref[0, pl.ds(q0, tq), :]
        q = [jnp.where(own[h], q_all, jnp.zeros_like(q_all)) for h in range(n_heads)]
        c_ref[...] = jnp.zeros_like(c_ref)
        acc_ref[...] = jnp.zeros_like(acc_ref)

        def step(start, mask, r0=0):
            k = k_ref[0, pl.ds(start, tk), :]
            v = v_ref[0, pl.ds(start, tk), :]
            for h in range(n_heads):
                z = _dot_nt(q[h][r0:], k)
                sp = _softplus2(z)
                spm = sp if mask is None else jnp.where(mask[r0:], sp, 0.0)
                c_prev = c_ref[h, r0:, :]
                rest = _dot(spm.astype(BF16), later) + jnp.concatenate([c_prev] * (tk // LANES), axis=1)
                a = jnp.exp2(z - sp - rest)
                if mask is not None:
                    a = jnp.where(mask[r0:], a, 0.0)
                acc_ref[h, r0:, :] += _dot(a.astype(BF16), v)
                c_ref[h, r0:, :] = c_prev + jnp.sum(spm, axis=-1, keepdims=True)

        def more_matters():
            least = None
            for h in range(n_heads):
                ch = c_ref[h, :, 0:1]
                least = ch if least is None else jnp.minimum(least, ch)
            return jnp.min(least) < -EXP2_UNDERFLOW

        for d in reversed(range(n_diag)):
            step(pl.multiple_of(q0 + d * tk, tk), col + d * tk < row, d * tk)

        def cond(st):
            return st[1]

        def body(st):
            kb = st[0]
            step(pl.multiple_of(kb * tk, tk), None)
            nxt = kb - 1
            return nxt, (nxt >= 0) & more_matters()

        kb0 = qi * n_diag - 1
        lax.while_loop(cond, body, (kb0, (kb0 >= 0) & more_matters()))
        out = None
        for h in range(n_heads):
            oh = _head_norm(acc_ref[h], g_ref[...], h * HEAD_DIM)
            out = oh if out is None else out + oh
        o_ref[0, pl.ds(q0, tq), :] = out.astype(o_ref.dtype)
        return carry

    lax.fori_loop(0, s_len // tq, q_tile, 0)


def _sb(q, k, v, g, tq, tk):
    return _attn_call(functools.partial(_sb_kernel, tq=tq, tk=tk), "sb", q, k, v, g, tq, LANES,
                      LANES // HEAD_DIM)


def _outproj_kernel(x_ref, fo_ref, so_ref, mod_ref, g_ref, wo_ref, wc_ref,
                    x1_ref, h2t_ref, sc_ref):
    gt1 = mod_ref[0, 2:3, :]
    sh2 = mod_ref[0, 3:4, :]
    sc2 = mod_ref[0, 4:5, :]
    mixed = _dot(fo_ref[0], wo_ref[0:HP, :]) + _dot(so_ref[0], wo_ref[HP:HP + SBW, :])
    x1 = x_ref[0] + gt1 * mixed
    x1_ref[0] = x1
    inv = lax.rsqrt(jnp.mean(x1 * x1, axis=-1, keepdims=True) + EPS)
    h2 = (x1 * inv) * g_ref[...] * (1.0 + sc2) + sh2
    th = h2.T.astype(BF16)
    h2t_ref[...] = th
    sc_ref[...] = _dot(wc_ref[...], th)


def _outproj(x, fo, so, mod3, g_ffn, w_out_p, wc, tm):
    b, s, d = x.shape
    t = b * s
    nsc = wc.shape[0]
    const = lambda shape: pl.BlockSpec(shape, lambda bi, si: (0,) * len(shape),
                                       pipeline_mode=pl.Buffered(1))
    spb = s // tm
    return pl.pallas_call(
        _outproj_kernel,
        grid=(b, spb),
        in_specs=[pl.BlockSpec((1, tm, d), lambda bi, si: (bi, si, 0)),
                  pl.BlockSpec((1, tm, HP), lambda bi, si: (bi, si, 0)),
                  pl.BlockSpec((1, tm, SBW), lambda bi, si: (bi, si, 0)),
                  pl.BlockSpec((1, N_MOD, d), lambda bi, si: (bi, 0, 0)),
                  const((1, d)),
                  const((HP + SBW, d)),
                  const((nsc, d))],
        out_specs=[pl.BlockSpec((1, tm, d), lambda bi, si: (bi, si, 0)),
                   pl.BlockSpec((d, tm), lambda bi, si: (0, bi * spb + si)),
                   pl.BlockSpec((nsc, tm), lambda bi, si: (0, bi * spb + si))],
        out_shape=[jax.ShapeDtypeStruct((b, s, d), F32),
                   jax.ShapeDtypeStruct((d, t), BF16),
                   jax.ShapeDtypeStruct((nsc, t), F32)],
        compiler_params=_cparams("arbitrary", "arbitrary"),
        name="outproj",
    )(x, fo, so, mod3, g_ffn, w_out_p, wc)


TAKEN = -(2.0 ** 126)
TAKEN_STEP = 2.0 ** -10


def _top16(s, exact):
    n, t = s.shape
    idx = lax.broadcasted_iota(jnp.int32, (n, t), 0).astype(F32)
    krow = lax.broadcasted_iota(jnp.int32, (PEER_TOPK, t), 0)
    vals = jnp.zeros((PEER_TOPK, t), F32)
    for k in range(PEER_TOPK):
        m = jnp.max(s, axis=0, keepdims=True)
        hit = s == m
        if exact:
            hit = idx == jnp.min(jnp.where(hit, idx, float(n)), axis=0, keepdims=True)
        s = jnp.where(hit, TAKEN * (1.0 + k * TAKEN_STEP), s)
        vals = jnp.where(krow == k, m, vals)
    rank = jnp.where(s <= TAKEN, (s * (1.0 / TAKEN) - 1.0) * (1.0 / TAKEN_STEP), float(PEER_TOPK))
    return vals, rank


SUBLANES = 8
N_CAND = PEER_TOPK + (SUBLANES - 1) * SUBLANES + SUBLANES


def _cand_flat_pos():
    r = lax.broadcasted_iota(jnp.int32, (N_CAND, LANES), 0)
    mid_lo, hi_lo = PEER_TOPK, N_CAND - SUBLANES
    sub_shift = SUBLANES.bit_length() - 1
    mid = (jnp.right_shift(r - mid_lo, sub_shift) + 1) * PEER_TOPK + jnp.bitwise_and(r - mid_lo, SUBLANES - 1)
    hi = (r - hi_lo + SUBLANES) * PEER_TOPK
    return jnp.where(r < mid_lo, r, jnp.where(r < hi_lo, mid, hi)).astype(F32)


def _pair_top16(v1, v2, r1, exact):
    cand = jnp.concatenate(
        [v1[0:1, :] + v2]
        + [v1[a:a + 1, :] + v2[0:SUBLANES, :] for a in range(1, SUBLANES)]
        + [v1[SUBLANES:, :] + v2[0:1, :]], axis=0)
    pos = _cand_flat_pos() if exact else None
    c0 = cand[0:1, :]
    zsum = jnp.zeros((1, LANES), F32)
    for k in range(PEER_TOPK):
        m = jnp.max(cand, axis=0, keepdims=True)
        hit = cand == m
        if exact:
            hit = pos == jnp.min(jnp.where(hit, pos, float(PEER_TOPK * PEER_TOPK)), axis=0, keepdims=True)
        cand = jnp.where(hit, NEG_INF, cand)
        zsum = zsum + jnp.exp(m - c0)
    taken = jnp.where(cand == NEG_INF, 1.0, 0.0)
    n_taken = jnp.sum(taken, axis=0, keepdims=True)
    lrow = jnp.zeros(r1.shape, F32)
    for a in range(PEER_TOPK):
        if a == 0:
            cnt_a = jnp.sum(taken[0:PEER_TOPK, :], axis=0, keepdims=True)
        elif a < SUBLANES:
            lo = PEER_TOPK + SUBLANES * (a - 1)
            cnt_a = jnp.sum(taken[lo:lo + SUBLANES, :], axis=0, keepdims=True)
        else:
            lo = N_CAND - SUBLANES + a - SUBLANES
            cnt_a = taken[lo:lo + 1, :]
        lrow = jnp.where(r1 == float(a), cnt_a, lrow)
    return lrow, zsum, n_taken


def _topk_kernel(sc_ref, r2_ref, e2_ref, l1_ref, e1_ref):
    n = PEER_N_KEYS
    tl = sc_ref.shape[1]
    ngrp = tl // LANES

    def run(item, exact):
        hd = lax.div(item, ngrp)
        gi = lax.rem(item, ngrp)
        lanes = pl.ds(pl.multiple_of(gi * LANES, LANES), LANES)
        s1 = sc_ref[pl.ds(pl.multiple_of(hd * 2 * n, n), n), lanes]
        s2 = sc_ref[pl.ds(pl.multiple_of(hd * 2 * n + n, n), n), lanes]
        v1, r1 = _top16(s1, exact)
        v2, r2 = _top16(s2, exact)
        lrow, zsum, n_taken = _pair_top16(v1, v2, r1, exact)
        r2_ref[hd, :, lanes] = pltpu.bitcast(r2.astype(BF16), jnp.uint32)
        e2_ref[hd, :, lanes] = pltpu.bitcast(jnp.exp(s2 - v2[0:1, :]).astype(BF16), jnp.uint32)
        l1_ref[hd, :, lanes] = _dup_bf16_words(lrow)
        e1_ref[hd, :, lanes] = _dup_bf16_words(jnp.exp(s1 - v1[0:1, :]) * (0.5 / zsum))
        k = float(PEER_TOPK)
        picked = lambda r: jnp.sum(jnp.where(r < k, 1.0, 0.0), axis=0, keepdims=True)
        return jnp.abs(picked(r1) - k) + jnp.abs(picked(r2) - k) + jnp.abs(n_taken - k)

    together = 2 if (PEER_HEADS * ngrp) % 2 == 0 else 1

    def body(it, carry):
        items = [it * together + u for u in range(together)]
        excess = [run(item, exact=False) for item in items]
        for item, ex in zip(items, excess):
            @pl.when(jnp.max(ex) > 0.0)
            def _():
                run(item, exact=True)

        return carry

    lax.fori_loop(0, PEER_HEADS * ngrp // together, body, 0)


def _topk(scores_t, tl):
    nsc, t = scores_t.shape
    spec = lambda rows: pl.BlockSpec((PEER_HEADS, rows, tl), lambda i: (0, 0, i))
    sds = lambda rows: jax.ShapeDtypeStruct((PEER_HEADS, rows, t), jnp.uint32)
    half, full = PEER_N_KEYS // 2, PEER_N_KEYS
    return pl.pallas_call(
        _topk_kernel,
        grid=(t // tl,),
        in_specs=[pl.BlockSpec((nsc, tl), lambda i: (0, i))],
        out_specs=[spec(half), spec(half), spec(full), spec(full)],
        out_shape=[sds(half), sds(half), sds(full), sds(full)],
        compiler_params=_cparams("arbitrary"),
        name="topk",
    )(scores_t)


def _gelu_x2(x):
    return x * (1.0 + lax.erf(x * (2.0 ** -0.5)))


def _dup_bf16_words(x):
    bits = pltpu.bitcast(x.astype(BF16).astype(F32), jnp.uint32)
    return bits | (bits >> 16)


def _bf16_rows(words, n):
    return pltpu.bitcast(jnp.broadcast_to(words, (n // 2, words.shape[1])), BF16)


def _peer_kernel(h2t_ref, dn_ref, up_ref, r2_ref, e2_ref, l1_ref, e1_ref, x1_ref, mod_ref, gf_ref,
                 o_ref, acc_ref, st_ref, pw_ref, *, rows_per_chunk):
    j = pl.program_id(2)
    n = PEER_N_KEYS

    @pl.when(j == 0)
    def _():
        acc_ref[...] = jnp.zeros_like(acc_ref)

    st_ref[...] = _gelu_x2(_dot(dn_ref[...], h2t_ref[...]).astype(BF16))
    zero = jnp.zeros((n, LANES), BF16)
    pair = 2
    for g in range(h2t_ref.shape[1] // LANES):
        lanes = slice(g * LANES, (g + 1) * LANES)
        for i1 in range(0, rows_per_chunk, pair):
            w = [None] * pair
            for hd in range(PEER_HEADS):
                r2 = pltpu.bitcast(r2_ref[hd, :, lanes], BF16)
                e2 = pltpu.bitcast(e2_ref[hd, :, lanes], BF16)
                for u in range(pair):
                    cnt = _bf16_rows(l1_ref[hd, i1 + u:i1 + u + 1, lanes], n)
                    e1 = _bf16_rows(e1_ref[hd, i1 + u:i1 + u + 1, lanes], n)
                    wh = jnp.where(r2 < cnt, e2, zero) * e1
                    w[u] = wh if w[u] is None else w[u] + wh
            for u in range(pair):
                rows = slice((i1 + u) * n, (i1 + u + 1) * n)
                pw_ref[rows, lanes] = st_ref[rows, lanes] * w[u]
    acc_ref[...] += _dot(up_ref[...], pw_ref[...])

    @pl.when(j == pl.num_programs(2) - 1)
    def _():
        gt2 = mod_ref[0, 5:6, :]
        x2 = x1_ref[0] + gt2 * acc_ref[...].T
        inv = lax.rsqrt(jnp.mean(x2 * x2, axis=-1, keepdims=True) + EPS)
        o_ref[0] = (x2 * inv) * gf_ref[...]


def _peer(h2t, down, up_t, r2, e2, l1, e1, x1, mod3, g_final, tt, ec):
    b, s, d = x1.shape
    ne = down.shape[0]
    rpc = ec // PEER_N_KEYS
    spb = s // tt
    tok = lambda bi, si, j: (0, 0, bi * spb + si)
    gate_tab = pl.BlockSpec((PEER_HEADS, rpc, tt), lambda bi, si, j: (0, j, bi * spb + si))
    return pl.pallas_call(
        functools.partial(_peer_kernel, rows_per_chunk=rpc),
        grid=(b, spb, ne // ec),
        in_specs=[pl.BlockSpec((d, tt), lambda bi, si, j: (0, bi * spb + si)),
                  pl.BlockSpec((ec, d), lambda bi, si, j: (j, 0)),
                  pl.BlockSpec((d, ec), lambda bi, si, j: (0, j)),
                  pl.BlockSpec((PEER_HEADS, PEER_N_KEYS // 2, tt), tok),
                  pl.BlockSpec((PEER_HEADS, PEER_N_KEYS // 2, tt), tok),
                  gate_tab, gate_tab,
                  pl.BlockSpec((1, tt, d), lambda bi, si, j: (bi, si, 0)),
                  pl.BlockSpec((1, N_MOD, d), lambda bi, si, j: (bi, 0, 0)),
                  pl.BlockSpec((1, d), lambda bi, si, j: (0, 0))],
        out_specs=pl.BlockSpec((1, tt, d), lambda bi, si, j: (bi, si, 0)),
        out_shape=jax.ShapeDtypeStruct((b, s, d), F32),
        scratch_shapes=[pltpu.VMEM((d, tt), F32), pltpu.VMEM((ec, tt), BF16), pltpu.VMEM((ec, tt), BF16)],
        compiler_params=_cparams("arbitrary", "arbitrary", "arbitrary"),
        name="peer",
    )(h2t, down, up_t, r2, e2, l1, e1, x1, mod3, g_final)


def _pad_heads(w):
    d = w.shape[0]
    w = w.reshape(d, N_HEADS, HEAD_DIM)
    return jnp.pad(w, ((0, 0), (0, 0), (0, LANES - HEAD_DIM))).reshape(d, HP)


def _bias_constants():
    pq = np.zeros((3 * LANES, HP), np.float32)
    pk = np.zeros((3 * LANES, HP), np.float32)
    cq = np.zeros((1, HP), np.float32)
    ck = np.zeros((1, HP), np.float32)
    cv = np.zeros((1, HP), np.float32)
    for h in range(N_HEADS):
        for part in range(3):
            pq[part * LANES + h, h * LANES + LANE_F_Q + part] = 1.0
            pk[part * LANES + h, h * LANES + LANE_ONE_Q + part] = -1.0
            cq[0, h * LANES + LANE_ONE_Q + part] = 1.0
            ck[0, h * LANES + LANE_F_Q + part] = 1.0
        cv[0, h * LANES + LANE_ONE_V] = 1.0
    return (jnp.asarray(pq, BF16), jnp.asarray(pk, BF16), jnp.asarray(cq), jnp.asarray(ck), jnp.asarray(cv))


PROJ_TOKENS = 512
ATTN_Q_TOKENS = 1024
FOX_K_TOKENS = 512
SB_K_TOKENS = 256
TOPK_TOKENS = 512
PEER_TOKENS = 1024
PEER_EXPERTS = 1024


def _tile(n, pref):
    return pref if n % pref == 0 else n


def kernel(x, c, w_ada, b_ada, g_norm_mix, w_in, b_forget, g_out_fox, g_out_sb, w_out, g_norm_ffn,
           w_query, sub_keys, expert_down, expert_up, g_final):
    b, s, d = x.shape
    assert w_ada.shape[0] == 1, "single-layer block: the final RMSNorm is fused into the PEER kernel"
    fw = N_HEADS * HEAD_DIM
    scale = HEAD_DIM ** -0.5 * LOG2E
    pq, pk, cq, ck, cv = _bias_constants()
    tm = _tile(s, PROJ_TOKENS)
    tq = _tile(s, ATTN_Q_TOKENS)

    mod = _ada(jnp.pad(c, ((0, -b % 8), (0, 0))), w_ada[0], b_ada[0][None, :])[:b]
    mod3 = mod.reshape(b, N_MOD, d)

    wi = w_in[0]
    o1 = 3 * fw
    o2 = o1 + N_HEADS
    w_all = jnp.concatenate([
        _pad_heads(wi[:, 0:fw] * scale), _pad_heads(wi[:, fw:2 * fw]), _pad_heads(wi[:, 2 * fw:o1]),
        wi[:, o2:o2 + fw] * scale, wi[:, o2 + fw:o2 + 3 * fw]], axis=1).astype(BF16)
    wf = jnp.pad(wi[:, o1:o2], ((0, 0), (0, LANES - N_HEADS)))
    wf_hi = wf.astype(BF16)
    wf_lo = (wf - wf_hi.astype(F32)).astype(BF16)
    b_f = jnp.pad(b_forget[0], (0, LANES - N_HEADS))[None, :]
    fq, fk, fv, sq, sk, sv = _inproj(x, mod3, g_norm_mix[0][None, :], w_all, wf_hi, wf_lo, b_f,
                                     pq, pk, cq, ck, cv, tm)

    g_fox = jnp.pad(g_out_fox[0], (0, LANES - HEAD_DIM))[None, :]
    g_sb = jnp.tile(g_out_sb[0], LANES // HEAD_DIM)[None, :]
    fo = _fox(fq, fk, fv, g_fox, tq, _tile(s, FOX_K_TOKENS))
    so = _sb(sq, sk, sv, g_sb, tq, _tile(s, SB_K_TOKENS))

    wo = w_out[0]
    w_out_fox = jnp.pad(wo[:fw].reshape(N_HEADS, HEAD_DIM, d),
                        ((0, 0), (0, LANES - HEAD_DIM), (0, 0))).reshape(HP, d)
    w_out_p = jnp.concatenate([w_out_fox, wo[fw:]], axis=0).astype(BF16)
    wc = _wc(sub_keys[0].reshape(PEER_HEADS * 2, PEER_N_KEYS, LANES), w_query[0])
    x1, h2t, scores_t = _outproj(x, fo, so, mod3, g_norm_ffn[0][None, :], w_out_p, wc.astype(BF16), tm)

    r2, e2, l1, e1 = _topk(scores_t, _tile(b * s, TOPK_TOKENS))
    return _peer(h2t, expert_down[0].astype(BF16), expert_up[0].astype(BF16).T, r2, e2, l1, e1,
                 x1, mod3, g_final[None, :], _tile(s, PEER_TOKENS), PEER_EXPERTS)
```

```python
import functools

import numpy as np
import jax
import jax.numpy as jnp
from jax import lax
from jax.experimental import pallas as pl
from jax.experimental.pallas import tpu as pltpu

F32 = jnp.float32
BF16 = jnp.bfloat16

HEAD_DIM = 64
N_HEADS = 8
PEER_HEADS = 8
PEER_N_KEYS = 128
PEER_TOPK = 16
N_MOD = 6
EPS = 1e-6

LANES = 128
VMEM_LIMIT = 56 * 1024 * 1024

HP = N_HEADS * LANES
SBW = N_HEADS * HEAD_DIM
LANE_ONE_Q = HEAD_DIM
LANE_F_Q = HEAD_DIM + 3
LANE_ONE_V = HEAD_DIM
NEG_INF = float("-inf")
LOG2E = 1.4426950408889634
EXP2_UNDERFLOW = -153.0


def _cparams(*sem):
    return pltpu.CompilerParams(dimension_semantics=sem, vmem_limit_bytes=VMEM_LIMIT)


def _dot(a, b):
    return jnp.dot(a, b, preferred_element_type=F32)


def _dot_nt(a, b):
    return lax.dot_general(a, b, (((1,), (1,)), ((), ())), preferred_element_type=F32)


def _split2(x):
    hi = x.astype(BF16)
    lo = (x - hi.astype(F32)).astype(BF16)
    return hi, lo


def _split3(x):
    hi = x.astype(BF16)
    r = x - hi.astype(F32)
    mid = r.astype(BF16)
    lo = (r - mid.astype(F32)).astype(BF16)
    return hi, mid, lo


def _softplus2(z):
    return jnp.maximum(z, 0.0) + jnp.log2(1.0 + jnp.exp2(-jnp.abs(z)))


def _ada_kernel(c_ref, w_ref, b_ref, o_ref):
    c = c_ref[...]
    a = c / (1.0 + jnp.exp(-c))
    ah, al = _split2(a)
    wh, wl = _split2(w_ref[...])
    o_ref[...] = _dot(ah, wh) + _dot(ah, wl) + _dot(al, wh) + b_ref[...]


def _ada(c_pad, w_ada, b_ada):
    rows, d = c_pad.shape
    n = w_ada.shape[1]
    bn = 1024
    return pl.pallas_call(
        _ada_kernel,
        grid=(n // bn,),
        in_specs=[pl.BlockSpec((rows, d), lambda j: (0, 0)),
                  pl.BlockSpec((d, bn), lambda j: (0, j)),
                  pl.BlockSpec((1, bn), lambda j: (0, j))],
        out_specs=pl.BlockSpec((rows, bn), lambda j: (0, j)),
        out_shape=jax.ShapeDtypeStruct((rows, n), F32),
        compiler_params=_cparams("arbitrary"),
        name="ada",
    )(c_pad, w_ada, b_ada)


def _wc_kernel(sk_ref, wq_ref, o_ref):
    sh, sl = _split2(sk_ref[0])
    wh, wl = _split2(wq_ref[...])
    o_ref[...] = _dot_nt(sh, wh) + _dot_nt(sh, wl) + _dot_nt(sl, wh)


def _wc(sub_keys, w_query):
    d = w_query.shape[0]
    nhp = sub_keys.shape[0]
    return pl.pallas_call(
        _wc_kernel,
        grid=(nhp,),
        in_specs=[pl.BlockSpec((1, PEER_N_KEYS, LANES), lambda i: (i, 0, 0)),
                  pl.BlockSpec((d, LANES), lambda i: (0, i))],
        out_specs=pl.BlockSpec((PEER_N_KEYS, d), lambda i: (i, 0)),
        out_shape=jax.ShapeDtypeStruct((nhp * PEER_N_KEYS, d), F32),
        compiler_params=_cparams("arbitrary"),
        name="wc",
    )(sub_keys, w_query)


def _inproj_kernel(x_ref, mod_ref, g_ref, w_ref, wfh_ref, wfl_ref, bf_ref, pq_ref, pk_ref,
                   cq_ref, ck_ref, cv_ref,
                   fq_ref, fk_ref, fv_ref, sq_ref, sk_ref, sv_ref, carry_ref):
    tm = x_ref.shape[1]

    @pl.when(pl.program_id(1) == 0)
    def _():
        carry_ref[...] = jnp.zeros_like(carry_ref)

    x = x_ref[0]
    inv = lax.rsqrt(jnp.mean(x * x, axis=-1, keepdims=True) + EPS)
    sh1 = mod_ref[0, 0:1, :]
    sc1 = mod_ref[0, 1:2, :]
    h = (x * inv) * g_ref[...] * (1.0 + sc1) + sh1
    hh, hl = _split2(h)

    fl = _dot(hh, wfh_ref[...]) + _dot(hh, wfl_ref[...]) + _dot(hl, wfh_ref[...]) + bf_ref[...]
    lf = jnp.minimum(fl, 0.0) - jnp.log(1.0 + jnp.exp(-jnp.abs(fl)))
    row = lax.broadcasted_iota(jnp.int32, (tm, tm), 0)
    col = lax.broadcasted_iota(jnp.int32, (tm, tm), 1)
    tri = jnp.where(col <= row, 1.0, 0.0).astype(BF16)
    l0, l1, l2 = _split3(lf)
    fcum = _dot(tri, l0) + _dot(tri, l1) + _dot(tri, l2) + carry_ref[...]
    carry_ref[...] = fcum[tm - 1:tm, :]
    f0, f1, f2 = _split3(fcum * LOG2E)
    fcat = jnp.concatenate([f0, f1, f2], axis=1)

    fq_ref[0] = (_dot(hh, w_ref[:, 0 * HP:1 * HP]) + _dot(fcat, pq_ref[...]) + cq_ref[...]).astype(BF16)
    fk_ref[0] = (_dot(hh, w_ref[:, 1 * HP:2 * HP]) + _dot(fcat, pk_ref[...]) + ck_ref[...]).astype(BF16)
    fv_ref[0] = (_dot(hh, w_ref[:, 2 * HP:3 * HP]) + cv_ref[...]).astype(BF16)
    sq_ref[0] = _dot(hh, w_ref[:, 3 * HP + 0 * SBW:3 * HP + 1 * SBW]).astype(BF16)
    sk_ref[0] = _dot(hh, w_ref[:, 3 * HP + 1 * SBW:3 * HP + 2 * SBW]).astype(BF16)
    sv_ref[0] = _dot(hh, w_ref[:, 3 * HP + 2 * SBW:3 * HP + 3 * SBW]).astype(BF16)


def _inproj(x, mod3, g_mix, w_all, wf_hi, wf_lo, b_f, pq, pk, cq, ck, cv, tm):
    b, s, d = x.shape
    const = lambda shape: pl.BlockSpec(shape, lambda bi, si: (0,) * len(shape),
                                       pipeline_mode=pl.Buffered(1))
    out_spec = lambda w: pl.BlockSpec((1, tm, w), lambda bi, si: (bi, si, 0))
    out_sds = lambda w: jax.ShapeDtypeStruct((b, s, w), BF16)
    widths = [HP] * 3 + [SBW] * 3
    return pl.pallas_call(
        _inproj_kernel,
        grid=(b, s // tm),
        in_specs=[pl.BlockSpec((1, tm, d), lambda bi, si: (bi, si, 0)),
                  pl.BlockSpec((1, N_MOD, d), lambda bi, si: (bi, 0, 0)),
                  const((1, d)),
                  const((d, sum(widths))),
                  const((d, LANES)), const((d, LANES)), const((1, LANES)),
                  const((3 * LANES, HP)), const((3 * LANES, HP)),
                  const((1, HP)), const((1, HP)), const((1, HP))],
        out_specs=[out_spec(w) for w in widths],
        out_shape=[out_sds(w) for w in widths],
        scratch_shapes=[pltpu.VMEM((1, LANES), F32)],
        compiler_params=_cparams("arbitrary", "arbitrary"),
        name="inproj",
    )(x, mod3, g_mix, w_all, wf_hi, wf_lo, b_f, pq, pk, cq, ck, cv)


def _head_norm(o, g, lo=0):
    lane = lax.broadcasted_iota(jnp.int32, o.shape, 1)
    o = jnp.where((lane >= lo) & (lane < lo + HEAD_DIM), o, 0.0)
    ms = jnp.sum(o * o, axis=-1, keepdims=True) * (1.0 / HEAD_DIM)
    return o * lax.rsqrt(ms + EPS) * g


HEADS_PER_STEP = 2


def _head_slices(ref):
    return [slice(h * LANES, (h + 1) * LANES) for h in range(ref.shape[2] // LANES)]


def _fox_kernel(q_ref, k_ref, v_ref, g_ref, o_ref, m_ref, acc_ref, *, tq, tk):
    s_len = q_ref.shape[1]
    heads = _head_slices(q_ref)
    n_diag = tq // tk
    lane1 = lax.broadcasted_iota(jnp.int32, (1, LANES), 1)
    head_lanes = lane1 < HEAD_DIM
    row = lax.broadcasted_iota(jnp.int32, (tq, tk), 0)
    col = lax.broadcasted_iota(jnp.int32, (tq, tk), 1)

    def knorm(i, mx):
        out = []
        for hl, m in zip(heads, mx):
            kb = k_ref[0, pl.ds(pl.multiple_of(i * tk, tk), tk), hl].astype(F32)
            kb = jnp.where(head_lanes, kb, 0.0)
            n2 = jnp.sum(kb * kb, axis=-1, keepdims=True)
            out.append(jnp.maximum(m, jnp.max(n2, axis=0, keepdims=True)))
        return tuple(out)

    kmax = [jnp.sqrt(m) for m in
            lax.fori_loop(0, s_len // tk, knorm, tuple(jnp.zeros((1, 1), F32) for _ in heads))]

    def q_tile(qi, carry):
        q0 = pl.multiple_of(qi * tq, tq)
        q = [q_ref[0, pl.ds(q0, tq), hl] for hl in heads]
        reach = []
        for h, qh in enumerate(q):
            qf = qh.astype(F32)
            qn = jnp.sqrt(jnp.sum(jnp.where(head_lanes, qf * qf, 0.0), axis=-1, keepdims=True))
            f_t = jnp.sum(jnp.where((lane1 >= LANE_F_Q) & (lane1 < LANE_F_Q + 3), qf, 0.0),
                          axis=-1, keepdims=True)
            reach.append(qn * kmax[h] + f_t)
        m_ref[...] = jnp.full_like(m_ref, NEG_INF)
        acc_ref[...] = jnp.zeros_like(acc_ref)

        def step(start, mask, r0=0):
            for h, hl in enumerate(heads):
                k = k_ref[0, pl.ds(start, tk), hl]
                v = v_ref[0, pl.ds(start, tk), hl]
                s = _dot_nt(q[h][r0:], k)
                if mask is not None:
                    s = jnp.where(mask[r0:], s, NEG_INF)
                m_prev = m_ref[h, r0:, :]
                m_new = jnp.maximum(m_prev, jnp.max(s, axis=-1, keepdims=True))
                p = jnp.exp2(s - jnp.concatenate([m_new] * (tk // LANES), axis=1))
                acc_ref[h, r0:, :] = jnp.exp2(m_prev - m_new) * acc_ref[h, r0:, :] + _dot(p.astype(BF16), v)
                m_ref[h, r0:, :] = m_new

        def block_matters(kb):
            worst = None
            for h, hl in enumerate(heads):
                last = k_ref[0, pl.ds(pl.multiple_of(kb * tk + tk - 16, 16), 16), hl][15:16, :].astype(F32)
                neg_f_last = jnp.sum(jnp.where((lane1 >= LANE_ONE_Q) & (lane1 < LANE_ONE_Q + 3), last, 0.0),
                                     axis=-1, keepdims=True)
                bound = reach[h] + neg_f_last - m_ref[h, :, 0:1]
                worst = bound if worst is None else jnp.maximum(worst, bound)
            return jnp.max(worst) > EXP2_UNDERFLOW

        for d in range(n_diag):
            step(pl.multiple_of(q0 + d * tk, tk), col + d * tk <= row, d * tk)

        def cond(st):
            return st[1]

        def body(st):
            kb = st[0]
            step(pl.multiple_of(kb * tk, tk), None)
            nxt = kb - 1
            return nxt, (nxt >= 0) & block_matters(jnp.maximum(nxt, 0))

        kb0 = qi * n_diag - 1
        lax.while_loop(cond, body, (kb0, (kb0 >= 0) & block_matters(jnp.maximum(kb0, 0))))

        for h, hl in enumerate(heads):
            acc = acc_ref[h]
            lane = lax.broadcasted_iota(jnp.int32, acc.shape, 1)
            l = jnp.sum(jnp.where(lane == LANE_ONE_V, acc, 0.0), axis=-1, keepdims=True)
            o_ref[0, pl.ds(q0, tq), hl] = _head_norm(acc / l, g_ref[...]).astype(o_ref.dtype)
        return carry

    lax.fori_loop(0, s_len // tq, q_tile, 0)


def _attn_call(body, name, q, k, v, g, tq, width, nh):
    b, s, total = q.shape
    spec = pl.BlockSpec((1, s, width), lambda bi, hi: (bi, 0, hi))
    return pl.pallas_call(
        body,
        grid=(b, total // width),
        in_specs=[spec, spec, spec, pl.BlockSpec((1, LANES), lambda bi, hi: (0, 0))],
        out_specs=spec,
        out_shape=jax.ShapeDtypeStruct((b, s, total), BF16),
        scratch_shapes=[pltpu.VMEM((nh, tq, LANES), F32), pltpu.VMEM((nh, tq, LANES), F32)],
        compiler_params=_cparams("arbitrary", "arbitrary"),
        name=name,
    )(q, k, v, g)


def _fox(q, k, v, g, tq, tk):
    return _attn_call(functools.partial(_fox_kernel, tq=tq, tk=tk), "fox", q, k, v, g, tq,
                      HEADS_PER_STEP * LANES, HEADS_PER_STEP)


def _sb_kernel(q_ref, k_ref, v_ref, g_ref, o_ref, c_ref, acc_ref, *, tq, tk):
    s_len = q_ref.shape[1]
    n_diag = tq // tk
    jrow = lax.broadcasted_iota(jnp.int32, (tk, tk), 0)
    scol = lax.broadcasted_iota(jnp.int32, (tk, tk), 1)
    later = jnp.where(jrow > scol, 1.0, 0.0).astype(BF16)
    row = lax.broadcasted_iota(jnp.int32, (tq, tk), 0)
    col = lax.broadcasted_iota(jnp.int32, (tq, tk), 1)

    n_heads = LANES // HEAD_DIM
    lane_q = lax.broadcasted_iota(jnp.int32, (tq, LANES), 1)
    own = [(lane_q >= h * HEAD_DIM) & (lane_q < (h + 1) * HEAD_DIM) for h in range(n_heads)]

    def q_tile(qi, carry):
        q0 = pl.multiple_of(qi * tq, tq)
        q_all = q_ref[0, pl.ds(q0, tq), :]
        q = [jnp.where(own[h], q_all, jnp.zeros_like(q_all)) for h in range(n_heads)]
        c_ref[...] = jnp.zeros_like(c_ref)
        acc_ref[...] = jnp.zeros_like(acc_ref)

        def step(start, mask, r0=0):
            k = k_ref[0, pl.ds(start, tk), :]
            v = v_ref[0, pl.ds(start, tk), :]
            for h in range(n_heads):
                z = _dot_nt(q[h][r0:], k)
                sp = _softplus2(z)
                spm = sp if mask is None else jnp.where(mask[r0:], sp, 0.0)
                c_prev = c_ref[h, r0:, :]
                rest = _dot(spm.astype(BF16), later) + jnp.concatenate([c_prev] * (tk // LANES), axis=1)
                a = jnp.exp2(z - sp - rest)
                if mask is not None:
                    a = jnp.where(mask[r0:], a, 0.0)
                acc_ref[h, r0:, :] += _dot(a.astype(BF16), v)
                c_ref[h, r0:, :] = c_prev + jnp.sum(spm, axis=-1, keepdims=True)

        def more_matters():
            least = None
            for h in range(n_heads):
                ch = c_ref[h, :, 0:1]
                least = ch if least is None else jnp.minimum(least, ch)
            return jnp.min(least) < -EXP2_UNDERFLOW

        for d in reversed(range(n_diag)):
            step(pl.multiple_of(q0 + d * tk, tk), col + d * tk < row, d * tk)

        def cond(st):
            return st[1]

        def body(st):
            kb = st[0]
            step(pl.multiple_of(kb * tk, tk), None)
            nxt = kb - 1
            return nxt, (nxt >= 0) & more_matters()

        kb0 = qi * n_diag - 1
        lax.while_loop(cond, body, (kb0, (kb0 >= 0) & more_matters()))
        out = None
        for h in range(n_heads):
            oh = _head_norm(acc_ref[h], g_ref[...], h * HEAD_DIM)
            out = oh if out is None else out + oh
        o_ref[0, pl.ds(q0, tq), :] = out.astype(o_ref.dtype)
        return carry

    lax.fori_loop(0, s_len // tq, q_tile, 0)


def _sb(q, k, v, g, tq, tk):
    return _attn_call(functools.partial(_sb_kernel, tq=tq, tk=tk), "sb", q, k, v, g, tq, LANES,
                      LANES // HEAD_DIM)


def _outproj_kernel(x_ref, fo_ref, so_ref, mod_ref, g_ref, wo_ref, wc_ref,
                    x1_ref, h2t_ref, sc_ref):
    gt1 = mod_ref[0, 2:3, :]
    sh2 = mod_ref[0, 3:4, :]
    sc2 = mod_ref[0, 4:5, :]
    mixed = _dot(fo_ref[0], wo_ref[0:HP, :]) + _dot(so_ref[0], wo_ref[HP:HP + SBW, :])
    x1 = x_ref[0] + gt1 * mixed
    x1_ref[0] = x1
    inv = lax.rsqrt(jnp.mean(x1 * x1, axis=-1, keepdims=True) + EPS)
    h2 = (x1 * inv) * g_ref[...] * (1.0 + sc2) + sh2
    th = h2.T.astype(BF16)
    h2t_ref[...] = th
    sc_ref[...] = _dot(wc_ref[...], th)


def _outproj(x, fo, so, mod3, g_ffn, w_out_p, wc, tm):
    b, s, d = x.shape
    t = b * s
    nsc = wc.shape[0]
    const = lambda shape: pl.BlockSpec(shape, lambda bi, si: (0,) * len(shape),
                                       pipeline_mode=pl.Buffered(1))
    spb = s // tm
    return pl.pallas_call(
        _outproj_kernel,
        grid=(b, spb),
        in_specs=[pl.BlockSpec((1, tm, d), lambda bi, si: (bi, si, 0)),
                  pl.BlockSpec((1, tm, HP), lambda bi, si: (bi, si, 0)),
                  pl.BlockSpec((1, tm, SBW), lambda bi, si: (bi, si, 0)),
                  pl.BlockSpec((1, N_MOD, d), lambda bi, si: (bi, 0, 0)),
                  const((1, d)),
                  const((HP + SBW, d)),
                  const((nsc, d))],
        out_specs=[pl.BlockSpec((1, tm, d), lambda bi, si: (bi, si, 0)),
                   pl.BlockSpec((d, tm), lambda bi, si: (0, bi * spb + si)),
                   pl.BlockSpec((nsc, tm), lambda bi, si: (0, bi * spb + si))],
        out_shape=[jax.ShapeDtypeStruct((b, s, d), F32),
                   jax.ShapeDtypeStruct((d, t), BF16),
                   jax.ShapeDtypeStruct((nsc, t), F32)],
        compiler_params=_cparams("arbitrary", "arbitrary"),
        name="outproj",
    )(x, fo, so, mod3, g_ffn, w_out_p, wc)


TAKEN = -(2.0 ** 126)
TAKEN_STEP = 2.0 ** -10


def _top16(s, exact):
    n, t = s.shape
    idx = lax.broadcasted_iota(jnp.int32, (n, t), 0).astype(F32)
    krow = lax.broadcasted_iota(jnp.int32, (PEER_TOPK, t), 0)
    vals = jnp.zeros((PEER_TOPK, t), F32)
    for k in range(PEER_TOPK):
        m = jnp.max(s, axis=0, keepdims=True)
        hit = s == m
        if exact:
            hit = idx == jnp.min(jnp.where(hit, idx, float(n)), axis=0, keepdims=True)
        s = jnp.where(hit, TAKEN * (1.0 + k * TAKEN_STEP), s)
        vals = jnp.where(krow == k, m, vals)
    rank = jnp.where(s <= TAKEN, (s * (1.0 / TAKEN) - 1.0) * (1.0 / TAKEN_STEP), float(PEER_TOPK))
    return vals, rank


SUBLANES = 8
N_CAND = PEER_TOPK + (SUBLANES - 1) * SUBLANES + SUBLANES


def _cand_flat_pos():
    r = lax.broadcasted_iota(jnp.int32, (N_CAND, LANES), 0)
    mid_lo, hi_lo = PEER_TOPK, N_CAND - SUBLANES
    sub_shift = SUBLANES.bit_length() - 1
    mid = (jnp.right_shift(r - mid_lo, sub_shift) + 1) * PEER_TOPK + jnp.bitwise_and(r - mid_lo, SUBLANES - 1)
    hi = (r - hi_lo + SUBLANES) * PEER_TOPK
    return jnp.where(r < mid_lo, r, jnp.where(r < hi_lo, mid, hi)).astype(F32)


def _pair_top16(v1, v2, r1, exact):
    cand = jnp.concatenate(
        [v1[0:1, :] + v2]
        + [v1[a:a + 1, :] + v2[0:SUBLANES, :] for a in range(1, SUBLANES)]
        + [v1[SUBLANES:, :] + v2[0:1, :]], axis=0)
    pos = _cand_flat_pos() if exact else None
    c0 = cand[0:1, :]
    zsum = jnp.zeros((1, LANES), F32)
    for k in range(PEER_TOPK):
        m = jnp.max(cand, axis=0, keepdims=True)
        hit = cand == m
        if exact:
            hit = pos == jnp.min(jnp.where(hit, pos, float(PEER_TOPK * PEER_TOPK)), axis=0, keepdims=True)
        cand = jnp.where(hit, NEG_INF, cand)
        zsum = zsum + jnp.exp(m - c0)
    taken = jnp.where(cand == NEG_INF, 1.0, 0.0)
    n_taken = jnp.sum(taken, axis=0, keepdims=True)
    lrow = jnp.zeros(r1.shape, F32)
    for a in range(PEER_TOPK):
        if a == 0:
            cnt_a = jnp.sum(taken[0:PEER_TOPK, :], axis=0, keepdims=True)
        elif a < SUBLANES:
            lo = PEER_TOPK + SUBLANES * (a - 1)
            cnt_a = jnp.sum(taken[lo:lo + SUBLANES, :], axis=0, keepdims=True)
        else:
            lo = N_CAND - SUBLANES + a - SUBLANES
            cnt_a = taken[lo:lo + 1, :]
        lrow = jnp.where(r1 == float(a), cnt_a, lrow)
    return lrow, zsum, n_taken


def _topk_kernel(sc_ref, r2_ref, e2_ref, l1_ref, e1_ref):
    n = PEER_N_KEYS
    tl = sc_ref.shape[1]
    ngrp = tl // LANES

    def run(item, exact):
        hd = lax.div(item, ngrp)
        gi = lax.rem(item, ngrp)
        lanes = pl.ds(pl.multiple_of(gi * LANES, LANES), LANES)
        s1 = sc_ref[pl.ds(pl.multiple_of(hd * 2 * n, n), n), lanes]
        s2 = sc_ref[pl.ds(pl.multiple_of(hd * 2 * n + n, n), n), lanes]
        v1, r1 = _top16(s1, exact)
        v2, r2 = _top16(s2, exact)
        lrow, zsum, n_taken = _pair_top16(v1, v2, r1, exact)
        r2_ref[hd, :, lanes] = pltpu.bitcast(r2.astype(BF16), jnp.uint32)
        e2_ref[hd, :, lanes] = pltpu.bitcast(jnp.exp(s2 - v2[0:1, :]).astype(BF16), jnp.uint32)
        l1_ref[hd, :, lanes] = _dup_bf16_words(lrow)
        e1_ref[hd, :, lanes] = _dup_bf16_words(jnp.exp(s1 - v1[0:1, :]) * (0.5 / zsum))
        k = float(PEER_TOPK)
        picked = lambda r: jnp.sum(jnp.where(r < k, 1.0, 0.0), axis=0, keepdims=True)
        return jnp.abs(picked(r1) - k) + jnp.abs(picked(r2) - k) + jnp.abs(n_taken - k)

    together = 4 if (PEER_HEADS * ngrp) % 4 == 0 else 1

    def body(it, carry):
        items = [it * together + u for u in range(together)]
        excess = [run(item, exact=False) for item in items]
        for item, ex in zip(items, excess):
            @pl.when(jnp.max(ex) > 0.0)
            def _():
                run(item, exact=True)

        return carry

    lax.fori_loop(0, PEER_HEADS * ngrp // together, body, 0)


def _topk(scores_t, tl):
    nsc, t = scores_t.shape
    spec = lambda rows: pl.BlockSpec((PEER_HEADS, rows, tl), lambda i: (0, 0, i))
    sds = lambda rows: jax.ShapeDtypeStruct((PEER_HEADS, rows, t), jnp.uint32)
    half, full = PEER_N_KEYS // 2, PEER_N_KEYS
    return pl.pallas_call(
        _topk_kernel,
        grid=(t // tl,),
        in_specs=[pl.BlockSpec((nsc, tl), lambda i: (0, i))],
        out_specs=[spec(half), spec(half), spec(full), spec(full)],
        out_shape=[sds(half), sds(half), sds(full), sds(full)],
        compiler_params=_cparams("arbitrary"),
        name="topk",
    )(scores_t)


def _gelu_x2(x):
    return x * (1.0 + lax.erf(x * (2.0 ** -0.5)))


def _dup_bf16_words(x):
    bits = pltpu.bitcast(x.astype(BF16).astype(F32), jnp.uint32)
    return bits | (bits >> 16)


def _bf16_rows(words, n):
    return pltpu.bitcast(jnp.broadcast_to(words, (n // 2, words.shape[1])), BF16)


def _peer_kernel(h2t_ref, dn_ref, up_ref, r2_ref, e2_ref, l1_ref, e1_ref, x1_ref, mod_ref, gf_ref,
                 o_ref, acc_ref, st_ref, pw_ref, *, rows_per_chunk):
    j = pl.program_id(2)
    n = PEER_N_KEYS

    @pl.when(j == 0)
    def _():
        acc_ref[...] = jnp.zeros_like(acc_ref)

    st_ref[...] = _gelu_x2(_dot(dn_ref[...], h2t_ref[...]).astype(BF16))
    zero = jnp.zeros((n, LANES), BF16)
    pair = 2
    for g in range(h2t_ref.shape[1] // LANES):
        lanes = slice(g * LANES, (g + 1) * LANES)
        for i1 in range(0, rows_per_chunk, pair):
            w = [None] * pair
            for hd in range(PEER_HEADS):
                r2 = pltpu.bitcast(r2_ref[hd, :, lanes], BF16)
                e2 = pltpu.bitcast(e2_ref[hd, :, lanes], BF16)
                for u in range(pair):
                    cnt = _bf16_rows(l1_ref[hd, i1 + u:i1 + u + 1, lanes], n)
                    e1 = _bf16_rows(e1_ref[hd, i1 + u:i1 + u + 1, lanes], n)
                    wh = jnp.where(r2 < cnt, e2, zero) * e1
                    w[u] = wh if w[u] is None else w[u] + wh
            for u in range(pair):
                rows = slice((i1 + u) * n, (i1 + u + 1) * n)
                pw_ref[rows, lanes] = st_ref[rows, lanes] * w[u]
    acc_ref[...] += _dot(up_ref[...], pw_ref[...])

    @pl.when(j == pl.num_programs(2) - 1)
    def _():
        gt2 = mod_ref[0, 5:6, :]
        x2 = x1_ref[0] + gt2 * acc_ref[...].T
        inv = lax.rsqrt(jnp.mean(x2 * x2, axis=-1, keepdims=True) + EPS)
        o_ref[0] = (x2 * inv) * gf_ref[...]


def _peer(h2t, down, up_t, r2, e2, l1, e1, x1, mod3, g_final, tt, ec):
    b, s, d = x1.shape
    ne = down.shape[0]
    rpc = ec // PEER_N_KEYS
    spb = s // tt
    tok = lambda bi, si, j: (0, 0, bi * spb + si)
    gate_tab = pl.BlockSpec((PEER_HEADS, rpc, tt), lambda bi, si, j: (0, j, bi * spb + si))
    return pl.pallas_call(
        functools.partial(_peer_kernel, rows_per_chunk=rpc),
        grid=(b, spb, ne // ec),
        in_specs=[pl.BlockSpec((d, tt), lambda bi, si, j: (0, bi * spb + si)),
                  pl.BlockSpec((ec, d), lambda bi, si, j: (j, 0)),
                  pl.BlockSpec((d, ec), lambda bi, si, j: (0, j)),
                  pl.BlockSpec((PEER_HEADS, PEER_N_KEYS // 2, tt), tok),
                  pl.BlockSpec((PEER_HEADS, PEER_N_KEYS // 2, tt), tok),
                  gate_tab, gate_tab,
                  pl.BlockSpec((1, tt, d), lambda bi, si, j: (bi, si, 0)),
                  pl.BlockSpec((1, N_MOD, d), lambda bi, si, j: (bi, 0, 0)),
                  pl.BlockSpec((1, d), lambda bi, si, j: (0, 0))],
        out_specs=pl.BlockSpec((1, tt, d), lambda bi, si, j: (bi, si, 0)),
        out_shape=jax.ShapeDtypeStruct((b, s, d), F32),
        scratch_shapes=[pltpu.VMEM((d, tt), F32), pltpu.VMEM((ec, tt), BF16), pltpu.VMEM((ec, tt), BF16)],
        compiler_params=_cparams("arbitrary", "arbitrary", "arbitrary"),
        name="peer",
    )(h2t, down, up_t, r2, e2, l1, e1, x1, mod3, g_final)


def _pad_heads(w):
    d = w.shape[0]
    w = w.reshape(d, N_HEADS, HEAD_DIM)
    return jnp.pad(w, ((0, 0), (0, 0), (0, LANES - HEAD_DIM))).reshape(d, HP)


def _bias_constants():
    pq = np.zeros((3 * LANES, HP), np.float32)
    pk = np.zeros((3 * LANES, HP), np.float32)
    cq = np.zeros((1, HP), np.float32)
    ck = np.zeros((1, HP), np.float32)
    cv = np.zeros((1, HP), np.float32)
    for h in range(N_HEADS):
        for part in range(3):
            pq[part * LANES + h, h * LANES + LANE_F_Q + part] = 1.0
            pk[part * LANES + h, h * LANES + LANE_ONE_Q + part] = -1.0
            cq[0, h * LANES + LANE_ONE_Q + part] = 1.0
            ck[0, h * LANES + LANE_F_Q + part] = 1.0
        cv[0, h * LANES + LANE_ONE_V] = 1.0
    return (jnp.asarray(pq, BF16), jnp.asarray(pk, BF16), jnp.asarray(cq), jnp.asarray(ck), jnp.asarray(cv))


PROJ_TOKENS = 512
ATTN_Q_TOKENS = 1024
FOX_K_TOKENS = 512
SB_K_TOKENS = 256
TOPK_TOKENS = 512
PEER_TOKENS = 1024
PEER_EXPERTS = 1024


def _tile(n, pref):
    return pref if n % pref == 0 else n


def kernel(x, c, w_ada, b_ada, g_norm_mix, w_in, b_forget, g_out_fox, g_out_sb, w_out, g_norm_ffn,
           w_query, sub_keys, expert_down, expert_up, g_final):
    b, s, d = x.shape
    assert w_ada.shape[0] == 1, "single-layer block: the final RMSNorm is fused into the PEER kernel"
    fw = N_HEADS * HEAD_DIM
    scale = HEAD_DIM ** -0.5 * LOG2E
    pq, pk, cq, ck, cv = _bias_constants()
    tm = _tile(s, PROJ_TOKENS)
    tq = _tile(s, ATTN_Q_TOKENS)

    mod = _ada(jnp.pad(c, ((0, -b % 8), (0, 0))), w_ada[0], b_ada[0][None, :])[:b]
    mod3 = mod.reshape(b, N_MOD, d)

    wi = w_in[0]
    o1 = 3 * fw
    o2 = o1 + N_HEADS
    w_all = jnp.concatenate([
        _pad_heads(wi[:, 0:fw] * scale), _pad_heads(wi[:, fw:2 * fw]), _pad_heads(wi[:, 2 * fw:o1]),
        wi[:, o2:o2 + fw] * scale, wi[:, o2 + fw:o2 + 3 * fw]], axis=1).astype(BF16)
    wf = jnp.pad(wi[:, o1:o2], ((0, 0), (0, LANES - N_HEADS)))
    wf_hi = wf.astype(BF16)
    wf_lo = (wf - wf_hi.astype(F32)).astype(BF16)
    b_f = jnp.pad(b_forget[0], (0, LANES - N_HEADS))[None, :]
    fq, fk, fv, sq, sk, sv = _inproj(x, mod3, g_norm_mix[0][None, :], w_all, wf_hi, wf_lo, b_f,
                                     pq, pk, cq, ck, cv, tm)

    g_fox = jnp.pad(g_out_fox[0], (0, LANES - HEAD_DIM))[None, :]
    g_sb = jnp.tile(g_out_sb[0], LANES // HEAD_DIM)[None, :]
    fo = _fox(fq, fk, fv, g_fox, tq, _tile(s, FOX_K_TOKENS))
    so = _sb(sq, sk, sv, g_sb, tq, _tile(s, SB_K_TOKENS))

    wo = w_out[0]
    w_out_fox = jnp.pad(wo[:fw].reshape(N_HEADS, HEAD_DIM, d),
                        ((0, 0), (0, LANES - HEAD_DIM), (0, 0))).reshape(HP, d)
    w_out_p = jnp.concatenate([w_out_fox, wo[fw:]], axis=0).astype(BF16)
    wc = _wc(sub_keys[0].reshape(PEER_HEADS * 2, PEER_N_KEYS, LANES), w_query[0])
    x1, h2t, scores_t = _outproj(x, fo, so, mod3, g_norm_ffn[0][None, :], w_out_p, wc.astype(BF16), tm)

    r2, e2, l1, e1 = _topk(scores_t, _tile(b * s, TOPK_TOKENS))
    return _peer(h2t, expert_down[0].astype(BF16), expert_up[0].astype(BF16).T, r2, e2, l1, e1,
                 x1, mod3, g_final[None, :], _tile(s, PEER_TOKENS), PEER_EXPERTS)
```
